```python
import numpy as np
import jax
import jax.numpy as jnp
from jax import lax

D_MODEL = 1024
BATCH = 2
SEQ = 16384
DEPTH = 4

CTX_LEN = 256
GRID_W = 64
D_A = 1024
H_A = 4
DH_A = D_A // H_A
D_B = 1024
H_B = 8
DH_B = D_B // H_B
CHUNK_A = 64
CHUNK_B = 16
CONV_K = 3
EPS = 1e-6
NEG = -1e30
MLSTM_F_BIAS = 3.0
IN_SIZES = (2 * D_A, D_A, D_A, D_A, 4 * H_A, D_B, D_B, D_B, D_B, D_B, D_MODEL, D_MODEL)
N_IN = sum(IN_SIZES)

kernel_name = "hybrid_mlstm_hgrn2_diffusion_prefix"


def rmsnorm(x, g):
    xf = x.astype(jnp.float32)
    y = xf * lax.rsqrt(jnp.mean(xf * xf, axis=-1, keepdims=True) + EPS)
    return (y * g.astype(jnp.float32)).astype(x.dtype)


def head_rmsnorm(h, g):
    y = h * lax.rsqrt(jnp.mean(h * h, axis=-1, keepdims=True) + EPS)
    return y.reshape(h.shape[0], h.shape[1], -1) * g.astype(jnp.float32)


def split_in(p):
    idx = []
    acc = 0
    for s in IN_SIZES[:-1]:
        acc += s
        idx.append(acc)
    return jnp.split(p, idx, axis=-1)


def to_chunks(a, L):
    Bn, T, H = a.shape[:3]
    a = a.reshape((Bn, T // L, L, H) + a.shape[3:])
    return jnp.transpose(a, (1, 0, 3, 2) + tuple(range(4, a.ndim)))


def from_chunks(h):
    NC, Bn, H, L, d = h.shape
    return jnp.transpose(h, (1, 0, 3, 2, 4)).reshape(Bn, NC * L, H, d)


def mlstm_chunk_step(carry, xs):
    C, n, m = carry
    q, k, v, li, lf = xs
    L = q.shape[2]
    causal = jnp.tril(jnp.ones((L, L), dtype=bool))
    b = jnp.cumsum(lf, axis=-1)
    D = jnp.where(causal, b[..., :, None] - b[..., None, :] + li[..., None, :], NEG)
    inter = b + m[..., None]
    m_t = jnp.maximum(inter, jnp.max(D, axis=-1))
    w = jnp.einsum('bhtd,bhsd->bhts', q, k) * jnp.exp(D - m_t[..., None])
    a_inter = jnp.exp(inter - m_t)
    num = a_inter[..., None] * jnp.einsum('bhtd,bhde->bhte', q, C) + jnp.einsum('bhts,bhse->bhte', w, v)
    den = a_inter * jnp.einsum('bhtd,bhd->bht', q, n) + jnp.sum(w, axis=-1)
    h = num / jnp.maximum(jnp.abs(den), jnp.exp(-m_t))[..., None]
    bL = b[..., -1]
    ws = bL[..., None] - b + li
    m_new = jnp.maximum(bL + m, jnp.max(ws, axis=-1))
    decay = jnp.exp(bL + m - m_new)
    ws_e = jnp.exp(ws - m_new[..., None])
    kw = k * ws_e[..., None]
    C_new = decay[..., None, None] * C + jnp.einsum('bhsd,bhse->bhde', kw, v)
    n_new = decay[..., None] * n + jnp.sum(kw, axis=2)
    return (C_new, n_new, m_new), h


def mlstm_scan(q, k, v, li, lf, state):
    xs = tuple(to_chunks(a, CHUNK_A) for a in (q, k, v, li, lf))
    state, h = lax.scan(mlstm_chunk_step, state, xs)
    return from_chunks(h), state


def hgrn_chunk_step(S, xs):
    q, v, k, lf = xs
    L = q.shape[2]
    causal = jnp.tril(jnp.ones((L, L), dtype=bool))[:, :, None]
    g = jnp.cumsum(lf, axis=2)
    o_inter = jnp.einsum('bhtc,bhce->bhte', q * jnp.exp(g), S)
    rel = jnp.where(causal, g[:, :, :, None, :] - g[:, :, None, :, :], 0.0)
    dec = jnp.where(causal, jnp.exp(rel), 0.0)
    a = jnp.einsum('bhtsc,bhsc->bhts', q[:, :, :, None, :] * dec, k)
    o = o_inter + jnp.einsum('bhts,bhse->bhte', a, v)
    gL = g[:, :, -1:, :]
    S_new = jnp.exp(gL[:, :, 0, :])[..., None] * S + jnp.einsum('bhsc,bhse->bhce', k * jnp.exp(gL - g), v)
    return S_new, o


def hgrn_scan(q, v, k, lf, state):
    xs = tuple(to_chunks(a, CHUNK_B) for a in (q, v, k, lf))
    state, o = lax.scan(hgrn_chunk_step, state, xs)
    return from_chunks(o), state


def bidirectional(scan_fn, init, ctx_shared, ctx_fwd, ctx_bwd, lat_shared, lat_fwd, lat_bwd, with_ctx_out):
    flip = lambda ts: tuple(jnp.flip(t, axis=1) for t in ts)
    hc_f, s_f = scan_fn(*ctx_shared, *ctx_fwd, init)
    hc_b, s_b = scan_fn(*flip(ctx_shared), *flip(ctx_bwd), init)
    h_f, _ = scan_fn(*lat_shared, *lat_fwd, s_f)
    h_b, _ = scan_fn(*flip(lat_shared), *flip(lat_bwd), s_b)
    h = h_f + jnp.flip(h_b, axis=1)
    hc = hc_f + jnp.flip(hc_b, axis=1) if with_ctx_out else None
    return h, hc


def conv_grid(u, w, b, rows):
    Bn, T, Cn = u.shape
    y = lax.conv_general_dilated(u.reshape(Bn, rows, GRID_W, Cn), w[:, :, None, :].astype(u.dtype),
                                 (1, 1), 'SAME', dimension_numbers=('NHWC', 'HWIO', 'NHWC'),
                                 feature_group_count=Cn)
    return y.reshape(Bn, T, Cn) + b.astype(u.dtype)


def conv_seq(u, w, b):
    Cn = u.shape[-1]
    y = lax.conv_general_dilated(u, w[1][:, None, :].astype(u.dtype), (1,), 'SAME',
                                 dimension_numbers=('NWC', 'WIO', 'NWC'), feature_group_count=Cn)
    return y + b.astype(u.dtype)


def prep(p, conv_fn, lb):
    f32 = jnp.float32
    qk, v, o, z_a, g_a, q_b, i_b, f_bf, f_bb, z_b, m_a, m_b = split_in(p)
    Bn, T = p.shape[:2]
    qk = jax.nn.silu(conv_fn(qk)).astype(f32)
    q_a = qk[..., :D_A].reshape(Bn, T, H_A, DH_A)
    k_a = qk[..., D_A:].reshape(Bn, T, H_A, DH_A) * (DH_A ** -0.5)
    v_a = v.astype(f32).reshape(Bn, T, H_A, DH_A)
    g_a = g_a.astype(f32).reshape(Bn, T, 4, H_A)
    a_fwd = (g_a[:, :, 0], jax.nn.log_sigmoid(g_a[:, :, 1]))
    a_bwd = (g_a[:, :, 2], jax.nn.log_sigmoid(g_a[:, :, 3]))

    def decay(f_raw):
        fr = f_raw.astype(f32)
        log_f = jax.nn.log_sigmoid(fr) + jnp.log1p(lb * jnp.exp(-fr))
        k = (1.0 - lb) * jax.nn.sigmoid(-fr)
        return (k.reshape(Bn, T, H_B, DH_B), log_f.reshape(Bn, T, H_B, DH_B))

    b_shared = (q_b.astype(f32).reshape(Bn, T, H_B, DH_B), i_b.astype(f32).reshape(Bn, T, H_B, DH_B))
    return ((q_a, k_a, v_a), a_fwd, a_bwd, b_shared, decay(f_bf), decay(f_bb), (o, z_a, z_b, m_a, m_b))


def merge(h_a, h_b, post, g_head_a, g_head_b, w_a, w_b, w_out, dtype):
    o, z_a, z_b, m_a, m_b = post
    Bn, T = h_a.shape[:2]
    f32 = jnp.float32
    o_gate = jax.nn.sigmoid(o.astype(f32)).reshape(Bn, T, H_A, DH_A)
    y_a = head_rmsnorm(h_a * o_gate, g_head_a) * jax.nn.silu(z_a.astype(f32))
    y_b = head_rmsnorm(h_b, g_head_b) * jax.nn.silu(z_b.astype(f32))
    y = (jax.nn.sigmoid(m_a) * (y_a.astype(dtype) @ w_a)
         + jax.nn.sigmoid(m_b) * (y_b.astype(dtype) @ w_b))
    return y @ w_out


def trunk_layer(x, xc, c, c_ctx, w_ada, b_ada, g_norm, w_in, b_in, w_conv, b_conv, lb,
                g_head_a, g_head_b, w_a, w_b, w_out, rows, with_ctx_out):
    shift, scale, gate = jnp.split(jax.nn.silu(c) @ w_ada + b_ada, 3, axis=-1)
    shift_c, scale_c, gate_c = jnp.split(jax.nn.silu(c_ctx) @ w_ada + b_ada, 3, axis=-1)
    h = rmsnorm(x, g_norm) * (1 + scale[:, None]) + shift[:, None]
    hc = rmsnorm(xc, g_norm) * (1 + scale_c) + shift_c
    lat = prep(h @ w_in + b_in, lambda u: conv_grid(u, w_conv, b_conv, rows), lb)
    con = prep(hc @ w_in + b_in, lambda u: conv_seq(u, w_conv, b_conv), lb)
    Bn = x.shape[0]
    f32 = jnp.float32
    init_a = (jnp.zeros((Bn, H_A, DH_A, DH_A), f32), jnp.zeros((Bn, H_A, DH_A), f32), jnp.zeros((Bn, H_A), f32))
    init_b = jnp.zeros((Bn, H_B, DH_B, DH_B), f32)
    h_a, hc_a = bidirectional(mlstm_scan, init_a, con[0], con[1], con[2], lat[0], lat[1], lat[2], with_ctx_out)
    h_b, hc_b = bidirectional(hgrn_scan, init_b, con[3], con[4], con[5], lat[3], lat[4], lat[5], with_ctx_out)
    x = x + gate[:, None] * merge(h_a, h_b, lat[6], g_head_a, g_head_b, w_a, w_b, w_out, x.dtype)
    if with_ctx_out:
        xc = xc + gate_c * merge(hc_a, hc_b, con[6], g_head_a, g_head_b, w_a, w_b, w_out, xc.dtype)
    return x, xc


def setup_inputs(seed: int = 0) -> dict:
    key = jax.random.key(seed)
    ks = jax.random.split(key, 20)
    nrm = jax.random.normal
    f_off = 2 * D_A + 3 * D_A + H_A
    fb_off = 2 * D_A + 3 * D_A + 3 * H_A
    b_in = 0.01 * nrm(ks[8], (DEPTH, N_IN), jnp.float32)
    b_in = b_in.at[:, f_off:f_off + H_A].add(MLSTM_F_BIAS).at[:, fb_off:fb_off + H_A].add(MLSTM_F_BIAS)
    return {
        "x": nrm(ks[0], (BATCH, SEQ, D_MODEL), jnp.float32),
        "c": nrm(ks[1], (BATCH, D_MODEL), jnp.float32),
        "ctx": nrm(ks[2], (BATCH, CTX_LEN, D_MODEL), jnp.float32),
        "c_ctx": nrm(ks[3], (D_MODEL,), jnp.float32),
        "w_ada": nrm(ks[4], (DEPTH, D_MODEL, 3 * D_MODEL), jnp.float32) * (0.5 * D_MODEL ** -0.5),
        "b_ada": 0.01 * nrm(ks[5], (DEPTH, 3 * D_MODEL), jnp.float32),
        "g_norm": 1.0 + 0.01 * nrm(ks[6], (DEPTH, D_MODEL), jnp.float32),
        "w_in": nrm(ks[7], (DEPTH, D_MODEL, N_IN), jnp.float32) * (D_MODEL ** -0.5),
        "b_in": b_in,
        "w_conv": nrm(ks[9], (DEPTH, CONV_K, CONV_K, 2 * D_A), jnp.float32) * (1.0 / CONV_K),
        "b_conv": 0.01 * nrm(ks[10], (DEPTH, 2 * D_A), jnp.float32),
        "lb_logits": 0.1 * nrm(ks[11], (DEPTH, D_B), jnp.float32),
        "g_head_a": 1.0 + 0.01 * nrm(ks[12], (DEPTH, D_A), jnp.float32),
        "g_head_b": 1.0 + 0.01 * nrm(ks[13], (DEPTH, D_B), jnp.float32),
        "w_a": nrm(ks[14], (DEPTH, D_A, D_MODEL), jnp.float32) * (D_A ** -0.5),
        "w_b": nrm(ks[15], (DEPTH, D_B, D_MODEL), jnp.float32) * (D_B ** -0.5),
        "w_out": nrm(ks[16], (DEPTH, D_MODEL, D_MODEL), jnp.float32) * (D_MODEL ** -0.5),
        "g_final": 1.0 + 0.01 * nrm(ks[17], (D_MODEL,), jnp.float32),
    }


def reference(x, c, ctx, c_ctx, w_ada, b_ada, g_norm, w_in, b_in, w_conv, b_conv, lb_logits,
              g_head_a, g_head_b, w_a, w_b, w_out, g_final):
    rows = x.shape[1] // GRID_W
    p = jax.nn.softmax(lb_logits.astype(jnp.float32), axis=0)
    lbs = jnp.cumsum(p, axis=0) - p[0:1]
    xc = ctx
    for l in range(DEPTH):
        x, xc = trunk_layer(x, xc, c, c_ctx, w_ada[l], b_ada[l], g_norm[l], w_in[l], b_in[l],
                            w_conv[l], b_conv[l], lbs[l], g_head_a[l], g_head_b[l],
                            w_a[l], w_b[l], w_out[l], rows, l < DEPTH - 1)
    return rmsnorm(x, g_final)
```

```python
import functools

import numpy as np
import jax
import jax.numpy as jnp
from jax import lax
from jax.experimental import pallas as pl
from jax.experimental.pallas import tpu as pltpu

F32 = jnp.float32
BF16 = jnp.bfloat16

H_A = 4
H_B = 8
GRID_W = 64
EPS = 1e-6
NEG = -1e30
N_GATES = 4 * H_A
GATE_PAD = 128
L_A = 256
L_B = 128
CONV_ROWS = 256
CONV_HALO = 128
CONV_COLS = 512
V7X_VMEM_LIMIT = 56 * 1024 * 1024
COL_QK, COL_V, COL_O, COL_ZA, COL_QB, COL_IB, COL_FF, COL_FB, COL_ZB, COL_MA, COL_MB = (
    0, 2, 3, 4, 5, 6, 7, 8, 9, 10, 11)
N_COL_BLOCKS = 12


def _cparams(semantics):
    return pltpu.CompilerParams(dimension_semantics=semantics, vmem_limit_bytes=V7X_VMEM_LIMIT)


def _sigmoid(x):
    return 1.0 / (1.0 + jnp.exp(-x))


def _split3(x):
    hi = x.astype(BF16)
    r1 = x - hi.astype(F32)
    mid = r1.astype(BF16)
    lo = (r1 - mid.astype(F32)).astype(BF16)
    return hi, mid, lo


def _cumsum_mm(tri, x):
    hi, mid, lo = _split3(x)
    d = lambda a: jnp.dot(tri, a, preferred_element_type=F32)
    return d(hi) + d(mid) + d(lo)


def _dot_nt(a, b):
    return lax.dot_general(a, b, (((1,), (1,)), ((), ())), preferred_element_type=F32)


def _dot_tn(a, b):
    return lax.dot_general(a, b, (((0,), (0,)), ((), ())), preferred_element_type=F32)


def _ada_kernel(cc_ref, w_ref, b_ref, out_ref):
    s = cc_ref[...]
    s = s * _sigmoid(s)
    w = w_ref[0]
    s_hi = s.astype(BF16)
    s_lo = (s - s_hi.astype(F32)).astype(BF16)
    w_hi = w.astype(BF16)
    w_lo = (w - w_hi.astype(F32)).astype(BF16)
    d = lambda a, b: jnp.dot(a, b, preferred_element_type=F32)
    out_ref[0] = d(s_hi, w_hi) + d(s_lo, w_hi) + d(s_hi, w_lo) + b_ref[0]


def _ada(cc, w_ada, b_ada):
    depth, d, n3 = w_ada.shape
    tn = 512
    return pl.pallas_call(
        _ada_kernel,
        out_shape=jax.ShapeDtypeStruct((depth, 8, n3), F32),
        grid=(depth, n3 // tn),
        in_specs=[pl.BlockSpec((8, d), lambda l, j: (0, 0)),
                  pl.BlockSpec((1, d, tn), lambda l, j: (l, 0, j)),
                  pl.BlockSpec((1, 1, tn), lambda l, j: (l, 0, j))],
        out_specs=pl.BlockSpec((1, 8, tn), lambda l, j: (l, 0, j)),
        compiler_params=_cparams(("arbitrary", "arbitrary")),
        name="ada",
    )(cc, w_ada, b_ada.reshape(depth, 1, n3))


def _lbs_kernel(lg_ref, out_ref):
    x = lg_ref[...]
    depth = x.shape[0]
    m = jnp.max(x, axis=0, keepdims=True)
    e = jnp.exp(x - m)
    p = e / jnp.sum(e, axis=0, keepdims=True)
    acc = jnp.zeros_like(p[0:1])
    for l in range(depth):
        acc = acc + p[l:l + 1]
        out_ref[l:l + 1, :] = acc - p[0:1]


def _lbs(lb_logits):
    return pl.pallas_call(
        _lbs_kernel, out_shape=jax.ShapeDtypeStruct(lb_logits.shape, F32), name="lbs",
    )(lb_logits.astype(F32))


def _inproj_kernel(x_ref, mod_ref, gn_ref, w_ref, b_ref, wg_ref, bg_ref, p_ref, g_ref, h_scr,
                   *, ctx_len, tm, n_batch):
    b = pl.program_id(0)
    i = pl.program_id(1)
    j = pl.program_id(2)

    @pl.when(j == 0)
    def _():
        x = x_ref[0]
        d = x.shape[-1]
        ms = jnp.mean(x * x, axis=-1, keepdims=True)
        y = x * lax.rsqrt(ms + EPS) * gn_ref[...]
        row = i * tm + lax.broadcasted_iota(jnp.int32, (tm, 1), 0)
        is_ctx = row < ctx_len
        mb = mod_ref[pl.ds(b, 1), :]
        mc = mod_ref[pl.ds(n_batch, 1), :]
        shift = jnp.where(is_ctx, mc[:, 0:d], mb[:, 0:d])
        scale = jnp.where(is_ctx, mc[:, d:2 * d], mb[:, d:2 * d])
        h = y * (1.0 + scale) + shift
        hi = h.astype(BF16)
        h_scr[...] = hi
        lo = (h - hi.astype(F32)).astype(BF16)
        dd = lambda a, bb: jnp.dot(a, bb, preferred_element_type=F32)
        g_ref[0] = dd(hi, wg_ref[0]) + dd(lo, wg_ref[0]) + dd(hi, wg_ref[1]) + bg_ref[...]

    p_ref[0] = jnp.dot(h_scr[...], w_ref[...], preferred_element_type=F32) + b_ref[...]


def _pick_tile(n, candidates):
    for c in candidates:
        if n % c == 0:
            return c
    raise ValueError(f"no tile for {n}")


def _inproj(xall, mod_l, gn, w_main, b_main, wg, bg, ctx_len):
    nb, tb, d = xall.shape
    n = w_main.shape[1]
    tm = _pick_tile(tb, (1280, 640, 256))
    tn = 1024
    kern = functools.partial(_inproj_kernel, ctx_len=ctx_len, tm=tm, n_batch=nb)
    return pl.pallas_call(
        kern,
        out_shape=(jax.ShapeDtypeStruct((nb, tb, n), F32),
                   jax.ShapeDtypeStruct((nb, tb, GATE_PAD), F32)),
        grid=(nb, tb // tm, n // tn),
        in_specs=[pl.BlockSpec((1, tm, d), lambda b, i, j: (b, i, 0)),
                  pl.BlockSpec((8, 3 * d), lambda b, i, j: (0, 0)),
                  pl.BlockSpec((1, d), lambda b, i, j: (0, 0)),
                  pl.BlockSpec((d, tn), lambda b, i, j: (0, j)),
                  pl.BlockSpec((1, tn), lambda b, i, j: (0, j)),
                  pl.BlockSpec((2, d, GATE_PAD), lambda b, i, j: (0, 0, 0)),
                  pl.BlockSpec((1, GATE_PAD), lambda b, i, j: (0, 0))],
        out_specs=(pl.BlockSpec((1, tm, tn), lambda b, i, j: (b, i, j)),
                   pl.BlockSpec((1, tm, GATE_PAD), lambda b, i, j: (b, i, 0))),
        scratch_shapes=[pltpu.VMEM((tm, d), BF16)],
        compiler_params=_cparams(("arbitrary", "arbitrary", "arbitrary")),
        name="inproj",
    )(xall, mod_l, gn, w_main, b_main, wg, bg)


def _conv_kernel(prev_ref, cur_ref, next_ref, w_ref, bc_ref, sc_ref, out_ref, e0, el, er,
                 *, n_tiles, tr, halo):
    i = pl.program_id(1)
    is_ctx = i == 0
    prev_ok = i >= 2
    next_ok = jnp.logical_and(i >= 1, i < n_tiles - 1)
    tc = cur_ref.shape[-1]
    ext = tr + 2 * halo
    e0[0:halo, :] = jnp.where(prev_ok, prev_ref[0], 0.0)
    e0[halo:halo + tr, :] = cur_ref[0]
    e0[halo + tr:ext, :] = jnp.where(next_ok, next_ref[0], 0.0)
    full = e0[...]
    col = lax.broadcasted_iota(jnp.int32, (ext, 1), 0) % GRID_W
    el[...] = jnp.where(jnp.logical_or(is_ctx, col != GRID_W - 1), full, 0.0)
    er[...] = jnp.where(jnp.logical_or(is_ctx, col != 0), full, 0.0)
    w = w_ref[...]
    acc = jnp.zeros((tr, tc), F32)
    srcs = (el, e0, er)
    for dr in (-1, 0, 1):
        for dc in (-1, 0, 1):
            wt = w[(dr + 1) * 3 + (dc + 1):(dr + 1) * 3 + (dc + 1) + 1, :]
            if dr != 0:
                wt = jnp.where(is_ctx, 0.0, wt)
            start = halo + GRID_W * dr + dc
            acc = acc + wt * srcs[dc + 1][start:start + tr, :]
    y = acc + bc_ref[...]
    out_ref[0] = (y * _sigmoid(y) * sc_ref[...]).astype(out_ref.dtype)


def _conv_silu(p, w9, bconv, scale, ctx_len, n_qk):
    nb, tb, _ = p.shape
    tr, halo, tc = CONV_ROWS, CONV_HALO, CONV_COLS
    assert ctx_len == tr and tb % tr == 0 and tr % GRID_W == 0 and halo % GRID_W == 0
    n_tiles = tb // tr
    per = tr // halo
    n_halo = tb // halo
    kern = functools.partial(_conv_kernel, n_tiles=n_tiles, tr=tr, halo=halo)
    return pl.pallas_call(
        kern,
        out_shape=jax.ShapeDtypeStruct((nb, tb, n_qk), BF16),
        grid=(nb, n_tiles, n_qk // tc),
        in_specs=[pl.BlockSpec((1, halo, tc), lambda b, i, j: (b, jnp.maximum(i * per - 1, 0), j)),
                  pl.BlockSpec((1, tr, tc), lambda b, i, j: (b, i, j)),
                  pl.BlockSpec((1, halo, tc), lambda b, i, j: (b, jnp.minimum((i + 1) * per, n_halo - 1), j)),
                  pl.BlockSpec((9, tc), lambda b, i, j: (0, j)),
                  pl.BlockSpec((1, tc), lambda b, i, j: (0, j)),
                  pl.BlockSpec((1, tc), lambda b, i, j: (0, j))],
        out_specs=pl.BlockSpec((1, tr, tc), lambda b, i, j: (b, i, j)),
        scratch_shapes=[pltpu.VMEM((tr + 2 * halo, tc), F32)] * 3,
        compiler_params=_cparams(("arbitrary", "arbitrary", "arbitrary")),
        name="conv_silu",
    )(p, p, p, w9, bconv, scale)


def _mlstm_kernel(qkf_ref, vf_ref, gf_ref, qkb_ref, vb_ref, gb_ref, hf_ref, hb_ref,
                  c_scr, n_scr, m_scr, *, la, dh):
    @pl.when(pl.program_id(1) == 0)
    def _():
        c_scr[...] = jnp.zeros_like(c_scr)
        n_scr[...] = jnp.zeros_like(n_scr)
        m_scr[...] = jnp.zeros_like(m_scr)

    r = lax.broadcasted_iota(jnp.int32, (la, la), 0)
    s = lax.broadcasted_iota(jnp.int32, (la, la), 1)
    dirs = ((qkf_ref, vf_ref, gf_ref, hf_ref), (qkb_ref, vb_ref, gb_ref, hb_ref))
    for d, (qk_ref, v_ref, g_ref, h_ref) in enumerate(dirs):
        seen = (s <= r) if d == 0 else (s >= r)
        tri = jnp.where(seen, 1.0, 0.0).astype(BF16)
        g = g_ref[0]
        lsig = jnp.minimum(g, 0.0) - jnp.log(1.0 + jnp.exp(-jnp.abs(g)))
        bc = _cumsum_mm(tri, lsig)
        g_t = g.T
        bc_t = bc.T
        last = la - 1 if d == 0 else 0
        for h in range(H_A):
            u = d * H_A + h
            ci = 2 * d * H_A + h
            cf = ci + H_A
            li_col = g[:, ci:ci + 1]
            b_col = bc[:, cf:cf + 1]
            li_row = g_t[ci:ci + 1, :]
            b_row = bc_t[cf:cf + 1, :]
            m = m_scr[u:u + 1, 0:1]
            dmat = jnp.where(seen, b_col - b_row + li_row, NEG)
            inter = b_col + m
            m_t = jnp.maximum(inter, jnp.max(dmat, axis=-1, keepdims=True))
            q = qk_ref[0, :, h * dh:(h + 1) * dh]
            k = qk_ref[0, :, (H_A + h) * dh:(H_A + h + 1) * dh]
            v = v_ref[0, :, h * dh:(h + 1) * dh].astype(BF16)
            w = _dot_nt(q, k) * jnp.exp(dmat - m_t)
            a_inter = jnp.exp(inter - m_t)
            c_st = c_scr[u]
            n_st = n_scr[u]
            num = (a_inter * jnp.dot(q, c_st.astype(BF16), preferred_element_type=F32)
                   + jnp.dot(w.astype(BF16), v, preferred_element_type=F32))
            qn = jnp.sum(q.astype(F32) * n_st, axis=-1, keepdims=True)
            den = a_inter * qn + jnp.sum(w, axis=-1, keepdims=True)
            h_ref[0, :, h * dh:(h + 1) * dh] = num / jnp.maximum(jnp.abs(den), jnp.exp(-m_t))
            b_l = b_col[last:last + 1, :]
            ws = b_l - b_col + li_col
            m_new = jnp.maximum(b_l + m, jnp.max(ws, axis=0, keepdims=True))
            decay = jnp.exp(b_l + m - m_new)
            kw = k.astype(F32) * jnp.exp(ws - m_new)
            c_scr[u] = decay * c_st + _dot_tn(kw.astype(BF16), v)
            n_scr[u] = decay * n_st + jnp.sum(kw, axis=0, keepdims=True)
            m_scr[u:u + 1, :] = jnp.broadcast_to(m_new, (1, m_scr.shape[-1]))


def _bwd_chunk(c, n_ctx_chunks, n_chunks):
    return jnp.where(c < n_ctx_chunks, n_ctx_chunks - 1 - c, n_chunks - 1 + n_ctx_chunks - c)


def _mlstm(qk, p, g, ctx_len, d_a):
    nb, tb, _ = qk.shape
    la = L_A
    dh = d_a // H_A
    assert ctx_len % la == 0 and tb % la == 0
    nc, ncc = tb // la, ctx_len // la
    bw = functools.partial(_bwd_chunk, n_ctx_chunks=ncc, n_chunks=nc)
    kern = functools.partial(_mlstm_kernel, la=la, dh=dh)
    hshape = jax.ShapeDtypeStruct((nb, tb, d_a), F32)
    return pl.pallas_call(
        kern,
        out_shape=(hshape, hshape),
        grid=(nb, nc),
        in_specs=[pl.BlockSpec((1, la, 2 * d_a), lambda b, c: (b, c, 0)),
                  pl.BlockSpec((1, la, d_a), lambda b, c: (b, c, COL_V)),
                  pl.BlockSpec((1, la, GATE_PAD), lambda b, c: (b, c, 0)),
                  pl.BlockSpec((1, la, 2 * d_a), lambda b, c: (b, bw(c), 0)),
                  pl.BlockSpec((1, la, d_a), lambda b, c: (b, bw(c), COL_V)),
                  pl.BlockSpec((1, la, GATE_PAD), lambda b, c: (b, bw(c), 0))],
        out_specs=(pl.BlockSpec((1, la, d_a), lambda b, c: (b, c, 0)),
                   pl.BlockSpec((1, la, d_a), lambda b, c: (b, bw(c), 0))),
        scratch_shapes=[pltpu.VMEM((2 * H_A, dh, dh), F32),
                        pltpu.VMEM((2 * H_A, 1, dh), F32),
                        pltpu.VMEM((2 * H_A, 128), F32)],
        compiler_params=_cparams(("arbitrary", "arbitrary")),
        name="mlstm",
    )(qk, p, g, qk, p, g)


def _hgrn_levels(l):
    return [1 << j for j in range(int(np.log2(l)))]


def _hgrn_masks(l):
    r = np.arange(l)[:, None]
    s = np.arange(l)[None, :]
    fwd = []
    for h in _hgrn_levels(l):
        same = (r // (2 * h)) == (s // (2 * h))
        fwd.append(same & ((r % (2 * h)) >= h) & ((s % (2 * h)) < h))
    fwd.append(r == s)
    fwd = np.stack(fwd).astype(np.float32)
    return np.stack([fwd, fwd.transpose(0, 2, 1)])


def _hgrn_tri(l):
    r = np.arange(l)[:, None]
    s = np.arange(l)[None, :]
    return np.stack([(s <= r), (s >= r)]).astype(np.float32)


def _hgrn_ref_rows(g_scr, lanes, l, h, d):
    off = h - 1 if d == 0 else h
    if h >= 8:
        pieces = [jnp.broadcast_to(g_scr[k * 2 * h + off:k * 2 * h + off + 1, lanes], (2 * h, 128))
                  for k in range(l // (2 * h))]
        return jnp.concatenate(pieces, axis=0) if len(pieces) > 1 else pieces[0]
    assert h in (2, 4)
    sub = lax.broadcasted_iota(jnp.int32, (8, 128), 0)
    pieces = []
    for grp in range(l // 8):
        if h == 4:
            pieces.append(jnp.broadcast_to(g_scr[grp * 8 + off:grp * 8 + off + 1, lanes], (8, 128)))
        else:
            lo = jnp.broadcast_to(g_scr[grp * 8 + off:grp * 8 + off + 1, lanes], (8, 128))
            hi = jnp.broadcast_to(g_scr[grp * 8 + 4 + off:grp * 8 + 4 + off + 1, lanes], (8, 128))
            pieces.append(jnp.where(sub < 4, lo, hi))
    return jnp.concatenate(pieces, axis=0)


def _hgrn_kernel(qf_ref, vf_ref, ff_ref, qb_ref, vb_ref, fb_ref, lb_ref, tri_ref, mask_ref,
                 of_ref, ob_ref, st_scr, g_scr, k_scr, f_scr, *, l, dh):
    @pl.when(pl.program_id(1) == 0)
    def _():
        st_scr[...] = jnp.zeros_like(st_scr)

    levels = _hgrn_levels(l)
    n_lev = len(levels)
    parity = lax.broadcasted_iota(jnp.int32, (l, dh), 0) % 2
    dirs = ((qf_ref, vf_ref, ff_ref, of_ref), (qb_ref, vb_ref, fb_ref, ob_ref))
    for d, (q_ref, v_ref, f_ref, o_ref) in enumerate(dirs):
        fr = f_ref[0]
        lbv = lb_ref[...]
        t = jnp.exp(-jnp.abs(fr))
        pos = fr >= 0.0
        rc = 1.0 / (1.0 + t)
        fg = jnp.where(pos, 1.0 + lbv * t, t + lbv) * rc
        f_scr[...] = fg
        k_scr[...] = (1.0 - lbv) * jnp.where(pos, t, 1.0) * rc
        g_scr[...] = _cumsum_mm(tri_ref[d], jnp.log(fg))
        last = l - 1 if d == 0 else 0
        for h in range(H_B):
            u = d * H_B + h
            lanes = slice(h * dh, (h + 1) * dh)
            q = q_ref[0, :, lanes]
            kk = k_scr[:, lanes]
            v = v_ref[0, :, lanes].astype(BF16)
            gh = g_scr[:, lanes]
            a = _dot_nt(q.astype(BF16), kk.astype(BF16)) * mask_ref[d, n_lev]
            for j, hl in enumerate(levels):
                if hl == 1:
                    e = jnp.where(parity == (1 - d), f_scr[:, lanes], 1.0)
                else:
                    gm = _hgrn_ref_rows(g_scr, lanes, l, hl, d)
                    e = jnp.exp(-jnp.abs(gh - gm))
                a = a + _dot_nt((q * e).astype(BF16), (kk * e).astype(BF16)) * mask_ref[d, j]
            st = st_scr[u]
            g_l = g_scr[last:last + 1, lanes]
            o = (_dot_nt((q * jnp.exp(gh)).astype(BF16), st.astype(BF16))
                 + jnp.dot(a.astype(BF16), v, preferred_element_type=F32))
            o_ref[0, :, lanes] = o
            ks = (kk * jnp.exp(g_l - gh)).astype(BF16)
            st_scr[u] = st * jnp.exp(g_l) + _dot_tn(v, ks)


def _hgrn(p, lb, ctx_len, d_b):
    nb, tb, _ = p.shape
    l = L_B
    dh = d_b // H_B
    assert dh == 128 and ctx_len % l == 0 and tb % l == 0
    nc, ncc = tb // l, ctx_len // l
    bw = functools.partial(_bwd_chunk, n_ctx_chunks=ncc, n_chunks=nc)
    kern = functools.partial(_hgrn_kernel, l=l, dh=dh)
    tri = jnp.asarray(_hgrn_tri(l), BF16)
    masks = jnp.asarray(_hgrn_masks(l), F32)
    n_m = masks.shape[1]
    oshape = jax.ShapeDtypeStruct((nb, tb, d_b), F32)
    blk = lambda col, chunk: pl.BlockSpec((1, l, d_b), lambda b, c: (b, chunk(c), col))
    ident = lambda c: c
    return pl.pallas_call(
        kern,
        out_shape=(oshape, oshape),
        grid=(nb, nc),
        in_specs=[blk(COL_QB, ident), blk(COL_IB, ident), blk(COL_FF, ident),
                  blk(COL_QB, bw), blk(COL_IB, bw), blk(COL_FB, bw),
                  pl.BlockSpec((1, d_b), lambda b, c: (0, 0)),
                  pl.BlockSpec((2, l, l), lambda b, c: (0, 0, 0)),
                  pl.BlockSpec((2, n_m, l, l), lambda b, c: (0, 0, 0, 0))],
        out_specs=(pl.BlockSpec((1, l, d_b), lambda b, c: (b, c, 0)),
                   pl.BlockSpec((1, l, d_b), lambda b, c: (b, bw(c), 0))),
        scratch_shapes=[pltpu.VMEM((2 * H_B, dh, dh), F32),
                        pltpu.VMEM((l, d_b), F32),
                        pltpu.VMEM((l, d_b), F32),
                        pltpu.VMEM((l, d_b), F32)],
        compiler_params=_cparams(("arbitrary", "arbitrary")),
        name="hgrn",
    )(p, p, p, p, p, p, lb, tri, masks)


def _head_rms(x, n_heads):
    dh = x.shape[-1] // n_heads
    outs = []
    for h in range(n_heads):
        xh = x[:, h * dh:(h + 1) * dh]
        outs.append(xh * lax.rsqrt(jnp.mean(xh * xh, axis=-1, keepdims=True) + EPS))
    return jnp.concatenate(outs, axis=-1)


def _merge_kernel(x_ref, haf_ref, hab_ref, hbf_ref, hbb_ref, o_ref, za_ref, zb_ref, ma_ref, mb_ref,
                  mod_ref, gha_ref, ghb_ref, wa_ref, wb_ref, wo_ref, out_ref, *, ctx_len, tm, n_batch):
    b = pl.program_id(0)
    i = pl.program_id(1)
    d = x_ref.shape[-1]
    h_a = (haf_ref[0] + hab_ref[0]) * _sigmoid(o_ref[0])
    za = za_ref[0]
    y_a = _head_rms(h_a, H_A) * gha_ref[...] * (za * _sigmoid(za))
    zb = zb_ref[0]
    y_b = _head_rms(hbf_ref[0] + hbb_ref[0], H_B) * ghb_ref[...] * (zb * _sigmoid(zb))
    pa = jnp.dot(y_a.astype(BF16), wa_ref[...], preferred_element_type=F32)
    pb = jnp.dot(y_b.astype(BF16), wb_ref[...], preferred_element_type=F32)
    y = _sigmoid(ma_ref[0]) * pa + _sigmoid(mb_ref[0]) * pb
    br = jnp.dot(y.astype(BF16), wo_ref[...], preferred_element_type=F32)
    row = i * tm + lax.broadcasted_iota(jnp.int32, (tm, 1), 0)
    gate = jnp.where(row < ctx_len, mod_ref[pl.ds(n_batch, 1), 2 * d:3 * d],
                     mod_ref[pl.ds(b, 1), 2 * d:3 * d])
    out_ref[0] = x_ref[0] + gate * br


def _merge(xall, haf, hab, hbf, hbb, p, mod_l, gha, ghb, wa, wb, wo, ctx_len):
    nb, tb, d = xall.shape
    tm = 256
    kern = functools.partial(_merge_kernel, ctx_len=ctx_len, tm=tm, n_batch=nb)
    row = pl.BlockSpec((1, tm, d), lambda b, i: (b, i, 0))
    pcol = lambda col: pl.BlockSpec((1, tm, d), lambda b, i: (b, i, col))
    full = lambda shape: pl.BlockSpec(shape, lambda b, i: (0,) * len(shape))
    return pl.pallas_call(
        kern,
        out_shape=jax.ShapeDtypeStruct(xall.shape, F32),
        grid=(nb, tb // tm),
        in_specs=[row, row, row, row, row,
                  pcol(COL_O), pcol(COL_ZA), pcol(COL_ZB), pcol(COL_MA), pcol(COL_MB),
                  full((8, 3 * d)), full((1, d)), full((1, d)),
                  full((d, d)), full((d, d)), full((d, d))],
        out_specs=row,
        compiler_params=_cparams(("arbitrary", "arbitrary")),
        name="merge",
    )(xall, haf, hab, hbf, hbb, p, p, p, p, p, mod_l, gha, ghb, wa, wb, wo)


def _final_kernel(x_ref, g_ref, out_ref):
    x = x_ref[0]
    out_ref[0] = x * lax.rsqrt(jnp.mean(x * x, axis=-1, keepdims=True) + EPS) * g_ref[...]


def _final_norm(xall, g, ctx_len):
    nb, tb, d = xall.shape
    tm = 256
    off = ctx_len // tm
    return pl.pallas_call(
        _final_kernel,
        out_shape=jax.ShapeDtypeStruct((nb, tb - ctx_len, d), F32),
        grid=(nb, (tb - ctx_len) // tm),
        in_specs=[pl.BlockSpec((1, tm, d), lambda b, i: (b, i + off, 0)),
                  pl.BlockSpec((1, d), lambda b, i: (0, 0))],
        out_specs=pl.BlockSpec((1, tm, d), lambda b, i: (b, i, 0)),
        compiler_params=_cparams(("arbitrary", "arbitrary")),
        name="final_norm",
    )(xall, g)


def kernel(x, c, ctx, c_ctx, w_ada, b_ada, g_norm, w_in, b_in, w_conv, b_conv, lb_logits,
           g_head_a, g_head_b, w_a, w_b, w_out, g_final):
    nb, seq, d = x.shape
    ctx_len = ctx.shape[1]
    depth = w_ada.shape[0]
    d_a = g_head_a.shape[-1]
    d_b = g_head_b.shape[-1]
    assert d_a == d and d_b == d and nb + 1 <= 8
    dh_a = d_a // H_A

    g0 = 5 * d_a
    g1 = g0 + N_GATES
    w_main = jnp.concatenate([w_in[:, :, :g0], w_in[:, :, g1:]], axis=-1).astype(BF16)
    b_main = jnp.concatenate([b_in[:, :g0], b_in[:, g1:]], axis=-1).astype(F32)
    assert w_main.shape[-1] == N_COL_BLOCKS * d
    wg = jnp.pad(w_in[:, :, g0:g1].astype(F32), ((0, 0), (0, 0), (0, GATE_PAD - N_GATES)))
    wg_hi = wg.astype(BF16)
    wg_lo = (wg - wg_hi.astype(F32)).astype(BF16)
    wg2 = jnp.stack([wg_hi, wg_lo], axis=1)
    bg = jnp.pad(b_in[:, g0:g1].astype(F32), ((0, 0), (0, GATE_PAD - N_GATES)))
    qk_scale = jnp.concatenate([jnp.ones((1, d_a), F32), jnp.full((1, d_a), dh_a ** -0.5, F32)], axis=-1)
    w9 = w_conv.reshape(depth, 9, 2 * d_a).astype(F32)

    cc = jnp.concatenate([c.astype(F32), c_ctx.astype(F32)[None, :],
                          jnp.zeros((8 - nb - 1, d), F32)], axis=0)
    mod = _ada(cc, w_ada.astype(F32), b_ada.astype(F32))
    lbs = _lbs(lb_logits)

    xall = jnp.concatenate([ctx.astype(F32), x.astype(F32)], axis=1)
    for l in range(depth):
        p, g = _inproj(xall, mod[l], g_norm[l][None, :].astype(F32), w_main[l], b_main[l][None, :],
                       wg2[l], bg[l][None, :], ctx_len)
        qk = _conv_silu(p, w9[l], b_conv[l][None, :].astype(F32), qk_scale, ctx_len, 2 * d_a)
        haf, hab = _mlstm(qk, p, g, ctx_len, d_a)
        hbf, hbb = _hgrn(p, lbs[l][None, :], ctx_len, d_b)
        xall = _merge(xall, haf, hab, hbf, hbb, p, mod[l], g_head_a[l][None, :].astype(F32),
                      g_head_b[l][None, :].astype(F32), w_a[l].astype(BF16), w_b[l].astype(BF16),
                      w_out[l].astype(BF16), ctx_len)
    return _final_norm(xall, g_final[None, :].astype(F32), ctx_len).astype(x.dtype)
```

```python
import functools
import math

import numpy as np
import jax
import jax.numpy as jnp
from jax import lax
from jax.experimental import pallas as pl
from jax.experimental.pallas import tpu as pltpu

F32 = jnp.float32
BF16 = jnp.bfloat16

H_A = 4
H_B = 8
GRID_W = 64
EPS = 1e-6
NEG = -1e30
LOG2E = math.log2(math.e)
N_GATES = 4 * H_A
LANES = 128
L_A = 256
L_B = 128
HGRN_COMPACT_MIN = 16
CONV_ROWS = 256
CONV_HALO = 128
CONV_COLS = 512
INPROJ_TN = 512
V7X_VMEM_LIMIT = 56 * 1024 * 1024
N_RAW, N_MIX, N_SIG, N_SILU, N_DECAY = 2, 3, 3, 2, 2
MIX_V, MIX_QB, MIX_IB = 0, 1, 2
SIG_O, SIG_MA, SIG_MB = 0, 1, 2
SILU_ZA, SILU_ZB = 0, 1


def _cparams(semantics):
    return pltpu.CompilerParams(dimension_semantics=semantics, vmem_limit_bytes=V7X_VMEM_LIMIT)


def _sigmoid(x):
    return 1.0 / (1.0 + jnp.exp(-x))


def _split3(x):
    hi = x.astype(BF16)
    r1 = x - hi.astype(F32)
    mid = r1.astype(BF16)
    lo = (r1 - mid.astype(F32)).astype(BF16)
    return hi, mid, lo


def _cumsum_mm(tri, x):
    hi, mid, lo = _split3(x)
    d = lambda a: jnp.dot(tri, a, preferred_element_type=F32)
    return d(hi) + d(mid) + d(lo)


def _dot_nt(a, b):
    return lax.dot_general(a, b, (((1,), (1,)), ((), ())), preferred_element_type=F32)


def _dot_tn(a, b):
    return lax.dot_general(a, b, (((0,), (0,)), ((), ())), preferred_element_type=F32)


def _ada_kernel(cc_ref, w_ref, b_ref, out_ref):
    s = cc_ref[...]
    s = s * _sigmoid(s)
    w = w_ref[0]
    s_hi = s.astype(BF16)
    s_lo = (s - s_hi.astype(F32)).astype(BF16)
    w_hi = w.astype(BF16)
    w_lo = (w - w_hi.astype(F32)).astype(BF16)
    d = lambda a, b: jnp.dot(a, b, preferred_element_type=F32)
    out_ref[0] = d(s_hi, w_hi) + d(s_lo, w_hi) + d(s_hi, w_lo) + b_ref[0]


def _ada(cc, w_ada, b_ada):
    depth, d, n3 = w_ada.shape
    tn = 512
    return pl.pallas_call(
        _ada_kernel,
        out_shape=jax.ShapeDtypeStruct((depth, 8, n3), F32),
        grid=(depth, n3 // tn),
        in_specs=[pl.BlockSpec((8, d), lambda l, j: (0, 0)),
                  pl.BlockSpec((1, d, tn), lambda l, j: (l, 0, j)),
                  pl.BlockSpec((1, 1, tn), lambda l, j: (l, 0, j))],
        out_specs=pl.BlockSpec((1, 8, tn), lambda l, j: (l, 0, j)),
        compiler_params=_cparams(("arbitrary", "arbitrary")),
        name="ada",
    )(cc, w_ada, b_ada.reshape(depth, 1, n3))


def _lbs_kernel(lg_ref, out_ref):
    x = lg_ref[...]
    depth = x.shape[0]
    m = jnp.max(x, axis=0, keepdims=True)
    e = jnp.exp(x - m)
    p = e / jnp.sum(e, axis=0, keepdims=True)
    acc = jnp.zeros_like(p[0:1])
    for l in range(depth):
        acc = acc + p[l:l + 1]
        out_ref[l:l + 1, :] = acc - p[0:1]


def _lbs(lb_logits):
    return pl.pallas_call(
        _lbs_kernel, out_shape=jax.ShapeDtypeStruct(lb_logits.shape, F32), name="lbs",
    )(lb_logits.astype(F32))


def _inproj_kernel(x_ref, mod_ref, gn_ref, w_ref, b_ref, wg_ref, bg_ref, lb_ref,
                   raw_ref, mix_ref, sig_ref, silu_ref, lf_ref, k_ref, g_ref, h_scr,
                   *, ctx_len, tm, n_batch, bounds):
    b = pl.program_id(0)
    i = pl.program_id(1)
    j = pl.program_id(2)

    @pl.when(j == 0)
    def _():
        x = x_ref[0]
        d = x.shape[-1]
        ms = jnp.mean(x * x, axis=-1, keepdims=True)
        y = x * lax.rsqrt(ms + EPS) * gn_ref[...]
        row = i * tm + lax.broadcasted_iota(jnp.int32, (tm, 1), 0)
        is_ctx = row < ctx_len
        mb = mod_ref[pl.ds(b, 1), :]
        mc = mod_ref[pl.ds(n_batch, 1), :]
        shift = jnp.where(is_ctx, mc[:, 0:d], mb[:, 0:d])
        scale = jnp.where(is_ctx, mc[:, d:2 * d], mb[:, d:2 * d])
        h = y * (1.0 + scale) + shift
        hi = h.astype(BF16)
        h_scr[...] = hi
        lo = (h - hi.astype(F32)).astype(BF16)
        dd = lambda a, bb: jnp.dot(a, bb, preferred_element_type=F32)
        g_ref[0] = dd(hi, wg_ref[0]) + dd(lo, wg_ref[0]) + dd(hi, wg_ref[1]) + bg_ref[...]

    acc = jnp.dot(h_scr[...], w_ref[...], preferred_element_type=F32) + b_ref[...]
    j_mix, j_sig, j_silu, j_decay = bounds

    @pl.when(j < j_mix)
    def _():
        raw_ref[0] = acc

    @pl.when(jnp.logical_and(j >= j_mix, j < j_sig))
    def _():
        mix_ref[0] = acc.astype(BF16)

    @pl.when(jnp.logical_and(j >= j_sig, j < j_silu))
    def _():
        sig_ref[0] = _sigmoid(acc).astype(BF16)

    @pl.when(jnp.logical_and(j >= j_silu, j < j_decay))
    def _():
        silu_ref[0] = (acc * _sigmoid(acc)).astype(BF16)

    @pl.when(j >= j_decay)
    def _():
        lbv = lb_ref[...]
        t = jnp.exp(-jnp.abs(acc))
        pos = acc >= 0.0
        rc = 1.0 / (1.0 + t)
        f = jnp.where(pos, 1.0 + lbv * t, t + lbv) * rc
        lf_ref[0] = jnp.log(f) * LOG2E
        k_ref[0] = ((1.0 - lbv) * jnp.where(pos, t, 1.0) * rc).astype(BF16)


def _pick_tile(n, candidates):
    for c in candidates:
        if n % c == 0:
            return c
    raise ValueError(f"no tile for {n}")


def _inproj(xall, mod_l, gn, w_main, b_main, wg, bg, lb, ctx_len):
    nb, tb, d = xall.shape
    tm = _pick_tile(tb, (1280, 640, 256))
    tn = INPROJ_TN
    per = d // tn
    counts = (N_RAW, N_MIX, N_SIG, N_SILU, N_DECAY)
    starts = np.cumsum((0,) + counts) * per
    assert w_main.shape[1] == int(starts[-1]) * tn
    kern = functools.partial(_inproj_kernel, ctx_len=ctx_len, tm=tm, n_batch=nb,
                             bounds=tuple(int(s) for s in starts[1:5]))

    def out_spec(g):
        lo, n = int(starts[g]), counts[g] * per
        return pl.BlockSpec((1, tm, tn), lambda b, i, j: (b, i, jnp.clip(j - lo, 0, n - 1)))

    shp = lambda n, dt: jax.ShapeDtypeStruct((nb, tb, n * d), dt)
    return pl.pallas_call(
        kern,
        out_shape=(shp(N_RAW, F32), shp(N_MIX, BF16), shp(N_SIG, BF16), shp(N_SILU, BF16),
                   shp(N_DECAY, F32), shp(N_DECAY, BF16),
                   jax.ShapeDtypeStruct((nb, tb, LANES), F32)),
        grid=(nb, tb // tm, int(starts[-1])),
        in_specs=[pl.BlockSpec((1, tm, d), lambda b, i, j: (b, i, 0)),
                  pl.BlockSpec((8, 3 * d), lambda b, i, j: (0, 0)),
                  pl.BlockSpec((1, d), lambda b, i, j: (0, 0)),
                  pl.BlockSpec((d, tn), lambda b, i, j: (0, j)),
                  pl.BlockSpec((1, tn), lambda b, i, j: (0, j)),
                  pl.BlockSpec((2, d, LANES), lambda b, i, j: (0, 0, 0)),
                  pl.BlockSpec((1, LANES), lambda b, i, j: (0, 0)),
                  pl.BlockSpec((1, tn), lambda b, i, j: (0, j % per))],
        out_specs=(out_spec(0), out_spec(1), out_spec(2), out_spec(3), out_spec(4), out_spec(4),
                   pl.BlockSpec((1, tm, LANES), lambda b, i, j: (b, i, 0))),
        scratch_shapes=[pltpu.VMEM((tm, d), BF16)],
        compiler_params=_cparams(("arbitrary", "arbitrary", "arbitrary")),
        name="inproj",
    )(xall, mod_l, gn, w_main, b_main, wg, bg, lb)


def _conv_kernel(prev_ref, cur_ref, next_ref, w_ref, bc_ref, sc_ref, out_ref, *, n_tiles, tr, halo):
    i = pl.program_id(1)
    is_ctx = i == 0
    prev_ok = i >= 2
    next_ok = jnp.logical_and(i >= 1, i < n_tiles - 1)
    tc = cur_ref.shape[-1]
    ext = tr + 2 * halo
    full = jnp.concatenate([jnp.where(prev_ok, prev_ref[0], 0.0), cur_ref[0],
                            jnp.where(next_ok, next_ref[0], 0.0)], axis=0)
    col = lax.broadcasted_iota(jnp.int32, (ext, 1), 0) % GRID_W
    left = jnp.where(jnp.logical_or(is_ctx, col != 0), pltpu.roll(full, 1, axis=0), 0.0)
    right = jnp.where(jnp.logical_or(is_ctx, col != GRID_W - 1), pltpu.roll(full, ext - 1, axis=0), 0.0)
    w = w_ref[...]
    acc = jnp.zeros((tr, tc), F32)
    srcs = (left, full, right)
    for dr in (-1, 0, 1):
        for dc in (-1, 0, 1):
            tap = (dr + 1) * 3 + (dc + 1)
            wt = w[tap:tap + 1, :]
            if dr != 0:
                wt = jnp.where(is_ctx, 0.0, wt)
            start = halo + GRID_W * dr
            acc = acc + wt * srcs[dc + 1][start:start + tr, :]
    y = acc + bc_ref[...]
    out_ref[0] = (y * _sigmoid(y) * sc_ref[...]).astype(out_ref.dtype)


def _conv_silu(raw, w9, bconv, scale, ctx_len):
    nb, tb, n_qk = raw.shape
    tr, halo, tc = CONV_ROWS, CONV_HALO, CONV_COLS
    assert ctx_len == tr and tb % tr == 0 and tr % GRID_W == 0 and halo % GRID_W == 0
    n_tiles = tb // tr
    per = tr // halo
    n_halo = tb // halo
    kern = functools.partial(_conv_kernel, n_tiles=n_tiles, tr=tr, halo=halo)
    return pl.pallas_call(
        kern,
        out_shape=jax.ShapeDtypeStruct((nb, tb, n_qk), BF16),
        grid=(nb, n_tiles, n_qk // tc),
        in_specs=[pl.BlockSpec((1, halo, tc), lambda b, i, j: (b, jnp.maximum(i * per - 1, 0), j)),
                  pl.BlockSpec((1, tr, tc), lambda b, i, j: (b, i, j)),
                  pl.BlockSpec((1, halo, tc), lambda b, i, j: (b, jnp.minimum((i + 1) * per, n_halo - 1), j)),
                  pl.BlockSpec((9, tc), lambda b, i, j: (0, j)),
                  pl.BlockSpec((1, tc), lambda b, i, j: (0, j)),
                  pl.BlockSpec((1, tc), lambda b, i, j: (0, j))],
        out_specs=pl.BlockSpec((1, tr, tc), lambda b, i, j: (b, i, j)),
        compiler_params=_cparams(("arbitrary", "arbitrary", "arbitrary")),
        name="conv_silu",
    )(raw, raw, raw, w9, bconv, scale)


def _mlstm_kernel(qkf_ref, vf_ref, gf_ref, qkb_ref, vb_ref, gb_ref, hf_ref, hb_ref,
                  c_scr, n_scr, m_scr, *, la, dh):
    @pl.when(pl.program_id(1) == 0)
    def _():
        c_scr[...] = jnp.zeros_like(c_scr)
        n_scr[...] = jnp.zeros_like(n_scr)
        m_scr[...] = jnp.zeros_like(m_scr)

    r = lax.broadcasted_iota(jnp.int32, (la, la), 0)
    s = lax.broadcasted_iota(jnp.int32, (la, la), 1)
    dirs = ((qkf_ref, vf_ref, gf_ref, hf_ref), (qkb_ref, vb_ref, gb_ref, hb_ref))
    for d, (qk_ref, v_ref, g_ref, h_ref) in enumerate(dirs):
        seen = (s <= r) if d == 0 else (s >= r)
        tri = jnp.where(seen, 1.0, 0.0).astype(BF16)
        g = g_ref[0]
        lsig = jnp.minimum(g, 0.0) - jnp.log(1.0 + jnp.exp(-jnp.abs(g)))
        bc = _cumsum_mm(tri, lsig)
        b2 = pltpu.roll(bc, LANES - H_A, axis=1) * LOG2E
        r2 = g * LOG2E - b2
        r2_t = r2.T
        last = la - 1 if d == 0 else 0
        b2_l = b2[last:last + 1, :]
        m2 = m_scr[d:d + 1, :]
        ws2 = r2 + b2_l
        m2_new = jnp.maximum(b2_l + m2, jnp.max(ws2, axis=0, keepdims=True))
        decay = jnp.exp2(b2_l + m2 - m2_new)
        wse = jnp.exp2(ws2 - m2_new)
        m_scr[d:d + 1, :] = m2_new
        for h in range(H_A):
            u = d * H_A + h
            ci = 2 * d * H_A + h
            rm = jnp.where(seen, r2_t[ci:ci + 1, :], NEG)
            m2u = m2[:, ci:ci + 1]
            mx = jnp.maximum(m2u, jnp.max(rm, axis=-1, keepdims=True))
            q = qk_ref[0, :, h * dh:(h + 1) * dh]
            k = qk_ref[0, :, (H_A + h) * dh:(H_A + h + 1) * dh]
            v = v_ref[0, :, h * dh:(h + 1) * dh]
            w = _dot_nt(q, k) * jnp.exp2(rm - mx)
            a_in = jnp.exp2(m2u - mx)
            em = jnp.exp2(-(b2[:, ci:ci + 1] + mx))
            c_st = c_scr[u]
            n_st = n_scr[u]
            qf = q.astype(F32)
            qa = (qf * a_in).astype(BF16)
            num = jnp.dot(jnp.concatenate([w.astype(BF16), qa], axis=1),
                          jnp.concatenate([v, c_st.astype(BF16)], axis=0), preferred_element_type=F32)
            qn = jnp.sum(qf * n_st, axis=-1, keepdims=True)
            den = a_in * qn + jnp.sum(w, axis=-1, keepdims=True)
            h_ref[0, :, h * dh:(h + 1) * dh] = (num / jnp.maximum(jnp.abs(den), em)).astype(h_ref.dtype)
            wse_u = wse[:, ci:ci + 1]
            kw = (k.astype(F32) * wse_u).astype(BF16)
            dec = decay[:, ci:ci + 1]
            c_scr[u] = dec * c_st + _dot_tn(kw, v)
            n_upd = _dot_tn(jnp.broadcast_to(wse_u, (la, LANES)).astype(BF16), k)
            n_scr[u] = dec * n_st + n_upd[0:1, :]


def _bwd_chunk(c, n_ctx_chunks, n_chunks):
    return jnp.where(c < n_ctx_chunks, n_ctx_chunks - 1 - c, n_chunks - 1 + n_ctx_chunks - c)


def _mlstm(qk, mix, g, ctx_len, d_a):
    nb, tb, _ = qk.shape
    la = L_A
    dh = d_a // H_A
    assert ctx_len % la == 0 and tb % la == 0
    nc, ncc = tb // la, ctx_len // la
    bw = functools.partial(_bwd_chunk, n_ctx_chunks=ncc, n_chunks=nc)
    kern = functools.partial(_mlstm_kernel, la=la, dh=dh)
    hshape = jax.ShapeDtypeStruct((nb, tb, d_a), BF16)
    return pl.pallas_call(
        kern,
        out_shape=(hshape, hshape),
        grid=(nb, nc),
        in_specs=[pl.BlockSpec((1, la, 2 * d_a), lambda b, c: (b, c, 0)),
                  pl.BlockSpec((1, la, d_a), lambda b, c: (b, c, MIX_V)),
                  pl.BlockSpec((1, la, LANES), lambda b, c: (b, c, 0)),
                  pl.BlockSpec((1, la, 2 * d_a), lambda b, c: (b, bw(c), 0)),
                  pl.BlockSpec((1, la, d_a), lambda b, c: (b, bw(c), MIX_V)),
                  pl.BlockSpec((1, la, LANES), lambda b, c: (b, bw(c), 0))],
        out_specs=(pl.BlockSpec((1, la, d_a), lambda b, c: (b, c, 0)),
                   pl.BlockSpec((1, la, d_a), lambda b, c: (b, bw(c), 0))),
        scratch_shapes=[pltpu.VMEM((2 * H_A, dh, dh), F32),
                        pltpu.VMEM((2 * H_A, 1, dh), F32),
                        pltpu.VMEM((8, LANES), F32)],
        compiler_params=_cparams(("arbitrary", "arbitrary")),
        name="mlstm",
    )(qk, mix, g, qk, mix, g)


def _hgrn_levels(l):
    return [1 << j for j in range(int(np.log2(l)))]


def _hgrn_q_rows(l, h, d):
    off = h if d == 0 else 0
    return [(k * 2 * h + off, k * 2 * h + off + h) for k in range(l // (2 * h))]


def _hgrn_masks(l):
    r = np.arange(l)[:, None]
    s = np.arange(l)[None, :]
    full, half = [[], []], [[], []]
    for d in (0, 1):
        full[d].append(r == s)
        for h in _hgrn_levels(l):
            same = (r // (2 * h)) == (s // (2 * h))
            m = same & ((r % (2 * h)) >= h) & ((s % (2 * h)) < h)
            m = m if d == 0 else m.T
            if h >= HGRN_COMPACT_MIN:
                half[d].append(np.concatenate([m[a:b] for a, b in _hgrn_q_rows(l, h, d)]))
            else:
                full[d].append(m)
    return np.array(full, np.float32), np.array(half, np.float32)


def _hgrn_tri(l):
    r = np.arange(l)[:, None]
    s = np.arange(l)[None, :]
    return np.stack([(s <= r), (s >= r)]).astype(np.float32)


def _hgrn_ref_rows(g_scr, lanes, l, h, d):
    off = h - 1 if d == 0 else h
    if h >= 8:
        pieces = [jnp.broadcast_to(g_scr[k * 2 * h + off:k * 2 * h + off + 1, lanes], (2 * h, LANES))
                  for k in range(l // (2 * h))]
        return jnp.concatenate(pieces, axis=0) if len(pieces) > 1 else pieces[0]
    assert h in (2, 4)
    sub = lax.broadcasted_iota(jnp.int32, (8, LANES), 0)
    pieces = []
    for grp in range(l // 8):
        if h == 4:
            pieces.append(jnp.broadcast_to(g_scr[grp * 8 + off:grp * 8 + off + 1, lanes], (8, LANES)))
        else:
            lo = jnp.broadcast_to(g_scr[grp * 8 + off:grp * 8 + off + 1, lanes], (8, LANES))
            hi = jnp.broadcast_to(g_scr[grp * 8 + 4 + off:grp * 8 + 4 + off + 1, lanes], (8, LANES))
            pieces.append(jnp.where(sub < 4, lo, hi))
    return jnp.concatenate(pieces, axis=0)


def _hgrn_kernel(qf_ref, vf_ref, kf_ref, lff_ref, qb_ref, vb_ref, kb_ref, lfb_ref,
                 tri_ref, mfull_ref, mhalf_ref, of_ref, ob_ref, st_scr, g_scr, *, l, dh):
    @pl.when(pl.program_id(1) == 0)
    def _():
        st_scr[...] = jnp.zeros_like(st_scr)

    levels = _hgrn_levels(l)
    dirs = ((qf_ref, vf_ref, kf_ref, lff_ref, of_ref), (qb_ref, vb_ref, kb_ref, lfb_ref, ob_ref))
    for d, (q_ref, v_ref, k_ref, lf_ref, o_ref) in enumerate(dirs):
        g_scr[...] = _cumsum_mm(tri_ref[d], lf_ref[0])
        last = l - 1 if d == 0 else 0
        for h in range(H_B):
            u = d * H_B + h
            lanes = slice(h * dh, (h + 1) * dh)
            q = q_ref[0, :, lanes]
            kk = k_ref[0, :, lanes]
            v = v_ref[0, :, lanes]
            gh = g_scr[:, lanes]
            a = _dot_nt(q, kk).astype(BF16) * mfull_ref[d, 0]
            n_full, n_half = 1, 0
            half_terms = []
            for hl in levels:
                if hl == 1:
                    f = jnp.exp2(lf_ref[0, :, lanes].astype(BF16))
                    a = a + _dot_nt(q * f, kk).astype(BF16) * mfull_ref[d, n_full]
                    n_full += 1
                    continue
                gm = _hgrn_ref_rows(g_scr, lanes, l, hl, d)
                e = jnp.exp2(-jnp.abs((gh - gm).astype(BF16)))
                if hl < HGRN_COMPACT_MIN:
                    a = a + _dot_nt(q * e, kk * e).astype(BF16) * mfull_ref[d, n_full]
                    n_full += 1
                else:
                    q_rows = _hgrn_q_rows(l, hl, d)
                    k_rows = _hgrn_q_rows(l, hl, 1 - d)
                    qc = jnp.concatenate([q[a0:a1] * e[a0:a1] for a0, a1 in q_rows], axis=0)
                    kparts = {}
                    for a0, a1 in q_rows:
                        kparts[a0] = kk[a0:a1]
                    for a0, a1 in k_rows:
                        kparts[a0] = kk[a0:a1] * e[a0:a1]
                    kt = jnp.concatenate([kparts[a0] for a0 in sorted(kparts)], axis=0)
                    half_terms.append((q_rows, _dot_nt(qc, kt).astype(BF16) * mhalf_ref[d, n_half]))
                    n_half += 1
            pieces = {r0: a[r0:r0 + HGRN_COMPACT_MIN] for r0 in range(0, l, HGRN_COMPACT_MIN)}
            for q_rows, term in half_terms:
                pos = 0
                for a0, a1 in q_rows:
                    for r0 in range(a0, a1, HGRN_COMPACT_MIN):
                        pieces[r0] = pieces[r0] + term[pos:pos + HGRN_COMPACT_MIN]
                        pos += HGRN_COMPACT_MIN
            a = jnp.concatenate([pieces[r0] for r0 in sorted(pieces)], axis=0)
            st = st_scr[u]
            g_l = g_scr[last:last + 1, lanes]
            o = (_dot_nt(q * jnp.exp2(gh.astype(BF16)), st.astype(BF16))
                 + jnp.dot(a, v, preferred_element_type=F32))
            o_ref[0, :, lanes] = o.astype(o_ref.dtype)
            ks = kk * jnp.exp2((g_l - gh).astype(BF16))
            st_scr[u] = st * jnp.exp2(g_l) + _dot_tn(v, ks)


def _hgrn(mix, kdec, lf, ctx_len, d_b):
    nb, tb, _ = mix.shape
    l = L_B
    dh = d_b // H_B
    assert dh == LANES and ctx_len % l == 0 and tb % l == 0
    nc, ncc = tb // l, ctx_len // l
    bw = functools.partial(_bwd_chunk, n_ctx_chunks=ncc, n_chunks=nc)
    kern = functools.partial(_hgrn_kernel, l=l, dh=dh)
    tri = jnp.asarray(_hgrn_tri(l), BF16)
    mfull, mhalf = _hgrn_masks(l)
    mfull, mhalf = jnp.asarray(mfull, BF16), jnp.asarray(mhalf, BF16)
    oshape = jax.ShapeDtypeStruct((nb, tb, d_b), BF16)
    blk = lambda col, chunk: pl.BlockSpec((1, l, d_b), lambda b, c: (b, chunk(c), col))
    ident = lambda c: c
    const = lambda a: pl.BlockSpec(a.shape, lambda b, c: (0,) * a.ndim)
    return pl.pallas_call(
        kern,
        out_shape=(oshape, oshape),
        grid=(nb, nc),
        in_specs=[blk(MIX_QB, ident), blk(MIX_IB, ident), blk(0, ident), blk(0, ident),
                  blk(MIX_QB, bw), blk(MIX_IB, bw), blk(1, bw), blk(1, bw),
                  const(tri), const(mfull), const(mhalf)],
        out_specs=(pl.BlockSpec((1, l, d_b), lambda b, c: (b, c, 0)),
                   pl.BlockSpec((1, l, d_b), lambda b, c: (b, bw(c), 0))),
        scratch_shapes=[pltpu.VMEM((2 * H_B, dh, dh), F32),
                        pltpu.VMEM((l, d_b), F32)],
        compiler_params=_cparams(("arbitrary", "arbitrary")),
        name="hgrn",
    )(mix, mix, kdec, lf, mix, mix, kdec, lf, tri, mfull, mhalf)


def _head_rms(x, n_heads):
    dh = x.shape[-1] // n_heads
    outs = []
    for h in range(n_heads):
        xh = x[:, h * dh:(h + 1) * dh]
        outs.append(xh * lax.rsqrt(jnp.mean(xh * xh, axis=-1, keepdims=True) + EPS))
    return jnp.concatenate(outs, axis=-1)


def _merge_kernel(x_ref, haf_ref, hab_ref, hbf_ref, hbb_ref, o_ref, za_ref, zb_ref, ma_ref, mb_ref,
                  mod_ref, gha_ref, ghb_ref, wa_ref, wb_ref, wo_ref, out_ref, *, ctx_len, tm, n_batch):
    b = pl.program_id(0)
    i = pl.program_id(1)
    d = x_ref.shape[-1]
    f32 = lambda ref: ref[0].astype(F32)
    h_a = (f32(haf_ref) + f32(hab_ref)) * f32(o_ref)
    y_a = _head_rms(h_a, H_A) * gha_ref[...] * f32(za_ref)
    y_b = _head_rms(f32(hbf_ref) + f32(hbb_ref), H_B) * ghb_ref[...] * f32(zb_ref)
    pa = jnp.dot(y_a.astype(BF16), wa_ref[...], preferred_element_type=F32)
    pb = jnp.dot(y_b.astype(BF16), wb_ref[...], preferred_element_type=F32)
    y = f32(ma_ref) * pa + f32(mb_ref) * pb
    br = jnp.dot(y.astype(BF16), wo_ref[...], preferred_element_type=F32)
    row = i * tm + lax.broadcasted_iota(jnp.int32, (tm, 1), 0)
    gate = jnp.where(row < ctx_len, mod_ref[pl.ds(n_batch, 1), 2 * d:3 * d],
                     mod_ref[pl.ds(b, 1), 2 * d:3 * d])
    out_ref[0] = x_ref[0] + gate * br


def _merge(xall, haf, hab, hbf, hbb, sig, silu, mod_l, gha, ghb, wa, wb, wo, ctx_len):
    nb, tb, d = xall.shape
    tm = 256
    kern = functools.partial(_merge_kernel, ctx_len=ctx_len, tm=tm, n_batch=nb)
    row = pl.BlockSpec((1, tm, d), lambda b, i: (b, i, 0))
    pcol = lambda col: pl.BlockSpec((1, tm, d), lambda b, i: (b, i, col))
    full = lambda shape: pl.BlockSpec(shape, lambda b, i: (0,) * len(shape))
    return pl.pallas_call(
        kern,
        out_shape=jax.ShapeDtypeStruct(xall.shape, F32),
        grid=(nb, tb // tm),
        in_specs=[row, row, row, row, row,
                  pcol(SIG_O), pcol(SILU_ZA), pcol(SILU_ZB), pcol(SIG_MA), pcol(SIG_MB),
                  full((8, 3 * d)), full((1, d)), full((1, d)),
                  full((d, d)), full((d, d)), full((d, d))],
        out_specs=row,
        compiler_params=_cparams(("arbitrary", "arbitrary")),
        name="merge",
    )(xall, haf, hab, hbf, hbb, sig, silu, silu, sig, sig, mod_l, gha, ghb, wa, wb, wo)


def _final_kernel(x_ref, g_ref, out_ref):
    x = x_ref[0]
    out_ref[0] = x * lax.rsqrt(jnp.mean(x * x, axis=-1, keepdims=True) + EPS) * g_ref[...]


def _final_norm(xall, g, ctx_len):
    nb, tb, d = xall.shape
    tm = 256
    off = ctx_len // tm
    return pl.pallas_call(
        _final_kernel,
        out_shape=jax.ShapeDtypeStruct((nb, tb - ctx_len, d), F32),
        grid=(nb, (tb - ctx_len) // tm),
        in_specs=[pl.BlockSpec((1, tm, d), lambda b, i: (b, i + off, 0)),
                  pl.BlockSpec((1, d), lambda b, i: (0, 0))],
        out_specs=pl.BlockSpec((1, tm, d), lambda b, i: (b, i, 0)),
        compiler_params=_cparams(("arbitrary", "arbitrary")),
        name="final_norm",
    )(xall, g)


def kernel(x, c, ctx, c_ctx, w_ada, b_ada, g_norm, w_in, b_in, w_conv, b_conv, lb_logits,
           g_head_a, g_head_b, w_a, w_b, w_out, g_final):
    nb, seq, d = x.shape
    ctx_len = ctx.shape[1]
    depth = w_ada.shape[0]
    d_a = g_head_a.shape[-1]
    d_b = g_head_b.shape[-1]
    assert d_a == d and d_b == d and nb + 1 <= 8
    dh_a = d_a // H_A

    g0 = 5 * d
    g1 = g0 + N_GATES
    grp = lambda n: slice(n * d, (n + 1) * d) if n < 5 else slice(g1 + (n - 5) * d, g1 + (n - 4) * d)
    QK0, QK1, V, O, ZA, QB, IB, FF, FB, ZB, MA, MB = range(12)
    order = (QK0, QK1, V, QB, IB, O, MA, MB, ZA, ZB, FF, FB)
    w_main = jnp.concatenate([w_in[:, :, grp(n)] for n in order], axis=-1).astype(BF16)
    b_main = jnp.concatenate([b_in[:, grp(n)] for n in order], axis=-1).astype(F32)
    wg = jnp.pad(w_in[:, :, g0:g1].astype(F32), ((0, 0), (0, 0), (0, LANES - N_GATES)))
    wg_hi = wg.astype(BF16)
    wg_lo = (wg - wg_hi.astype(F32)).astype(BF16)
    wg2 = jnp.stack([wg_hi, wg_lo], axis=1)
    bg = jnp.pad(b_in[:, g0:g1].astype(F32), ((0, 0), (0, LANES - N_GATES)))
    qk_scale = jnp.concatenate([jnp.ones((1, d_a), F32), jnp.full((1, d_a), dh_a ** -0.5, F32)], axis=-1)
    w9 = w_conv.reshape(depth, 9, 2 * d_a).astype(F32)

    cc = jnp.concatenate([c.astype(F32), c_ctx.astype(F32)[None, :],
                          jnp.zeros((8 - nb - 1, d), F32)], axis=0)
    mod = _ada(cc, w_ada.astype(F32), b_ada.astype(F32))
    lbs = _lbs(lb_logits)

    xall = jnp.concatenate([ctx.astype(F32), x.astype(F32)], axis=1)
    for l in range(depth):
        raw, mix, sig, silu, lf, kdec, g = _inproj(
            xall, mod[l], g_norm[l][None, :].astype(F32), w_main[l], b_main[l][None, :],
            wg2[l], bg[l][None, :], lbs[l][None, :], ctx_len)
        qk = _conv_silu(raw, w9[l], b_conv[l][None, :].astype(F32), qk_scale, ctx_len)
        haf, hab = _mlstm(qk, mix, g, ctx_len, d_a)
        hbf, hbb = _hgrn(mix, kdec, lf, ctx_len, d_b)
        xall = _merge(xall, haf, hab, hbf, hbb, sig, silu, mod[l], g_head_a[l][None, :].astype(F32),
                      g_head_b[l][None, :].astype(F32), w_a[l].astype(BF16), w_b[l].astype(BF16),
                      w_out[l].astype(BF16), ctx_len)
    return _final_norm(xall, g_final[None, :].astype(F32), ctx_len).astype(x.dtype)
```

```python
import functools
import math

import numpy as np
import jax
import jax.numpy as jnp
from jax import lax
from jax.experimental import pallas as pl
from jax.experimental.pallas import tpu as pltpu

F32 = jnp.float32
BF16 = jnp.bfloat16

H_A = 4
H_B = 8
GRID_W = 64
EPS = 1e-6
NEG = -1e30
LOG2E = math.log2(math.e)
N_GATES = 4 * H_A
LANES = 128
L_A = 256
L_B = 128
HGRN_COMPACT_MIN = 16
CONV_ROWS = 256
CONV_HALO = 128
CONV_COLS = 512
INPROJ_TN = 512
PROJ_ROWS = 256
V7X_VMEM_LIMIT = 56 * 1024 * 1024
N_RAW, N_MIX, N_SIG, N_SILU, N_DECAY = 2, 3, 3, 2, 2
MIX_V, MIX_QB, MIX_IB = 0, 1, 2
SIG_O, SIG_MA, SIG_MB = 0, 1, 2
SILU_ZA, SILU_ZB = 0, 1


def _cparams(semantics):
    return pltpu.CompilerParams(dimension_semantics=semantics, vmem_limit_bytes=V7X_VMEM_LIMIT)


def _sigmoid(x):
    return 1.0 / (1.0 + jnp.exp(-x))


def _split3(x):
    hi = x.astype(BF16)
    r1 = x - hi.astype(F32)
    mid = r1.astype(BF16)
    lo = (r1 - mid.astype(F32)).astype(BF16)
    return hi, mid, lo


def _cumsum_mm(tri, x):
    hi, mid, lo = _split3(x)
    d = lambda a: jnp.dot(tri, a, preferred_element_type=F32)
    return d(hi) + d(mid) + d(lo)


def _dot_nt(a, b):
    return lax.dot_general(a, b, (((1,), (1,)), ((), ())), preferred_element_type=F32)


def _dot_tn(a, b):
    return lax.dot_general(a, b, (((0,), (0,)), ((), ())), preferred_element_type=F32)


def _ada_kernel(cc_ref, w_ref, b_ref, out_ref):
    s = cc_ref[...]
    s = s * _sigmoid(s)
    w = w_ref[0]
    s_hi = s.astype(BF16)
    s_lo = (s - s_hi.astype(F32)).astype(BF16)
    w_hi = w.astype(BF16)
    w_lo = (w - w_hi.astype(F32)).astype(BF16)
    d = lambda a, b: jnp.dot(a, b, preferred_element_type=F32)
    out_ref[0] = d(s_hi, w_hi) + d(s_lo, w_hi) + d(s_hi, w_lo) + b_ref[0]


def _ada(cc, w_ada, b_ada):
    depth, d, n3 = w_ada.shape
    tn = 512
    return pl.pallas_call(
        _ada_kernel,
        out_shape=jax.ShapeDtypeStruct((depth, 8, n3), F32),
        grid=(depth, n3 // tn),
        in_specs=[pl.BlockSpec((8, d), lambda l, j: (0, 0)),
                  pl.BlockSpec((1, d, tn), lambda l, j: (l, 0, j)),
                  pl.BlockSpec((1, 1, tn), lambda l, j: (l, 0, j))],
        out_specs=pl.BlockSpec((1, 8, tn), lambda l, j: (l, 0, j)),
        compiler_params=_cparams(("arbitrary", "arbitrary")),
        name="ada",
    )(cc, w_ada, b_ada.reshape(depth, 1, n3))


def _lbs_kernel(lg_ref, out_ref):
    x = lg_ref[...]
    depth = x.shape[0]
    m = jnp.max(x, axis=0, keepdims=True)
    e = jnp.exp(x - m)
    p = e / jnp.sum(e, axis=0, keepdims=True)
    acc = jnp.zeros_like(p[0:1])
    for l in range(depth):
        acc = acc + p[l:l + 1]
        out_ref[l:l + 1, :] = acc - p[0:1]


def _lbs(lb_logits):
    return pl.pallas_call(
        _lbs_kernel, out_shape=jax.ShapeDtypeStruct(lb_logits.shape, F32), name="lbs",
    )(lb_logits.astype(F32))


def _norm_kernel(x_ref, mod_ref, gn_ref, wg_ref, bg_ref, h_ref, g_ref, *, ctx_len, tm, n_batch):
    b = pl.program_id(0)
    i = pl.program_id(1)
    x = x_ref[0]
    d = x.shape[-1]
    ms = jnp.mean(x * x, axis=-1, keepdims=True)
    y = x * lax.rsqrt(ms + EPS) * gn_ref[...]
    row = i * tm + lax.broadcasted_iota(jnp.int32, (tm, 1), 0)
    is_ctx = row < ctx_len
    mb = mod_ref[pl.ds(b, 1), :]
    mc = mod_ref[pl.ds(n_batch, 1), :]
    shift = jnp.where(is_ctx, mc[:, 0:d], mb[:, 0:d])
    scale = jnp.where(is_ctx, mc[:, d:2 * d], mb[:, d:2 * d])
    h = y * (1.0 + scale) + shift
    hi = h.astype(BF16)
    h_ref[0] = hi
    lo = (h - hi.astype(F32)).astype(BF16)
    dd = lambda a, bb: jnp.dot(a, bb, preferred_element_type=F32)
    g_ref[0] = dd(hi, wg_ref[0]) + dd(lo, wg_ref[0]) + dd(hi, wg_ref[1]) + bg_ref[...]


def _pick_tile(n, candidates):
    for c in candidates:
        if n % c == 0:
            return c
    raise ValueError(f"no tile for {n}")


def _norm_gates(xall, mod_l, gn, wg, bg, ctx_len):
    nb, tb, d = xall.shape
    tm = _pick_tile(tb, (640, 256))
    kern = functools.partial(_norm_kernel, ctx_len=ctx_len, tm=tm, n_batch=nb)
    return pl.pallas_call(
        kern,
        out_shape=(jax.ShapeDtypeStruct((nb, tb, d), BF16),
                   jax.ShapeDtypeStruct((nb, tb, LANES), F32)),
        grid=(nb, tb // tm),
        in_specs=[pl.BlockSpec((1, tm, d), lambda b, i: (b, i, 0)),
                  pl.BlockSpec((8, 3 * d), lambda b, i: (0, 0)),
                  pl.BlockSpec((1, d), lambda b, i: (0, 0)),
                  pl.BlockSpec((2, d, LANES), lambda b, i: (0, 0, 0)),
                  pl.BlockSpec((1, LANES), lambda b, i: (0, 0))],
        out_specs=(pl.BlockSpec((1, tm, d), lambda b, i: (b, i, 0)),
                   pl.BlockSpec((1, tm, LANES), lambda b, i: (b, i, 0))),
        compiler_params=_cparams(("arbitrary", "arbitrary")),
        name="norm_gates",
    )(xall, mod_l, gn, wg, bg)


def _proj_kernel(*refs, kind, rs):
    if kind == "decay":
        h_ref, w_ref, b_ref, lb_ref, lf_ref, k_ref = refs
    else:
        h_ref, w_ref, b_ref, out_ref = refs
    tm = h_ref.shape[1]
    for r0 in range(0, tm, rs):
        rows = slice(r0, r0 + rs)
        acc = jnp.dot(h_ref[0, rows, :], w_ref[...], preferred_element_type=F32) + b_ref[...]
        if kind == "raw":
            out_ref[0, rows, :] = acc
        elif kind == "cast":
            out_ref[0, rows, :] = acc.astype(BF16)
        elif kind == "sigmoid":
            out_ref[0, rows, :] = _sigmoid(acc).astype(BF16)
        elif kind == "silu":
            out_ref[0, rows, :] = (acc * _sigmoid(acc)).astype(BF16)
        else:
            lbv = lb_ref[...]
            t = jnp.exp(-jnp.abs(acc))
            pos = acc >= 0.0
            rc = 1.0 / (1.0 + t)
            f = jnp.where(pos, 1.0 + lbv * t, t + lbv) * rc
            lf_ref[0, rows, :] = jnp.log(f) * LOG2E
            k_ref[0, rows, :] = ((1.0 - lbv) * jnp.where(pos, t, 1.0) * rc).astype(BF16)


def _proj(h16, w_main, b_main, col0, n_groups, kind, lb=None):
    nb, tb, d = h16.shape
    tm = _pick_tile(tb, (1280, 640, 256))
    tn = INPROJ_TN
    per = d // tn
    off = col0 * per
    kern = functools.partial(_proj_kernel, kind=kind, rs=PROJ_ROWS)
    in_specs = [pl.BlockSpec((1, tm, d), lambda b, i, j: (b, i, 0)),
                pl.BlockSpec((d, tn), lambda b, i, j: (0, j + off)),
                pl.BlockSpec((1, tn), lambda b, i, j: (0, j + off))]
    args = [h16, w_main, b_main]
    oblk = pl.BlockSpec((1, tm, tn), lambda b, i, j: (b, i, j))
    shp = lambda dt: jax.ShapeDtypeStruct((nb, tb, n_groups * d), dt)
    if kind == "decay":
        in_specs.append(pl.BlockSpec((1, tn), lambda b, i, j: (0, j % per)))
        args.append(lb)
        out_shape, out_specs = (shp(F32), shp(BF16)), (oblk, oblk)
    else:
        out_shape, out_specs = shp(F32 if kind == "raw" else BF16), oblk
    return pl.pallas_call(
        kern,
        out_shape=out_shape,
        grid=(nb, tb // tm, n_groups * per),
        in_specs=in_specs,
        out_specs=out_specs,
        compiler_params=_cparams(("arbitrary", "arbitrary", "arbitrary")),
        name="proj_" + kind,
    )(*args)


def _conv_kernel(prev_ref, cur_ref, next_ref, w_ref, bc_ref, sc_ref, out_ref, *, n_tiles, tr, halo):
    i = pl.program_id(1)
    is_ctx = i == 0
    prev_ok = i >= 2
    next_ok = jnp.logical_and(i >= 1, i < n_tiles - 1)
    tc = cur_ref.shape[-1]
    ext = tr + 2 * halo
    full = jnp.concatenate([jnp.where(prev_ok, prev_ref[0], 0.0), cur_ref[0],
                            jnp.where(next_ok, next_ref[0], 0.0)], axis=0)
    col = lax.broadcasted_iota(jnp.int32, (ext, 1), 0) % GRID_W
    left = jnp.where(jnp.logical_or(is_ctx, col != 0), pltpu.roll(full, 1, axis=0), 0.0)
    right = jnp.where(jnp.logical_or(is_ctx, col != GRID_W - 1), pltpu.roll(full, ext - 1, axis=0), 0.0)
    w = w_ref[...]
    acc = jnp.zeros((tr, tc), F32)
    srcs = (left, full, right)
    for dr in (-1, 0, 1):
        for dc in (-1, 0, 1):
            tap = (dr + 1) * 3 + (dc + 1)
            wt = w[tap:tap + 1, :]
            if dr != 0:
                wt = jnp.where(is_ctx, 0.0, wt)
            start = halo + GRID_W * dr
            acc = acc + wt * srcs[dc + 1][start:start + tr, :]
    y = acc + bc_ref[...]
    out_ref[0] = (y * _sigmoid(y) * sc_ref[...]).astype(out_ref.dtype)


def _conv_silu(raw, w9, bconv, scale, ctx_len):
    nb, tb, n_qk = raw.shape
    tr, halo, tc = CONV_ROWS, CONV_HALO, CONV_COLS
    assert ctx_len == tr and tb % tr == 0 and tr % GRID_W == 0 and halo % GRID_W == 0
    n_tiles = tb // tr
    per = tr // halo
    n_halo = tb // halo
    kern = functools.partial(_conv_kernel, n_tiles=n_tiles, tr=tr, halo=halo)
    return pl.pallas_call(
        kern,
        out_shape=jax.ShapeDtypeStruct((nb, tb, n_qk), BF16),
        grid=(nb, n_tiles, n_qk // tc),
        in_specs=[pl.BlockSpec((1, halo, tc), lambda b, i, j: (b, jnp.maximum(i * per - 1, 0), j)),
                  pl.BlockSpec((1, tr, tc), lambda b, i, j: (b, i, j)),
                  pl.BlockSpec((1, halo, tc), lambda b, i, j: (b, jnp.minimum((i + 1) * per, n_halo - 1), j)),
                  pl.BlockSpec((9, tc), lambda b, i, j: (0, j)),
                  pl.BlockSpec((1, tc), lambda b, i, j: (0, j)),
                  pl.BlockSpec((1, tc), lambda b, i, j: (0, j))],
        out_specs=pl.BlockSpec((1, tr, tc), lambda b, i, j: (b, i, j)),
        compiler_params=_cparams(("arbitrary", "arbitrary", "arbitrary")),
        name="conv_silu",
    )(raw, raw, raw, w9, bconv, scale)


def _mlstm_kernel(qkf_ref, vf_ref, gf_ref, qkb_ref, vb_ref, gb_ref, hf_ref, hb_ref,
                  c_scr, n_scr, m_scr, *, la, dh):
    @pl.when(pl.program_id(1) == 0)
    def _():
        c_scr[...] = jnp.zeros_like(c_scr)
        n_scr[...] = jnp.zeros_like(n_scr)
        m_scr[...] = jnp.zeros_like(m_scr)

    r = lax.broadcasted_iota(jnp.int32, (la, la), 0)
    s = lax.broadcasted_iota(jnp.int32, (la, la), 1)
    dirs = ((qkf_ref, vf_ref, gf_ref, hf_ref), (qkb_ref, vb_ref, gb_ref, hb_ref))
    for d, (qk_ref, v_ref, g_ref, h_ref) in enumerate(dirs):
        seen = (s <= r) if d == 0 else (s >= r)
        tri = jnp.where(seen, 1.0, 0.0).astype(BF16)
        g = g_ref[0]
        lsig = jnp.minimum(g, 0.0) - jnp.log(1.0 + jnp.exp(-jnp.abs(g)))
        bc = _cumsum_mm(tri, lsig)
        b2 = pltpu.roll(bc, LANES - H_A, axis=1) * LOG2E
        r2 = g * LOG2E - b2
        r2_t = r2.T
        last = la - 1 if d == 0 else 0
        b2_l = b2[last:last + 1, :]
        m2 = m_scr[d:d + 1, :]
        ws2 = r2 + b2_l
        m2_new = jnp.maximum(b2_l + m2, jnp.max(ws2, axis=0, keepdims=True))
        decay = jnp.exp2(b2_l + m2 - m2_new)
        wse = jnp.exp2(ws2 - m2_new)
        m_scr[d:d + 1, :] = m2_new
        for h in range(H_A):
            u = d * H_A + h
            ci = 2 * d * H_A + h
            rm = jnp.where(seen, r2_t[ci:ci + 1, :], NEG)
            m2u = m2[:, ci:ci + 1]
            mx = jnp.maximum(m2u, jnp.max(rm, axis=-1, keepdims=True))
            q = qk_ref[0, :, h * dh:(h + 1) * dh]
            k = qk_ref[0, :, (H_A + h) * dh:(H_A + h + 1) * dh]
            v = v_ref[0, :, h * dh:(h + 1) * dh]
            w = _dot_nt(q, k) * jnp.exp2(rm - mx)
            a_in = jnp.exp2(m2u - mx)
            em = jnp.exp2(-(b2[:, ci:ci + 1] + mx))
            c_st = c_scr[u]
            n_st = n_scr[u]
            qf = q.astype(F32)
            qa = (qf * a_in).astype(BF16)
            num = jnp.dot(jnp.concatenate([w.astype(BF16), qa], axis=1),
                          jnp.concatenate([v, c_st.astype(BF16)], axis=0), preferred_element_type=F32)
            qn = jnp.sum(qf * n_st, axis=-1, keepdims=True)
            den = a_in * qn + jnp.sum(w, axis=-1, keepdims=True)
            h_ref[0, :, h * dh:(h + 1) * dh] = (num / jnp.maximum(jnp.abs(den), em)).astype(h_ref.dtype)
            wse_u = wse[:, ci:ci + 1]
            kw = (k.astype(F32) * wse_u).astype(BF16)
            dec = decay[:, ci:ci + 1]
            c_scr[u] = dec * c_st + _dot_tn(kw, v)
            n_upd = _dot_tn(jnp.broadcast_to(wse_u, (la, LANES)).astype(BF16), k)
            n_scr[u] = dec * n_st + n_upd[0:1, :]


def _bwd_chunk(c, n_ctx_chunks, n_chunks):
    return jnp.where(c < n_ctx_chunks, n_ctx_chunks - 1 - c, n_chunks - 1 + n_ctx_chunks - c)


def _mlstm(qk, mix, g, ctx_len, d_a):
    nb, tb, _ = qk.shape
    la = L_A
    dh = d_a // H_A
    assert ctx_len % la == 0 and tb % la == 0
    nc, ncc = tb // la, ctx_len // la
    bw = functools.partial(_bwd_chunk, n_ctx_chunks=ncc, n_chunks=nc)
    kern = functools.partial(_mlstm_kernel, la=la, dh=dh)
    hshape = jax.ShapeDtypeStruct((nb, tb, d_a), BF16)
    return pl.pallas_call(
        kern,
        out_shape=(hshape, hshape),
        grid=(nb, nc),
        in_specs=[pl.BlockSpec((1, la, 2 * d_a), lambda b, c: (b, c, 0)),
                  pl.BlockSpec((1, la, d_a), lambda b, c: (b, c, MIX_V)),
                  pl.BlockSpec((1, la, LANES), lambda b, c: (b, c, 0)),
                  pl.BlockSpec((1, la, 2 * d_a), lambda b, c: (b, bw(c), 0)),
                  pl.BlockSpec((1, la, d_a), lambda b, c: (b, bw(c), MIX_V)),
                  pl.BlockSpec((1, la, LANES), lambda b, c: (b, bw(c), 0))],
        out_specs=(pl.BlockSpec((1, la, d_a), lambda b, c: (b, c, 0)),
                   pl.BlockSpec((1, la, d_a), lambda b, c: (b, bw(c), 0))),
        scratch_shapes=[pltpu.VMEM((2 * H_A, dh, dh), F32),
                        pltpu.VMEM((2 * H_A, 1, dh), F32),
                        pltpu.VMEM((8, LANES), F32)],
        compiler_params=_cparams(("arbitrary", "arbitrary")),
        name="mlstm",
    )(qk, mix, g, qk, mix, g)


def _hgrn_levels(l):
    return [1 << j for j in range(int(np.log2(l)))]


def _hgrn_q_rows(l, h, d):
    off = h if d == 0 else 0
    return [(k * 2 * h + off, k * 2 * h + off + h) for k in range(l // (2 * h))]


def _hgrn_masks(l):
    r = np.arange(l)[:, None]
    s = np.arange(l)[None, :]
    full, half = [[], []], [[], []]
    for d in (0, 1):
        full[d].append(r == s)
        for h in _hgrn_levels(l):
            same = (r // (2 * h)) == (s // (2 * h))
            m = same & ((r % (2 * h)) >= h) & ((s % (2 * h)) < h)
            m = m if d == 0 else m.T
            if h >= HGRN_COMPACT_MIN:
                half[d].append(np.concatenate([m[a:b] for a, b in _hgrn_q_rows(l, h, d)]))
            else:
                full[d].append(m)
    return np.array(full, np.float32), np.array(half, np.float32)


def _hgrn_tri(l):
    r = np.arange(l)[:, None]
    s = np.arange(l)[None, :]
    return np.stack([(s <= r), (s >= r)]).astype(np.float32)


def _hgrn_split_rows(g_scr, l, h, d):
    assert h in (2, 4)
    off = h - 1 if d == 0 else h
    n = g_scr.shape[-1]
    sub = lax.broadcasted_iota(jnp.int32, (8, n), 0)
    pieces = []
    for grp in range(l // 8):
        lo = jnp.broadcast_to(g_scr[grp * 8 + off:grp * 8 + off + 1, :], (8, n))
        if h == 4:
            pieces.append(lo)
        else:
            hi = jnp.broadcast_to(g_scr[grp * 8 + 4 + off:grp * 8 + 4 + off + 1, :], (8, n))
            pieces.append(jnp.where(sub < 4, lo, hi))
    return jnp.concatenate(pieces, axis=0)


def _hgrn_kernel(qf_ref, vf_ref, kf_ref, lff_ref, qb_ref, vb_ref, kb_ref, lfb_ref,
                 tri_ref, mfull_ref, mhalf_ref, of_ref, ob_ref, st_scr, g_scr, *, l, dh):
    @pl.when(pl.program_id(1) == 0)
    def _():
        st_scr[...] = jnp.zeros_like(st_scr)

    heads = [slice(h * dh, (h + 1) * dh) for h in range(H_B)]
    dirs = ((qf_ref, vf_ref, kf_ref, lff_ref, of_ref), (qb_ref, vb_ref, kb_ref, lfb_ref, ob_ref))
    for d, (q_ref, v_ref, k_ref, lf_ref, o_ref) in enumerate(dirs):
        g_scr[...] = _cumsum_mm(tri_ref[d], lf_ref[0])
        q = q_ref[0]
        kk = k_ref[0]
        v = v_ref[0]

        def scores(qs, ks, mask):
            return [_dot_nt(qs[:, hs], ks[:, hs]).astype(BF16) * mask for hs in heads]

        acc = scores(q, kk, mfull_ref[d, 0])
        n_full, n_half = 1, 0
        for hl in _hgrn_levels(l):
            m_off = hl - 1 if d == 0 else hl
            q_rows = _hgrn_q_rows(l, hl, d)
            k_rows = _hgrn_q_rows(l, hl, 1 - d)
            if hl < HGRN_COMPACT_MIN:
                if hl == 1:
                    qs, ks = q * jnp.exp2(lf_ref[0].astype(BF16)), kk
                else:
                    if hl < 8:
                        x = -jnp.abs(g_scr[...] - _hgrn_split_rows(g_scr, l, hl, d))
                    else:
                        parts = {}
                        for (q0, q1), (k0, k1) in zip(q_rows, k_rows):
                            gm = g_scr[min(q0, k0) + m_off:min(q0, k0) + m_off + 1, :]
                            parts[q0] = g_scr[q0:q1, :] - gm
                            parts[k0] = gm - g_scr[k0:k1, :]
                        x = jnp.concatenate([parts[r0] for r0 in sorted(parts)], axis=0)
                    e = jnp.exp2(x.astype(BF16))
                    qs, ks = q * e, kk * e
                acc = [a + t for a, t in zip(acc, scores(qs, ks, mfull_ref[d, n_full]))]
                n_full += 1
            else:
                xq, xk = [], []
                for (q0, q1), (k0, k1) in zip(q_rows, k_rows):
                    gm = g_scr[min(q0, k0) + m_off:min(q0, k0) + m_off + 1, :]
                    xq.append(g_scr[q0:q1, :] - gm)
                    xk.append(gm - g_scr[k0:k1, :])
                eq = jnp.exp2(jnp.concatenate(xq, axis=0).astype(BF16))
                ek = jnp.exp2(jnp.concatenate(xk, axis=0).astype(BF16))
                qc = jnp.concatenate([q[q0:q1] for q0, q1 in q_rows], axis=0) * eq
                kparts = {q0: kk[q0:q1] for q0, q1 in q_rows}
                for n, (k0, k1) in enumerate(k_rows):
                    kparts[k0] = kk[k0:k1] * ek[n * hl:(n + 1) * hl]
                kt = jnp.concatenate([kparts[r0] for r0 in sorted(kparts)], axis=0)
                terms = scores(qc, kt, mhalf_ref[d, n_half])
                n_half += 1
                for i in range(H_B):
                    rows = {k0: acc[i][k0:k1] for k0, k1 in k_rows}
                    for n, (q0, q1) in enumerate(q_rows):
                        rows[q0] = acc[i][q0:q1] + terms[i][n * hl:(n + 1) * hl]
                    acc[i] = jnp.concatenate([rows[r0] for r0 in sorted(rows)], axis=0)
        last = l - 1 if d == 0 else 0
        g_all = g_scr[...]
        g_l = g_scr[last:last + 1, :]
        qi = q * jnp.exp2(g_all.astype(BF16))
        ks = kk * jnp.exp2((g_l - g_all).astype(BF16))
        dec = jnp.exp2(g_l)
        for i, hs in enumerate(heads):
            u = d * H_B + i
            st = st_scr[u]
            o = _dot_nt(qi[:, hs], st.astype(BF16)) + jnp.dot(acc[i], v[:, hs], preferred_element_type=F32)
            o_ref[0, :, hs] = o.astype(o_ref.dtype)
            st_scr[u] = st * dec[:, hs] + _dot_tn(v[:, hs], ks[:, hs])


def _hgrn(mix, kdec, lf, ctx_len, d_b):
    nb, tb, _ = mix.shape
    l = L_B
    dh = d_b // H_B
    assert dh == LANES and ctx_len % l == 0 and tb % l == 0
    nc, ncc = tb // l, ctx_len // l
    bw = functools.partial(_bwd_chunk, n_ctx_chunks=ncc, n_chunks=nc)
    kern = functools.partial(_hgrn_kernel, l=l, dh=dh)
    tri = jnp.asarray(_hgrn_tri(l), BF16)
    mfull, mhalf = _hgrn_masks(l)
    mfull, mhalf = jnp.asarray(mfull, BF16), jnp.asarray(mhalf, BF16)
    oshape = jax.ShapeDtypeStruct((nb, tb, d_b), BF16)
    blk = lambda col, chunk: pl.BlockSpec((1, l, d_b), lambda b, c: (b, chunk(c), col))
    ident = lambda c: c
    const = lambda a: pl.BlockSpec(a.shape, lambda b, c: (0,) * a.ndim)
    return pl.pallas_call(
        kern,
        out_shape=(oshape, oshape),
        grid=(nb, nc),
        in_specs=[blk(MIX_QB, ident), blk(MIX_IB, ident), blk(0, ident), blk(0, ident),
                  blk(MIX_QB, bw), blk(MIX_IB, bw), blk(1, bw), blk(1, bw),
                  const(tri), const(mfull), const(mhalf)],
        out_specs=(pl.BlockSpec((1, l, d_b), lambda b, c: (b, c, 0)),
                   pl.BlockSpec((1, l, d_b), lambda b, c: (b, bw(c), 0))),
        scratch_shapes=[pltpu.VMEM((2 * H_B, dh, dh), F32),
                        pltpu.VMEM((l, d_b), F32)],
        compiler_params=_cparams(("arbitrary", "arbitrary")),
        name="hgrn",
    )(mix, mix, kdec, lf, mix, mix, kdec, lf, tri, mfull, mhalf)


def _head_rms(x, n_heads):
    dh = x.shape[-1] // n_heads
    outs = []
    for h in range(n_heads):
        xh = x[:, h * dh:(h + 1) * dh]
        outs.append(xh * lax.rsqrt(jnp.mean(xh * xh, axis=-1, keepdims=True) + EPS))
    return jnp.concatenate(outs, axis=-1)


def _merge_kernel(x_ref, haf_ref, hab_ref, hbf_ref, hbb_ref, o_ref, za_ref, zb_ref, ma_ref, mb_ref,
                  mod_ref, gha_ref, ghb_ref, wa_ref, wb_ref, wo_ref, out_ref, *, ctx_len, tm, n_batch):
    b = pl.program_id(0)
    i = pl.program_id(1)
    d = x_ref.shape[-1]
    f32 = lambda ref: ref[0].astype(F32)
    h_a = (f32(haf_ref) + f32(hab_ref)) * f32(o_ref)
    y_a = _head_rms(h_a, H_A) * gha_ref[...] * f32(za_ref)
    y_b = _head_rms(f32(hbf_ref) + f32(hbb_ref), H_B) * ghb_ref[...] * f32(zb_ref)
    pa = jnp.dot(y_a.astype(BF16), wa_ref[...], preferred_element_type=F32)
    pb = jnp.dot(y_b.astype(BF16), wb_ref[...], preferred_element_type=F32)
    y = f32(ma_ref) * pa + f32(mb_ref) * pb
    br = jnp.dot(y.astype(BF16), wo_ref[...], preferred_element_type=F32)
    row = i * tm + lax.broadcasted_iota(jnp.int32, (tm, 1), 0)
    gate = jnp.where(row < ctx_len, mod_ref[pl.ds(n_batch, 1), 2 * d:3 * d],
                     mod_ref[pl.ds(b, 1), 2 * d:3 * d])
    out_ref[0] = x_ref[0] + gate * br


def _merge(xall, haf, hab, hbf, hbb, sig, silu, mod_l, gha, ghb, wa, wb, wo, ctx_len):
    nb, tb, d = xall.shape
    tm = 256
    kern = functools.partial(_merge_kernel, ctx_len=ctx_len, tm=tm, n_batch=nb)
    row = pl.BlockSpec((1, tm, d), lambda b, i: (b, i, 0))
    pcol = lambda col: pl.BlockSpec((1, tm, d), lambda b, i: (b, i, col))
    full = lambda shape: pl.BlockSpec(shape, lambda b, i: (0,) * len(shape))
    return pl.pallas_call(
        kern,
        out_shape=jax.ShapeDtypeStruct(xall.shape, F32),
        grid=(nb, tb // tm),
        in_specs=[row, row, row, row, row,
                  pcol(SIG_O), pcol(SILU_ZA), pcol(SILU_ZB), pcol(SIG_MA), pcol(SIG_MB),
                  full((8, 3 * d)), full((1, d)), full((1, d)),
                  full((d, d)), full((d, d)), full((d, d))],
        out_specs=row,
        compiler_params=_cparams(("arbitrary", "arbitrary")),
        name="merge",
    )(xall, haf, hab, hbf, hbb, sig, silu, silu, sig, sig, mod_l, gha, ghb, wa, wb, wo)


def _final_kernel(x_ref, g_ref, out_ref):
    x = x_ref[0]
    out_ref[0] = x * lax.rsqrt(jnp.mean(x * x, axis=-1, keepdims=True) + EPS) * g_ref[...]


def _final_norm(xall, g, ctx_len):
    nb, tb, d = xall.shape
    tm = 256
    off = ctx_len // tm
    return pl.pallas_call(
        _final_kernel,
        out_shape=jax.ShapeDtypeStruct((nb, tb - ctx_len, d), F32),
        grid=(nb, (tb - ctx_len) // tm),
        in_specs=[pl.BlockSpec((1, tm, d), lambda b, i: (b, i + off, 0)),
                  pl.BlockSpec((1, d), lambda b, i: (0, 0))],
        out_specs=pl.BlockSpec((1, tm, d), lambda b, i: (b, i, 0)),
        compiler_params=_cparams(("arbitrary", "arbitrary")),
        name="final_norm",
    )(xall, g)


def kernel(x, c, ctx, c_ctx, w_ada, b_ada, g_norm, w_in, b_in, w_conv, b_conv, lb_logits,
           g_head_a, g_head_b, w_a, w_b, w_out, g_final):
    nb, seq, d = x.shape
    ctx_len = ctx.shape[1]
    depth = w_ada.shape[0]
    d_a = g_head_a.shape[-1]
    d_b = g_head_b.shape[-1]
    assert d_a == d and d_b == d and nb + 1 <= 8
    dh_a = d_a // H_A

    g0 = 5 * d
    g1 = g0 + N_GATES
    grp = lambda n: slice(n * d, (n + 1) * d) if n < 5 else slice(g1 + (n - 5) * d, g1 + (n - 4) * d)
    QK0, QK1, V, O, ZA, QB, IB, FF, FB, ZB, MA, MB = range(12)
    order = (QK0, QK1, V, QB, IB, O, MA, MB, ZA, ZB, FF, FB)
    w_main = jnp.concatenate([w_in[:, :, grp(n)] for n in order], axis=-1).astype(BF16)
    b_main = jnp.concatenate([b_in[:, grp(n)] for n in order], axis=-1).astype(F32)
    wg = jnp.pad(w_in[:, :, g0:g1].astype(F32), ((0, 0), (0, 0), (0, LANES - N_GATES)))
    wg_hi = wg.astype(BF16)
    wg_lo = (wg - wg_hi.astype(F32)).astype(BF16)
    wg2 = jnp.stack([wg_hi, wg_lo], axis=1)
    bg = jnp.pad(b_in[:, g0:g1].astype(F32), ((0, 0), (0, LANES - N_GATES)))
    qk_scale = jnp.concatenate([jnp.ones((1, d_a), F32), jnp.full((1, d_a), dh_a ** -0.5, F32)], axis=-1)
    w9 = w_conv.reshape(depth, 9, 2 * d_a).astype(F32)

    cc = jnp.concatenate([c.astype(F32), c_ctx.astype(F32)[None, :],
                          jnp.zeros((8 - nb - 1, d), F32)], axis=0)
    mod = _ada(cc, w_ada.astype(F32), b_ada.astype(F32))
    lbs = _lbs(lb_logits)
    c_mix, c_sig, c_silu, c_dec = (int(n) for n in np.cumsum((N_RAW, N_MIX, N_SIG, N_SILU)))

    xall = jnp.concatenate([ctx.astype(F32), x.astype(F32)], axis=1)
    for l in range(depth):
        h16, g = _norm_gates(xall, mod[l], g_norm[l][None, :].astype(F32), wg2[l], bg[l][None, :], ctx_len)
        bm = b_main[l][None, :]
        raw = _proj(h16, w_main[l], bm, 0, N_RAW, "raw")
        mix = _proj(h16, w_main[l], bm, c_mix, N_MIX, "cast")
        sig = _proj(h16, w_main[l], bm, c_sig, N_SIG, "sigmoid")
        silu = _proj(h16, w_main[l], bm, c_silu, N_SILU, "silu")
        lf, kdec = _proj(h16, w_main[l], bm, c_dec, N_DECAY, "decay", lbs[l][None, :])
        qk = _conv_silu(raw, w9[l], b_conv[l][None, :].astype(F32), qk_scale, ctx_len)
        haf, hab = _mlstm(qk, mix, g, ctx_len, d_a)
        hbf, hbb = _hgrn(mix, kdec, lf, ctx_len, d_b)
        xall = _merge(xall, haf, hab, hbf, hbb, sig, silu, mod[l], g_head_a[l][None, :].astype(F32),
                      g_head_b[l][None, :].astype(F32), w_a[l].astype(BF16), w_b[l].astype(BF16),
                      w_out[l].astype(BF16), ctx_len)
    return _final_norm(xall, g_final[None, :].astype(F32), ctx_len).astype(x.dtype)
```

```python
import functools
import math

import numpy as np
import jax
import jax.numpy as jnp
from jax import lax
from jax.experimental import pallas as pl
from jax.experimental.pallas import tpu as pltpu

F32 = jnp.float32
BF16 = jnp.bfloat16

H_A = 4
H_B = 8
GRID_W = 64
EPS = 1e-6
NEG = -1e30
LOG2E = math.log2(math.e)
N_GATES = 4 * H_A
LANES = 128
L_A = 256
L_B = 128
HGRN_COMPACT_MIN = 16
CONV_ROWS = 256
CONV_HALO = 128
HGRN_CUMSUM_TERMS = 2
CONV_COLS = 2048
INPROJ_TN = 1024
PROJ_ROWS = 256
V7X_VMEM_LIMIT = 56 * 1024 * 1024
N_RAW, N_MIX, N_SIG, N_SILU, N_DECAY = 2, 3, 3, 2, 2
MIX_V, MIX_QB, MIX_IB = 0, 1, 2
SIG_O, SIG_MA, SIG_MB = 0, 1, 2
SILU_ZA, SILU_ZB = 0, 1


def _cparams(semantics):
    return pltpu.CompilerParams(dimension_semantics=semantics, vmem_limit_bytes=V7X_VMEM_LIMIT)


def _sigmoid(x):
    return 1.0 / (1.0 + jnp.exp(-x))


def _split3(x):
    hi = x.astype(BF16)
    r1 = x - hi.astype(F32)
    mid = r1.astype(BF16)
    lo = (r1 - mid.astype(F32)).astype(BF16)
    return hi, mid, lo


def _cumsum_mm(tri, x, terms=3):
    out = None
    for part in _split3(x)[:terms]:
        term = jnp.dot(tri, part, preferred_element_type=F32)
        out = term if out is None else out + term
    return out


def _dot_nt(a, b):
    return lax.dot_general(a, b, (((1,), (1,)), ((), ())), preferred_element_type=F32)


def _dot_tn(a, b):
    return lax.dot_general(a, b, (((0,), (0,)), ((), ())), preferred_element_type=F32)


def _ada_kernel(cc_ref, w_ref, b_ref, out_ref):
    s = cc_ref[...]
    s = s * _sigmoid(s)
    w = w_ref[0]
    s_hi = s.astype(BF16)
    s_lo = (s - s_hi.astype(F32)).astype(BF16)
    w_hi = w.astype(BF16)
    w_lo = (w - w_hi.astype(F32)).astype(BF16)
    d = lambda a, b: jnp.dot(a, b, preferred_element_type=F32)
    out_ref[0] = d(s_hi, w_hi) + d(s_lo, w_hi) + d(s_hi, w_lo) + b_ref[0]


def _ada(cc, w_ada, b_ada):
    depth, d, n3 = w_ada.shape
    tn = 512
    return pl.pallas_call(
        _ada_kernel,
        out_shape=jax.ShapeDtypeStruct((depth, 8, n3), F32),
        grid=(depth, n3 // tn),
        in_specs=[pl.BlockSpec((8, d), lambda l, j: (0, 0)),
                  pl.BlockSpec((1, d, tn), lambda l, j: (l, 0, j)),
                  pl.BlockSpec((1, 1, tn), lambda l, j: (l, 0, j))],
        out_specs=pl.BlockSpec((1, 8, tn), lambda l, j: (l, 0, j)),
        compiler_params=_cparams(("arbitrary", "arbitrary")),
        name="ada",
    )(cc, w_ada, b_ada.reshape(depth, 1, n3))


def _lbs_kernel(lg_ref, out_ref):
    x = lg_ref[...]
    depth = x.shape[0]
    m = jnp.max(x, axis=0, keepdims=True)
    e = jnp.exp(x - m)
    p = e / jnp.sum(e, axis=0, keepdims=True)
    acc = jnp.zeros_like(p[0:1])
    for l in range(depth):
        acc = acc + p[l:l + 1]
        out_ref[l:l + 1, :] = acc - p[0:1]


def _lbs(lb_logits):
    return pl.pallas_call(
        _lbs_kernel, out_shape=jax.ShapeDtypeStruct(lb_logits.shape, F32), name="lbs",
    )(lb_logits.astype(F32))


def _norm_kernel(x_ref, mod_ref, gn_ref, wg_ref, bg_ref, h_ref, g_ref, *, ctx_len, tm, n_batch):
    b = pl.program_id(0)
    i = pl.program_id(1)
    x = x_ref[0]
    d = x.shape[-1]
    ms = jnp.mean(x * x, axis=-1, keepdims=True)
    y = x * lax.rsqrt(ms + EPS) * gn_ref[...]
    row = i * tm + lax.broadcasted_iota(jnp.int32, (tm, 1), 0)
    is_ctx = row < ctx_len
    mb = mod_ref[pl.ds(b, 1), :]
    mc = mod_ref[pl.ds(n_batch, 1), :]
    shift = jnp.where(is_ctx, mc[:, 0:d], mb[:, 0:d])
    scale = jnp.where(is_ctx, mc[:, d:2 * d], mb[:, d:2 * d])
    h = y * (1.0 + scale) + shift
    hi = h.astype(BF16)
    h_ref[0] = hi
    lo = (h - hi.astype(F32)).astype(BF16)
    dd = lambda a, bb: jnp.dot(a, bb, preferred_element_type=F32)
    g_ref[0] = dd(hi, wg_ref[0]) + dd(lo, wg_ref[0]) + dd(hi, wg_ref[1]) + bg_ref[...]


def _pick_tile(n, candidates):
    for c in candidates:
        if n % c == 0:
            return c
    raise ValueError(f"no tile for {n}")


def _norm_gates(xall, mod_l, gn, wg, bg, ctx_len):
    nb, tb, d = xall.shape
    tm = _pick_tile(tb, (640, 256))
    kern = functools.partial(_norm_kernel, ctx_len=ctx_len, tm=tm, n_batch=nb)
    return pl.pallas_call(
        kern,
        out_shape=(jax.ShapeDtypeStruct((nb, tb, d), BF16),
                   jax.ShapeDtypeStruct((nb, tb, LANES), F32)),
        grid=(nb, tb // tm),
        in_specs=[pl.BlockSpec((1, tm, d), lambda b, i: (b, i, 0)),
                  pl.BlockSpec((8, 3 * d), lambda b, i: (0, 0)),
                  pl.BlockSpec((1, d), lambda b, i: (0, 0)),
                  pl.BlockSpec((2, d, LANES), lambda b, i: (0, 0, 0)),
                  pl.BlockSpec((1, LANES), lambda b, i: (0, 0))],
        out_specs=(pl.BlockSpec((1, tm, d), lambda b, i: (b, i, 0)),
                   pl.BlockSpec((1, tm, LANES), lambda b, i: (b, i, 0))),
        compiler_params=_cparams(("arbitrary", "arbitrary")),
        name="norm_gates",
    )(xall, mod_l, gn, wg, bg)


def _proj_kernel(*refs, kind, rs):
    if kind == "decay":
        h_ref, w_ref, b_ref, lb_ref, lf_ref, k_ref = refs
    else:
        h_ref, w_ref, b_ref, out_ref = refs
    tm = h_ref.shape[1]
    for r0 in range(0, tm, rs):
        rows = slice(r0, r0 + rs)
        acc = jnp.dot(h_ref[0, rows, :], w_ref[...], preferred_element_type=F32) + b_ref[...]
        if kind == "raw":
            out_ref[0, rows, :] = acc
        elif kind == "cast":
            out_ref[0, rows, :] = acc.astype(BF16)
        elif kind == "sigmoid":
            out_ref[0, rows, :] = _sigmoid(acc).astype(BF16)
        elif kind == "silu":
            out_ref[0, rows, :] = (acc * _sigmoid(acc)).astype(BF16)
        else:
            lbv = lb_ref[...]
            t = jnp.exp(-jnp.abs(acc))
            pos = acc >= 0.0
            rc = 1.0 / (1.0 + t)
            f = jnp.where(pos, 1.0 + lbv * t, t + lbv) * rc
            lf_ref[0, rows, :] = jnp.log(f) * LOG2E
            k_ref[0, rows, :] = ((1.0 - lbv) * jnp.where(pos, t, 1.0) * rc).astype(BF16)


def _proj(h16, w_main, b_main, col0, n_groups, kind, lb=None):
    nb, tb, d = h16.shape
    tm = _pick_tile(tb, (1280, 640, 256))
    tn = INPROJ_TN
    per = d // tn
    off = col0 * per
    kern = functools.partial(_proj_kernel, kind=kind, rs=PROJ_ROWS)
    in_specs = [pl.BlockSpec((1, tm, d), lambda b, i, j: (b, i, 0)),
                pl.BlockSpec((d, tn), lambda b, i, j: (0, j + off)),
                pl.BlockSpec((1, tn), lambda b, i, j: (0, j + off))]
    args = [h16, w_main, b_main]
    oblk = pl.BlockSpec((1, tm, tn), lambda b, i, j: (b, i, j))
    shp = lambda dt: jax.ShapeDtypeStruct((nb, tb, n_groups * d), dt)
    if kind == "decay":
        in_specs.append(pl.BlockSpec((1, tn), lambda b, i, j: (0, j % per)))
        args.append(lb)
        out_shape, out_specs = (shp(F32), shp(BF16)), (oblk, oblk)
    else:
        out_shape, out_specs = shp(F32 if kind == "raw" else BF16), oblk
    return pl.pallas_call(
        kern,
        out_shape=out_shape,
        grid=(nb, tb // tm, n_groups * per),
        in_specs=in_specs,
        out_specs=out_specs,
        compiler_params=_cparams(("arbitrary", "arbitrary", "arbitrary")),
        name="proj_" + kind,
    )(*args)


def _conv_kernel(prev_ref, cur_ref, next_ref, w_ref, bc_ref, sc_ref, out_ref, *, n_tiles, tr, halo):
    i = pl.program_id(1)
    is_ctx = i == 0
    prev_ok = i >= 2
    next_ok = jnp.logical_and(i >= 1, i < n_tiles - 1)
    tc = cur_ref.shape[-1]
    ext = tr + 2 * halo
    full = jnp.concatenate([jnp.where(prev_ok, prev_ref[0], 0.0), cur_ref[0],
                            jnp.where(next_ok, next_ref[0], 0.0)], axis=0)
    col = lax.broadcasted_iota(jnp.int32, (ext, 1), 0) % GRID_W
    left = jnp.where(jnp.logical_or(is_ctx, col != 0), pltpu.roll(full, 1, axis=0), 0.0)
    right = jnp.where(jnp.logical_or(is_ctx, col != GRID_W - 1), pltpu.roll(full, ext - 1, axis=0), 0.0)
    w = w_ref[...]
    acc = jnp.zeros((tr, tc), F32)
    srcs = (left, full, right)
    for dr in (-1, 0, 1):
        for dc in (-1, 0, 1):
            tap = (dr + 1) * 3 + (dc + 1)
            wt = w[tap:tap + 1, :]
            if dr != 0:
                wt = jnp.where(is_ctx, 0.0, wt)
            start = halo + GRID_W * dr
            acc = acc + wt * srcs[dc + 1][start:start + tr, :]
    y = acc + bc_ref[...]
    out_ref[0] = (y * _sigmoid(y) * sc_ref[...]).astype(out_ref.dtype)


def _conv_silu(raw, w9, bconv, scale, ctx_len):
    nb, tb, n_qk = raw.shape
    tr, halo, tc = CONV_ROWS, CONV_HALO, CONV_COLS
    assert ctx_len == tr and tb % tr == 0 and tr % GRID_W == 0 and halo % GRID_W == 0
    n_tiles = tb // tr
    per = tr // halo
    n_halo = tb // halo
    kern = functools.partial(_conv_kernel, n_tiles=n_tiles, tr=tr, halo=halo)
    return pl.pallas_call(
        kern,
        out_shape=jax.ShapeDtypeStruct((nb, tb, n_qk), BF16),
        grid=(nb, n_tiles, n_qk // tc),
        in_specs=[pl.BlockSpec((1, halo, tc), lambda b, i, j: (b, jnp.maximum(i * per - 1, 0), j)),
                  pl.BlockSpec((1, tr, tc), lambda b, i, j: (b, i, j)),
                  pl.BlockSpec((1, halo, tc), lambda b, i, j: (b, jnp.minimum((i + 1) * per, n_halo - 1), j)),
                  pl.BlockSpec((9, tc), lambda b, i, j: (0, j)),
                  pl.BlockSpec((1, tc), lambda b, i, j: (0, j)),
                  pl.BlockSpec((1, tc), lambda b, i, j: (0, j))],
        out_specs=pl.BlockSpec((1, tr, tc), lambda b, i, j: (b, i, j)),
        compiler_params=_cparams(("arbitrary", "arbitrary", "arbitrary")),
        name="conv_silu",
    )(raw, raw, raw, w9, bconv, scale)


def _mlstm_kernel(qkf_ref, vf_ref, gf_ref, qkb_ref, vb_ref, gb_ref, hf_ref, hb_ref,
                  c_scr, n_scr, m_scr, *, la, dh):
    @pl.when(pl.program_id(1) == 0)
    def _():
        c_scr[...] = jnp.zeros_like(c_scr)
        n_scr[...] = jnp.zeros_like(n_scr)
        m_scr[...] = jnp.zeros_like(m_scr)

    r = lax.broadcasted_iota(jnp.int32, (la, la), 0)
    s = lax.broadcasted_iota(jnp.int32, (la, la), 1)
    ones_blk = jnp.ones((la, LANES), BF16)
    wide = lambda x, n: jnp.concatenate([x] * (n // LANES), axis=1)
    dirs = ((qkf_ref, vf_ref, gf_ref, hf_ref), (qkb_ref, vb_ref, gb_ref, hb_ref))
    for d, (qk_ref, v_ref, g_ref, h_ref) in enumerate(dirs):
        seen = (s <= r) if d == 0 else (s >= r)
        tri = jnp.where(seen, 1.0, 0.0).astype(BF16)
        g = g_ref[0]
        lsig = jnp.minimum(g, 0.0) - jnp.log(1.0 + jnp.exp(-jnp.abs(g)))
        bc = _cumsum_mm(tri, lsig)
        b2 = pltpu.roll(bc, LANES - H_A, axis=1) * LOG2E
        r2 = g * LOG2E - b2
        r2_t = r2.T
        last = la - 1 if d == 0 else 0
        b2_l = b2[last:last + 1, :]
        m2 = m_scr[d:d + 1, :]
        ws2 = r2 + b2_l
        m2_new = jnp.maximum(b2_l + m2, jnp.max(ws2, axis=0, keepdims=True))
        decay = jnp.exp2(b2_l + m2 - m2_new)
        wse = jnp.exp2(ws2 - m2_new)
        m_scr[d:d + 1, :] = m2_new
        cols = [2 * d * H_A + h for h in range(H_A)]
        q = [qk_ref[0, :, h * dh:(h + 1) * dh] for h in range(H_A)]
        k = [qk_ref[0, :, (H_A + h) * dh:(H_A + h + 1) * dh] for h in range(H_A)]
        v = [v_ref[0, :, h * dh:(h + 1) * dh] for h in range(H_A)]
        col = lambda x, ci: jnp.broadcast_to(x[:, ci:ci + 1], (la, LANES))
        rm = [jnp.where(seen, r2_t[ci:ci + 1, :], NEG) for ci in cols]
        mx = [jnp.broadcast_to(jnp.maximum(m2[:, ci:ci + 1], jnp.max(rm_h, axis=-1, keepdims=True)),
                               (la, LANES)) for ci, rm_h in zip(cols, rm)]
        s_mat = [_dot_nt(q[h], k[h]) for h in range(H_A)]
        for h, ci in enumerate(cols):
            u = d * H_A + h
            w16 = (s_mat[h] * jnp.exp2(rm[h] - wide(mx[h], la))).astype(BF16)
            a_in = jnp.exp2(m2[:, ci:ci + 1] - mx[h])
            em = jnp.exp2(-(col(b2, ci) + mx[h]))
            c_st = c_scr[u]
            n_st = n_scr[u]
            qa = q[h] * wide(a_in.astype(BF16), dh)
            num = jnp.dot(jnp.concatenate([w16, qa], axis=1),
                          jnp.concatenate([v[h], c_st.astype(BF16)], axis=0), preferred_element_type=F32)
            qn = _dot_nt(q[h], jnp.broadcast_to(n_st, (LANES, dh)).astype(BF16))
            den = a_in * qn + jnp.dot(w16, ones_blk, preferred_element_type=F32)
            rcp = 1.0 / jnp.maximum(jnp.abs(den), em)
            h_ref[0, :, h * dh:(h + 1) * dh] = (num * wide(rcp, dh)).astype(h_ref.dtype)
            wse_u = col(wse, ci).astype(BF16)
            dec = decay[:, ci:ci + 1]
            c_scr[u] = dec * c_st + _dot_tn(k[h] * wide(wse_u, dh), v[h])
            n_scr[u] = dec * n_st + _dot_tn(wse_u, k[h])[0:1, :]


def _bwd_chunk(c, n_ctx_chunks, n_chunks):
    return jnp.where(c < n_ctx_chunks, n_ctx_chunks - 1 - c, n_chunks - 1 + n_ctx_chunks - c)


def _mlstm(qk, mix, g, ctx_len, d_a):
    nb, tb, _ = qk.shape
    la = L_A
    dh = d_a // H_A
    assert ctx_len % la == 0 and tb % la == 0
    nc, ncc = tb // la, ctx_len // la
    bw = functools.partial(_bwd_chunk, n_ctx_chunks=ncc, n_chunks=nc)
    kern = functools.partial(_mlstm_kernel, la=la, dh=dh)
    hshape = jax.ShapeDtypeStruct((nb, tb, d_a), BF16)
    return pl.pallas_call(
        kern,
        out_shape=(hshape, hshape),
        grid=(nb, nc),
        in_specs=[pl.BlockSpec((1, la, 2 * d_a), lambda b, c: (b, c, 0)),
                  pl.BlockSpec((1, la, d_a), lambda b, c: (b, c, MIX_V)),
                  pl.BlockSpec((1, la, LANES), lambda b, c: (b, c, 0)),
                  pl.BlockSpec((1, la, 2 * d_a), lambda b, c: (b, bw(c), 0)),
                  pl.BlockSpec((1, la, d_a), lambda b, c: (b, bw(c), MIX_V)),
                  pl.BlockSpec((1, la, LANES), lambda b, c: (b, bw(c), 0))],
        out_specs=(pl.BlockSpec((1, la, d_a), lambda b, c: (b, c, 0)),
                   pl.BlockSpec((1, la, d_a), lambda b, c: (b, bw(c), 0))),
        scratch_shapes=[pltpu.VMEM((2 * H_A, dh, dh), F32),
                        pltpu.VMEM((2 * H_A, 1, dh), F32),
                        pltpu.VMEM((8, LANES), F32)],
        compiler_params=_cparams(("arbitrary", "arbitrary")),
        name="mlstm",
    )(qk, mix, g, qk, mix, g)


def _hgrn_levels(l):
    return [1 << j for j in range(int(np.log2(l)))]


def _hgrn_q_rows(l, h, d):
    off = h if d == 0 else 0
    return [(k * 2 * h + off, k * 2 * h + off + h) for k in range(l // (2 * h))]


def _hgrn_masks(l):
    r = np.arange(l)[:, None]
    s = np.arange(l)[None, :]
    full, half = [[], []], [[], []]
    for d in (0, 1):
        full[d].append(r == s)
        for h in _hgrn_levels(l):
            same = (r // (2 * h)) == (s // (2 * h))
            m = same & ((r % (2 * h)) >= h) & ((s % (2 * h)) < h)
            m = m if d == 0 else m.T
            if h >= HGRN_COMPACT_MIN:
                half[d].append(np.concatenate([m[a:b] for a, b in _hgrn_q_rows(l, h, d)]))
            else:
                full[d].append(m)
    return np.array(full, np.float32), np.array(half, np.float32)


def _hgrn_tri(l):
    r = np.arange(l)[:, None]
    s = np.arange(l)[None, :]
    return np.stack([(s <= r), (s >= r)]).astype(np.float32)


def _hgrn_split_rows(g_scr, l, h, d):
    assert h in (2, 4)
    off = h - 1 if d == 0 else h
    n = g_scr.shape[-1]
    sub = lax.broadcasted_iota(jnp.int32, (8, n), 0)
    pieces = []
    for grp in range(l // 8):
        lo = jnp.broadcast_to(g_scr[grp * 8 + off:grp * 8 + off + 1, :], (8, n))
        if h == 4:
            pieces.append(lo)
        else:
            hi = jnp.broadcast_to(g_scr[grp * 8 + 4 + off:grp * 8 + 4 + off + 1, :], (8, n))
            pieces.append(jnp.where(sub < 4, lo, hi))
    return jnp.concatenate(pieces, axis=0)


def _hgrn_kernel(qf_ref, vf_ref, kf_ref, lff_ref, qb_ref, vb_ref, kb_ref, lfb_ref,
                 tri_ref, mfull_ref, mhalf_ref, of_ref, ob_ref, st_scr, g_scr, *, l, dh):
    @pl.when(pl.program_id(1) == 0)
    def _():
        st_scr[...] = jnp.zeros_like(st_scr)

    heads = [slice(h * dh, (h + 1) * dh) for h in range(H_B)]
    dirs = ((qf_ref, vf_ref, kf_ref, lff_ref, of_ref), (qb_ref, vb_ref, kb_ref, lfb_ref, ob_ref))
    for d, (q_ref, v_ref, k_ref, lf_ref, o_ref) in enumerate(dirs):
        g_scr[...] = _cumsum_mm(tri_ref[d], lf_ref[0], terms=HGRN_CUMSUM_TERMS)
        q = q_ref[0]
        kk = k_ref[0]
        v = v_ref[0]

        def scores(qs, ks, mask):
            return [_dot_nt(qs[:, hs], ks[:, hs]).astype(BF16) * mask for hs in heads]

        acc = scores(q, kk, mfull_ref[d, 0])
        n_full, n_half = 1, 0
        for hl in _hgrn_levels(l):
            m_off = hl - 1 if d == 0 else hl
            q_rows = _hgrn_q_rows(l, hl, d)
            k_rows = _hgrn_q_rows(l, hl, 1 - d)
            if hl < HGRN_COMPACT_MIN:
                if hl == 1:
                    qs, ks = q * jnp.exp2(lf_ref[0]).astype(BF16), kk
                else:
                    if hl < 8:
                        x = -jnp.abs(g_scr[...] - _hgrn_split_rows(g_scr, l, hl, d))
                    else:
                        parts = {}
                        for (q0, q1), (k0, k1) in zip(q_rows, k_rows):
                            gm = g_scr[min(q0, k0) + m_off:min(q0, k0) + m_off + 1, :]
                            parts[q0] = g_scr[q0:q1, :] - gm
                            parts[k0] = gm - g_scr[k0:k1, :]
                        x = jnp.concatenate([parts[r0] for r0 in sorted(parts)], axis=0)
                    e = jnp.exp2(x).astype(BF16)
                    qs, ks = q * e, kk * e
                acc = [a + t for a, t in zip(acc, scores(qs, ks, mfull_ref[d, n_full]))]
                n_full += 1
            else:
                xq, xk = [], []
                for (q0, q1), (k0, k1) in zip(q_rows, k_rows):
                    gm = g_scr[min(q0, k0) + m_off:min(q0, k0) + m_off + 1, :]
                    xq.append(g_scr[q0:q1, :] - gm)
                    xk.append(gm - g_scr[k0:k1, :])
                eq = jnp.exp2(jnp.concatenate(xq, axis=0)).astype(BF16)
                ek = jnp.exp2(jnp.concatenate(xk, axis=0)).astype(BF16)
                qc = jnp.concatenate([q[q0:q1] for q0, q1 in q_rows], axis=0) * eq
                kparts = {q0: kk[q0:q1] for q0, q1 in q_rows}
                for n, (k0, k1) in enumerate(k_rows):
                    kparts[k0] = kk[k0:k1] * ek[n * hl:(n + 1) * hl]
                kt = jnp.concatenate([kparts[r0] for r0 in sorted(kparts)], axis=0)
                terms = scores(qc, kt, mhalf_ref[d, n_half])
                n_half += 1
                for i in range(H_B):
                    rows = {k0: acc[i][k0:k1] for k0, k1 in k_rows}
                    for n, (q0, q1) in enumerate(q_rows):
                        rows[q0] = acc[i][q0:q1] + terms[i][n * hl:(n + 1) * hl]
                    acc[i] = jnp.concatenate([rows[r0] for r0 in sorted(rows)], axis=0)
        last = l - 1 if d == 0 else 0
        g_all = g_scr[...]
        g_l = g_scr[last:last + 1, :]
        qi = q * jnp.exp2(g_all).astype(BF16)
        ks = kk * jnp.exp2(g_l - g_all).astype(BF16)
        dec = jnp.exp2(g_l)
        for i, hs in enumerate(heads):
            u = d * H_B + i
            st = st_scr[u]
            o = _dot_nt(qi[:, hs], st.astype(BF16)) + jnp.dot(acc[i], v[:, hs], preferred_element_type=F32)
            o_ref[0, :, hs] = o.astype(o_ref.dtype)
            st_scr[u] = st * dec[:, hs] + _dot_tn(v[:, hs], ks[:, hs])


def _hgrn(mix, kdec, lf, ctx_len, d_b):
    nb, tb, _ = mix.shape
    l = L_B
    dh = d_b // H_B
    assert dh == LANES and ctx_len % l == 0 and tb % l == 0
    nc, ncc = tb // l, ctx_len // l
    bw = functools.partial(_bwd_chunk, n_ctx_chunks=ncc, n_chunks=nc)
    kern = functools.partial(_hgrn_kernel, l=l, dh=dh)
    tri = jnp.asarray(_hgrn_tri(l), BF16)
    mfull, mhalf = _hgrn_masks(l)
    mfull, mhalf = jnp.asarray(mfull, BF16), jnp.asarray(mhalf, BF16)
    oshape = jax.ShapeDtypeStruct((nb, tb, d_b), BF16)
    blk = lambda col, chunk: pl.BlockSpec((1, l, d_b), lambda b, c: (b, chunk(c), col))
    ident = lambda c: c
    const = lambda a: pl.BlockSpec(a.shape, lambda b, c: (0,) * a.ndim)
    return pl.pallas_call(
        kern,
        out_shape=(oshape, oshape),
        grid=(nb, nc),
        in_specs=[blk(MIX_QB, ident), blk(MIX_IB, ident), blk(0, ident), blk(0, ident),
                  blk(MIX_QB, bw), blk(MIX_IB, bw), blk(1, bw), blk(1, bw),
                  const(tri), const(mfull), const(mhalf)],
        out_specs=(pl.BlockSpec((1, l, d_b), lambda b, c: (b, c, 0)),
                   pl.BlockSpec((1, l, d_b), lambda b, c: (b, bw(c), 0))),
        scratch_shapes=[pltpu.VMEM((2 * H_B, dh, dh), F32),
                        pltpu.VMEM((l, d_b), F32)],
        compiler_params=_cparams(("arbitrary", "arbitrary")),
        name="hgrn",
    )(mix, mix, kdec, lf, mix, mix, kdec, lf, tri, mfull, mhalf)


def _head_rms(x, n_heads):
    dh = x.shape[-1] // n_heads
    outs = []
    for h in range(n_heads):
        xh = x[:, h * dh:(h + 1) * dh]
        outs.append(xh * lax.rsqrt(jnp.mean(xh * xh, axis=-1, keepdims=True) + EPS))
    return jnp.concatenate(outs, axis=-1)


def _merge_kernel(x_ref, haf_ref, hab_ref, hbf_ref, hbb_ref, o_ref, za_ref, zb_ref, ma_ref, mb_ref,
                  mod_ref, gha_ref, ghb_ref, wa_ref, wb_ref, wo_ref, out_ref, *, ctx_len, tm, n_batch):
    b = pl.program_id(0)
    i = pl.program_id(1)
    d = x_ref.shape[-1]
    f32 = lambda ref: ref[0].astype(F32)
    h_a = (f32(haf_ref) + f32(hab_ref)) * f32(o_ref)
    y_a = _head_rms(h_a, H_A) * gha_ref[...] * f32(za_ref)
    y_b = _head_rms(f32(hbf_ref) + f32(hbb_ref), H_B) * ghb_ref[...] * f32(zb_ref)
    pa = jnp.dot(y_a.astype(BF16), wa_ref[...], preferred_element_type=F32)
    pb = jnp.dot(y_b.astype(BF16), wb_ref[...], preferred_element_type=F32)
    y = f32(ma_ref) * pa + f32(mb_ref) * pb
    br = jnp.dot(y.astype(BF16), wo_ref[...], preferred_element_type=F32)
    row = i * tm + lax.broadcasted_iota(jnp.int32, (tm, 1), 0)
    gate = jnp.where(row < ctx_len, mod_ref[pl.ds(n_batch, 1), 2 * d:3 * d],
                     mod_ref[pl.ds(b, 1), 2 * d:3 * d])
    out_ref[0] = x_ref[0] + gate * br


def _merge(xall, haf, hab, hbf, hbb, sig, silu, mod_l, gha, ghb, wa, wb, wo, ctx_len):
    nb, tb, d = xall.shape
    tm = 256
    kern = functools.partial(_merge_kernel, ctx_len=ctx_len, tm=tm, n_batch=nb)
    row = pl.BlockSpec((1, tm, d), lambda b, i: (b, i, 0))
    pcol = lambda col: pl.BlockSpec((1, tm, d), lambda b, i: (b, i, col))
    full = lambda shape: pl.BlockSpec(shape, lambda b, i: (0,) * len(shape))
    return pl.pallas_call(
        kern,
        out_shape=jax.ShapeDtypeStruct(xall.shape, F32),
        grid=(nb, tb // tm),
        in_specs=[row, row, row, row, row,
                  pcol(SIG_O), pcol(SILU_ZA), pcol(SILU_ZB), pcol(SIG_MA), pcol(SIG_MB),
                  full((8, 3 * d)), full((1, d)), full((1, d)),
                  full((d, d)), full((d, d)), full((d, d))],
        out_specs=row,
        compiler_params=_cparams(("arbitrary", "arbitrary")),
        name="merge",
    )(xall, haf, hab, hbf, hbb, sig, silu, silu, sig, sig, mod_l, gha, ghb, wa, wb, wo)


def _final_kernel(x_ref, g_ref, out_ref):
    x = x_ref[0]
    out_ref[0] = x * lax.rsqrt(jnp.mean(x * x, axis=-1, keepdims=True) + EPS) * g_ref[...]


def _final_norm(xall, g, ctx_len):
    nb, tb, d = xall.shape
    tm = 256
    off = ctx_len // tm
    return pl.pallas_call(
        _final_kernel,
        out_shape=jax.ShapeDtypeStruct((nb, tb - ctx_len, d), F32),
        grid=(nb, (tb - ctx_len) // tm),
        in_specs=[pl.BlockSpec((1, tm, d), lambda b, i: (b, i + off, 0)),
                  pl.BlockSpec((1, d), lambda b, i: (0, 0))],
        out_specs=pl.BlockSpec((1, tm, d), lambda b, i: (b, i, 0)),
        compiler_params=_cparams(("arbitrary", "arbitrary")),
        name="final_norm",
    )(xall, g)


def kernel(x, c, ctx, c_ctx, w_ada, b_ada, g_norm, w_in, b_in, w_conv, b_conv, lb_logits,
           g_head_a, g_head_b, w_a, w_b, w_out, g_final):
    nb, seq, d = x.shape
    ctx_len = ctx.shape[1]
    depth = w_ada.shape[0]
    d_a = g_head_a.shape[-1]
    d_b = g_head_b.shape[-1]
    assert d_a == d and d_b == d and nb + 1 <= 8
    dh_a = d_a // H_A

    g0 = 5 * d
    g1 = g0 + N_GATES
    grp = lambda n: slice(n * d, (n + 1) * d) if n < 5 else slice(g1 + (n - 5) * d, g1 + (n - 4) * d)
    QK0, QK1, V, O, ZA, QB, IB, FF, FB, ZB, MA, MB = range(12)
    order = (QK0, QK1, V, QB, IB, O, MA, MB, ZA, ZB, FF, FB)
    w_main = jnp.concatenate([w_in[:, :, grp(n)] for n in order], axis=-1).astype(BF16)
    b_main = jnp.concatenate([b_in[:, grp(n)] for n in order], axis=-1).astype(F32)
    wg = jnp.pad(w_in[:, :, g0:g1].astype(F32), ((0, 0), (0, 0), (0, LANES - N_GATES)))
    wg_hi = wg.astype(BF16)
    wg_lo = (wg - wg_hi.astype(F32)).astype(BF16)
    wg2 = jnp.stack([wg_hi, wg_lo], axis=1)
    bg = jnp.pad(b_in[:, g0:g1].astype(F32), ((0, 0), (0, LANES - N_GATES)))
    qk_scale = jnp.concatenate([jnp.ones((1, d_a), F32), jnp.full((1, d_a), dh_a ** -0.5, F32)], axis=-1)
    w9 = w_conv.reshape(depth, 9, 2 * d_a).astype(F32)

    cc = jnp.concatenate([c.astype(F32), c_ctx.astype(F32)[None, :],
                          jnp.zeros((8 - nb - 1, d), F32)], axis=0)
    mod = _ada(cc, w_ada.astype(F32), b_ada.astype(F32))
    lbs = _lbs(lb_logits)
    c_mix, c_sig, c_silu, c_dec = (int(n) for n in np.cumsum((N_RAW, N_MIX, N_SIG, N_SILU)))

    xall = jnp.concatenate([ctx.astype(F32), x.astype(F32)], axis=1)
    for l in range(depth):
        h16, g = _norm_gates(xall, mod[l], g_norm[l][None, :].astype(F32), wg2[l], bg[l][None, :], ctx_len)
        bm = b_main[l][None, :]
        raw = _proj(h16, w_main[l], bm, 0, N_RAW, "raw")
        mix = _proj(h16, w_main[l], bm, c_mix, N_MIX, "cast")
        sig = _proj(h16, w_main[l], bm, c_sig, N_SIG, "sigmoid")
        silu = _proj(h16, w_main[l], bm, c_silu, N_SILU, "silu")
        lf, kdec = _proj(h16, w_main[l], bm, c_dec, N_DECAY, "decay", lbs[l][None, :])
        qk = _conv_silu(raw, w9[l], b_conv[l][None, :].astype(F32), qk_scale, ctx_len)
        haf, hab = _mlstm(qk, mix, g, ctx_len, d_a)
        hbf, hbb = _hgrn(mix, kdec, lf, ctx_len, d_b)
        xall = _merge(xall, haf, hab, hbf, hbb, sig, silu, mod[l], g_head_a[l][None, :].astype(F32),
                      g_head_b[l][None, :].astype(F32), w_a[l].astype(BF16), w_b[l].astype(BF16),
                      w_out[l].astype(BF16), ctx_len)
    return _final_norm(xall, g_final[None, :].astype(F32), ctx_len).astype(x.dtype)
```

```python
import functools
import math

import numpy as np
import jax
import jax.numpy as jnp
from jax import lax
from jax.experimental import pallas as pl
from jax.experimental.pallas import tpu as pltpu

F32 = jnp.float32
BF16 = jnp.bfloat16

H_A = 4
H_B = 8
GRID_W = 64
EPS = 1e-6
NEG = -1e30
LOG2E = math.log2(math.e)
N_GATES = 4 * H_A
LANES = 128
L_A = 256
L_B = 128
HGRN_COMPACT_MIN = 16
CONV_ROWS = 256
CONV_HALO = 128
HGRN_CUMSUM_TERMS = 2
CONV_COLS = 2048
PROJ_ROWS = 256
V7X_VMEM_LIMIT = 56 * 1024 * 1024
N_MIX, N_SIG, N_RAW, N_SILU, N_DECAY = 3, 3, 2, 2, 2
MIX_V, MIX_QB, MIX_IB = 0, 1, 2
SIG_O, SIG_MA, SIG_MB = 0, 1, 2
SILU_ZA, SILU_ZB = 0, 1


def _cparams(semantics):
    return pltpu.CompilerParams(dimension_semantics=semantics, vmem_limit_bytes=V7X_VMEM_LIMIT)


def _sigmoid(x):
    return 1.0 / (1.0 + jnp.exp(-x))


def _split3(x):
    hi = x.astype(BF16)
    r1 = x - hi.astype(F32)
    mid = r1.astype(BF16)
    lo = (r1 - mid.astype(F32)).astype(BF16)
    return hi, mid, lo


def _cumsum_mm(tri, x, terms=3):
    out = None
    for part in _split3(x)[:terms]:
        term = jnp.dot(tri, part, preferred_element_type=F32)
        out = term if out is None else out + term
    return out


def _dot_nt(a, b):
    return lax.dot_general(a, b, (((1,), (1,)), ((), ())), preferred_element_type=F32)


def _dot_tn(a, b):
    return lax.dot_general(a, b, (((0,), (0,)), ((), ())), preferred_element_type=F32)


def _ada_kernel(cc_ref, w_ref, b_ref, out_ref):
    s = cc_ref[...]
    s = s * _sigmoid(s)
    w = w_ref[0]
    s_hi = s.astype(BF16)
    s_lo = (s - s_hi.astype(F32)).astype(BF16)
    w_hi = w.astype(BF16)
    w_lo = (w - w_hi.astype(F32)).astype(BF16)
    d = lambda a, b: jnp.dot(a, b, preferred_element_type=F32)
    out_ref[0] = d(s_hi, w_hi) + d(s_lo, w_hi) + d(s_hi, w_lo) + b_ref[0]


def _ada(cc, w_ada, b_ada):
    depth, d, n3 = w_ada.shape
    tn = 512
    return pl.pallas_call(
        _ada_kernel,
        out_shape=jax.ShapeDtypeStruct((depth, 8, n3), F32),
        grid=(depth, n3 // tn),
        in_specs=[pl.BlockSpec((8, d), lambda l, j: (0, 0)),
                  pl.BlockSpec((1, d, tn), lambda l, j: (l, 0, j)),
                  pl.BlockSpec((1, 1, tn), lambda l, j: (l, 0, j))],
        out_specs=pl.BlockSpec((1, 8, tn), lambda l, j: (l, 0, j)),
        compiler_params=_cparams(("arbitrary", "arbitrary")),
        name="ada",
    )(cc, w_ada, b_ada.reshape(depth, 1, n3))


def _lbs_kernel(lg_ref, out_ref):
    x = lg_ref[...]
    depth = x.shape[0]
    m = jnp.max(x, axis=0, keepdims=True)
    e = jnp.exp(x - m)
    p = e / jnp.sum(e, axis=0, keepdims=True)
    acc = jnp.zeros_like(p[0:1])
    for l in range(depth):
        acc = acc + p[l:l + 1]
        out_ref[l:l + 1, :] = acc - p[0:1]


def _lbs(lb_logits):
    return pl.pallas_call(
        _lbs_kernel, out_shape=jax.ShapeDtypeStruct(lb_logits.shape, F32), name="lbs",
    )(lb_logits.astype(F32))


def _norm_modulate(x, is_ctx, b, mod_ref, gn_ref, wg_ref, bg_ref, n_batch):
    d = x.shape[-1]
    ms = jnp.mean(x * x, axis=-1, keepdims=True)
    y = x * lax.rsqrt(ms + EPS) * gn_ref[...]
    mb = mod_ref[pl.ds(b, 1), :]
    mc = mod_ref[pl.ds(n_batch, 1), :]
    shift = jnp.where(is_ctx, mc[:, 0:d], mb[:, 0:d])
    scale = jnp.where(is_ctx, mc[:, d:2 * d], mb[:, d:2 * d])
    h = y * (1.0 + scale) + shift
    hi = h.astype(BF16)
    lo = (h - hi.astype(F32)).astype(BF16)
    dd = lambda a, bb: jnp.dot(a, bb, preferred_element_type=F32)
    return hi, dd(hi, wg_ref[0]) + dd(lo, wg_ref[0]) + dd(hi, wg_ref[1]) + bg_ref[...]


def _norm_kernel(x_ref, mod_ref, gn_ref, wg_ref, bg_ref, h_ref, g_ref, *, ctx_len, tm, n_batch):
    b = pl.program_id(0)
    row = pl.program_id(1) * tm + lax.broadcasted_iota(jnp.int32, (tm, 1), 0)
    h_ref[0], g_ref[0] = _norm_modulate(x_ref[0], row < ctx_len, b, mod_ref, gn_ref, wg_ref, bg_ref, n_batch)


def _pick_tile(n, candidates):
    for c in candidates:
        if n % c == 0:
            return c
    raise ValueError(f"no tile for {n}")


def _norm_gates(xall, mod_l, gn, wg, bg, ctx_len):
    nb, tb, d = xall.shape
    tm = _pick_tile(tb, (640, 256))
    kern = functools.partial(_norm_kernel, ctx_len=ctx_len, tm=tm, n_batch=nb)
    return pl.pallas_call(
        kern,
        out_shape=(jax.ShapeDtypeStruct((nb, tb, d), BF16),
                   jax.ShapeDtypeStruct((nb, tb, LANES), F32)),
        grid=(nb, tb // tm),
        in_specs=[pl.BlockSpec((1, tm, d), lambda b, i: (b, i, 0)),
                  pl.BlockSpec((8, 3 * d), lambda b, i: (0, 0)),
                  pl.BlockSpec((1, d), lambda b, i: (0, 0)),
                  pl.BlockSpec((2, d, LANES), lambda b, i: (0, 0, 0)),
                  pl.BlockSpec((1, LANES), lambda b, i: (0, 0))],
        out_specs=(pl.BlockSpec((1, tm, d), lambda b, i: (b, i, 0)),
                   pl.BlockSpec((1, tm, LANES), lambda b, i: (b, i, 0))),
        compiler_params=_cparams(("arbitrary", "arbitrary")),
        name="norm_gates",
    )(xall, mod_l, gn, wg, bg)


def _proj_kernel(*refs, kind, rs):
    if kind == "decay":
        h_ref, w_ref, b_ref, lb_ref, lf_ref, k_ref = refs
    else:
        h_ref, w_ref, b_ref, out_ref = refs
    tm = h_ref.shape[1]
    for r0 in range(0, tm, rs):
        rows = slice(r0, r0 + rs)
        acc = jnp.dot(h_ref[0, rows, :], w_ref[...], preferred_element_type=F32) + b_ref[...]
        if kind == "raw":
            out_ref[0, rows, :] = acc
        elif kind == "cast":
            out_ref[0, rows, :] = acc.astype(BF16)
        elif kind == "sigmoid":
            out_ref[0, rows, :] = _sigmoid(acc).astype(BF16)
        elif kind == "silu":
            out_ref[0, rows, :] = (acc * _sigmoid(acc)).astype(BF16)
        else:
            lbv = lb_ref[...]
            t = jnp.exp2(jnp.abs(acc) * (-LOG2E))
            pos = acc >= 0.0
            rc = 1.0 / (1.0 + t)
            f = jnp.where(pos, 1.0 + lbv * t, t + lbv) * rc
            lf_ref[0, rows, :] = jnp.log(f) * LOG2E
            k_ref[0, rows, :] = ((1.0 - lbv) * jnp.where(pos, t, 1.0) * rc).astype(BF16)


def _proj(h16, w_main, b_main, col0, n_groups, kind, lb=None):
    nb, tb, d = h16.shape
    tm = _pick_tile(tb, (1280, 640, 256))
    tn = n_groups * d
    off = col0 // n_groups
    assert col0 % n_groups == 0
    kern = functools.partial(_proj_kernel, kind=kind, rs=PROJ_ROWS)
    in_specs = [pl.BlockSpec((1, tm, d), lambda b, i: (b, i, 0)),
                pl.BlockSpec((d, tn), lambda b, i: (0, off)),
                pl.BlockSpec((1, tn), lambda b, i: (0, off))]
    args = [h16, w_main, b_main]
    oblk = pl.BlockSpec((1, tm, tn), lambda b, i: (b, i, 0))
    shp = lambda dt: jax.ShapeDtypeStruct((nb, tb, tn), dt)
    if kind == "decay":
        in_specs.append(pl.BlockSpec((1, tn), lambda b, i: (0, 0)))
        args.append(jnp.concatenate([lb] * n_groups, axis=-1))
        out_shape, out_specs = (shp(F32), shp(BF16)), (oblk, oblk)
    else:
        out_shape, out_specs = shp(F32 if kind == "raw" else BF16), oblk
    return pl.pallas_call(
        kern,
        out_shape=out_shape,
        grid=(nb, tb // tm),
        in_specs=in_specs,
        out_specs=out_specs,
        compiler_params=_cparams(("arbitrary", "arbitrary")),
        name="proj_" + kind,
    )(*args)


def _conv_kernel(prev_ref, cur_ref, next_ref, w_ref, bc_ref, sc_ref, out_ref, *, n_tiles, tr, halo):
    i = pl.program_id(1)
    is_ctx = i == 0
    prev_ok = i >= 2
    next_ok = jnp.logical_and(i >= 1, i < n_tiles - 1)
    margin = GRID_W + 8
    full = jnp.concatenate([jnp.where(prev_ok, prev_ref[0, halo - margin:halo, :], 0.0), cur_ref[0],
                            jnp.where(next_ok, next_ref[0, 0:margin, :], 0.0)], axis=0)
    w = w_ref[...]
    sums = []
    for dc in (-1, 0, 1):
        acc = None
        for dr in (-1, 0, 1):
            tap = (dr + 1) * 3 + (dc + 1)
            wt = w[tap:tap + 1, :]
            if dr != 0:
                wt = jnp.where(is_ctx, 0.0, wt)
            start = margin + GRID_W * dr - 8
            term = wt * full[start:start + tr + 16, :]
            acc = term if acc is None else acc + term
        sums.append(acc)
    col = lax.broadcasted_iota(jnp.int32, (tr, 1), 0) % GRID_W
    rows = tr + 16
    left = pltpu.roll(sums[0], 1, axis=0)[8:8 + tr, :]
    right = pltpu.roll(sums[2], rows - 1, axis=0)[8:8 + tr, :]
    y = (sums[1][8:8 + tr, :] + jnp.where(jnp.logical_or(is_ctx, col != 0), left, 0.0)
         + jnp.where(jnp.logical_or(is_ctx, col != GRID_W - 1), right, 0.0) + bc_ref[...])
    out_ref[0] = (y * _sigmoid(y) * sc_ref[...]).astype(out_ref.dtype)


def _conv_silu(raw, w9, bconv, scale, ctx_len):
    nb, tb, n_qk = raw.shape
    tr, halo, tc = CONV_ROWS, CONV_HALO, CONV_COLS
    assert ctx_len == tr and tb % tr == 0 and tr % GRID_W == 0 and halo % GRID_W == 0
    n_tiles = tb // tr
    per = tr // halo
    n_halo = tb // halo
    kern = functools.partial(_conv_kernel, n_tiles=n_tiles, tr=tr, halo=halo)
    return pl.pallas_call(
        kern,
        out_shape=jax.ShapeDtypeStruct((nb, tb, n_qk), BF16),
        grid=(nb, n_tiles, n_qk // tc),
        in_specs=[pl.BlockSpec((1, halo, tc), lambda b, i, j: (b, jnp.maximum(i * per - 1, 0), j)),
                  pl.BlockSpec((1, tr, tc), lambda b, i, j: (b, i, j)),
                  pl.BlockSpec((1, halo, tc), lambda b, i, j: (b, jnp.minimum((i + 1) * per, n_halo - 1), j)),
                  pl.BlockSpec((9, tc), lambda b, i, j: (0, j)),
                  pl.BlockSpec((1, tc), lambda b, i, j: (0, j)),
                  pl.BlockSpec((1, tc), lambda b, i, j: (0, j))],
        out_specs=pl.BlockSpec((1, tr, tc), lambda b, i, j: (b, i, j)),
        compiler_params=_cparams(("arbitrary", "arbitrary", "arbitrary")),
        name="conv_silu",
    )(raw, raw, raw, w9, bconv, scale)


def _mlstm_kernel(qkf_ref, vf_ref, gf_ref, qkb_ref, vb_ref, gb_ref, hf_ref, hb_ref,
                  c_scr, n_scr, m_scr, *, la, dh):
    @pl.when(pl.program_id(1) == 0)
    def _():
        c_scr[...] = jnp.zeros_like(c_scr)
        n_scr[...] = jnp.zeros_like(n_scr)
        m_scr[...] = jnp.zeros_like(m_scr)

    r = lax.broadcasted_iota(jnp.int32, (la, la), 0)
    s = lax.broadcasted_iota(jnp.int32, (la, la), 1)
    ones_blk = jnp.ones((la, LANES), BF16)
    wide = lambda x, n: jnp.concatenate([x] * (n // LANES), axis=1)
    dirs = ((qkf_ref, vf_ref, gf_ref, hf_ref), (qkb_ref, vb_ref, gb_ref, hb_ref))
    for d, (qk_ref, v_ref, g_ref, h_ref) in enumerate(dirs):
        seen = (s <= r) if d == 0 else (s >= r)
        tri = jnp.where(seen, 1.0, 0.0).astype(BF16)
        g = g_ref[0]
        lsig = jnp.minimum(g, 0.0) - jnp.log(1.0 + jnp.exp(-jnp.abs(g)))
        bc = _cumsum_mm(tri, lsig)
        b2 = pltpu.roll(bc, LANES - H_A, axis=1) * LOG2E
        r2 = g * LOG2E - b2
        r2_t = r2.T
        last = la - 1 if d == 0 else 0
        b2_l = b2[last:last + 1, :]
        m2 = m_scr[d:d + 1, :]
        ws2 = r2 + b2_l
        m2_new = jnp.maximum(b2_l + m2, jnp.max(ws2, axis=0, keepdims=True))
        decay = jnp.exp2(b2_l + m2 - m2_new)
        wse = jnp.exp2(ws2 - m2_new)
        m_scr[d:d + 1, :] = m2_new
        cols = [2 * d * H_A + h for h in range(H_A)]
        q = [qk_ref[0, :, h * dh:(h + 1) * dh] for h in range(H_A)]
        k = [qk_ref[0, :, (H_A + h) * dh:(H_A + h + 1) * dh] for h in range(H_A)]
        v = [v_ref[0, :, h * dh:(h + 1) * dh] for h in range(H_A)]
        col = lambda x, ci: jnp.broadcast_to(x[:, ci:ci + 1], (la, LANES))
        rm = [jnp.where(seen, r2_t[ci:ci + 1, :], NEG) for ci in cols]
        mx = [jnp.broadcast_to(jnp.maximum(m2[:, ci:ci + 1], jnp.max(rm_h, axis=-1, keepdims=True)),
                               (la, LANES)) for ci, rm_h in zip(cols, rm)]
        s_mat = [_dot_nt(q[h], k[h]) for h in range(H_A)]
        for h, ci in enumerate(cols):
            u = d * H_A + h
            w16 = (s_mat[h] * jnp.exp2(rm[h] - wide(mx[h], la))).astype(BF16)
            a_in = jnp.exp2(m2[:, ci:ci + 1] - mx[h])
            em = jnp.exp2(-(col(b2, ci) + mx[h]))
            c_st = c_scr[u]
            n_st = n_scr[u]
            qa = q[h] * wide(a_in.astype(BF16), dh)
            num = jnp.dot(jnp.concatenate([w16, qa], axis=1),
                          jnp.concatenate([v[h], c_st.astype(BF16)], axis=0), preferred_element_type=F32)
            qn = _dot_nt(q[h], jnp.broadcast_to(n_st, (LANES, dh)).astype(BF16))
            den = a_in * qn + jnp.dot(w16, ones_blk, preferred_element_type=F32)
            rcp = 1.0 / jnp.maximum(jnp.abs(den), em)
            h_ref[0, :, h * dh:(h + 1) * dh] = (num * wide(rcp, dh)).astype(h_ref.dtype)
            wse_u = col(wse, ci).astype(BF16)
            dec = decay[:, ci:ci + 1]
            c_scr[u] = dec * c_st + _dot_tn(k[h] * wide(wse_u, dh), v[h])
            n_scr[u] = dec * n_st + _dot_tn(wse_u, k[h])[0:1, :]


def _bwd_chunk(c, n_ctx_chunks, n_chunks):
    return jnp.where(c < n_ctx_chunks, n_ctx_chunks - 1 - c, n_chunks - 1 + n_ctx_chunks - c)


def _mlstm(qk, mix, g, ctx_len, d_a):
    nb, tb, _ = qk.shape
    la = L_A
    dh = d_a // H_A
    assert ctx_len % la == 0 and tb % la == 0
    nc, ncc = tb // la, ctx_len // la
    bw = functools.partial(_bwd_chunk, n_ctx_chunks=ncc, n_chunks=nc)
    kern = functools.partial(_mlstm_kernel, la=la, dh=dh)
    hshape = jax.ShapeDtypeStruct((nb, tb, d_a), BF16)
    return pl.pallas_call(
        kern,
        out_shape=(hshape, hshape),
        grid=(nb, nc),
        in_specs=[pl.BlockSpec((1, la, 2 * d_a), lambda b, c: (b, c, 0)),
                  pl.BlockSpec((1, la, d_a), lambda b, c: (b, c, MIX_V)),
                  pl.BlockSpec((1, la, LANES), lambda b, c: (b, c, 0)),
                  pl.BlockSpec((1, la, 2 * d_a), lambda b, c: (b, bw(c), 0)),
                  pl.BlockSpec((1, la, d_a), lambda b, c: (b, bw(c), MIX_V)),
                  pl.BlockSpec((1, la, LANES), lambda b, c: (b, bw(c), 0))],
        out_specs=(pl.BlockSpec((1, la, d_a), lambda b, c: (b, c, 0)),
                   pl.BlockSpec((1, la, d_a), lambda b, c: (b, bw(c), 0))),
        scratch_shapes=[pltpu.VMEM((2 * H_A, dh, dh), F32),
                        pltpu.VMEM((2 * H_A, 1, dh), F32),
                        pltpu.VMEM((8, LANES), F32)],
        compiler_params=_cparams(("arbitrary", "arbitrary")),
        name="mlstm",
    )(qk, mix, g, qk, mix, g)


def _hgrn_levels(l):
    return [1 << j for j in range(int(np.log2(l)))]


def _hgrn_q_rows(l, h, d):
    off = h if d == 0 else 0
    return [(k * 2 * h + off, k * 2 * h + off + h) for k in range(l // (2 * h))]


def _hgrn_masks(l):
    r = np.arange(l)[:, None]
    s = np.arange(l)[None, :]
    full, half = [[], []], [[], []]
    for d in (0, 1):
        full[d].append(r == s)
        for h in _hgrn_levels(l):
            same = (r // (2 * h)) == (s // (2 * h))
            m = same & ((r % (2 * h)) >= h) & ((s % (2 * h)) < h)
            m = m if d == 0 else m.T
            if h >= HGRN_COMPACT_MIN:
                half[d].append(np.concatenate([m[a:b] for a, b in _hgrn_q_rows(l, h, d)]))
            else:
                full[d].append(m)
    return np.array(full, np.float32), np.array(half, np.float32)


def _hgrn_tri(l):
    r = np.arange(l)[:, None]
    s = np.arange(l)[None, :]
    return np.stack([(s <= r), (s >= r)]).astype(np.float32)


def _hgrn_split_rows(g_scr, l, h, d):
    assert h in (2, 4)
    off = h - 1 if d == 0 else h
    n = g_scr.shape[-1]
    sub = lax.broadcasted_iota(jnp.int32, (8, n), 0)
    pieces = []
    for grp in range(l // 8):
        lo = jnp.broadcast_to(g_scr[grp * 8 + off:grp * 8 + off + 1, :], (8, n))
        if h == 4:
            pieces.append(lo)
        else:
            hi = jnp.broadcast_to(g_scr[grp * 8 + 4 + off:grp * 8 + 4 + off + 1, :], (8, n))
            pieces.append(jnp.where(sub < 4, lo, hi))
    return jnp.concatenate(pieces, axis=0)


def _hgrn_kernel(qf_ref, vf_ref, kf_ref, lff_ref, qb_ref, vb_ref, kb_ref, lfb_ref,
                 tri_ref, mfull_ref, mhalf_ref, of_ref, ob_ref, st_scr, g_scr, *, l, dh):
    @pl.when(pl.program_id(1) == 0)
    def _():
        st_scr[...] = jnp.zeros_like(st_scr)

    heads = [slice(h * dh, (h + 1) * dh) for h in range(H_B)]
    dirs = ((qf_ref, vf_ref, kf_ref, lff_ref, of_ref), (qb_ref, vb_ref, kb_ref, lfb_ref, ob_ref))
    for d in (0, 1):
        g_scr[d] = _cumsum_mm(tri_ref[d], dirs[d][3][0], terms=HGRN_CUMSUM_TERMS)
    q = [dirs[d][0][0] for d in (0, 1)]
    kk = [dirs[d][2][0] for d in (0, 1)]

    def scores(qs, ks, mask):
        return [_dot_nt(qs[:, hs], ks[:, hs]).astype(BF16) * mask for hs in heads]

    acc = [scores(q[d], kk[d], mfull_ref[d, 0]) for d in (0, 1)]
    n_full, n_half = 1, 0
    for hl in _hgrn_levels(l):
        for d in (0, 1):
            g = g_scr.at[d]
            m_off = hl - 1 if d == 0 else hl
            q_rows = _hgrn_q_rows(l, hl, d)
            k_rows = _hgrn_q_rows(l, hl, 1 - d)
            if hl < HGRN_COMPACT_MIN:
                if hl == 1:
                    qs, ks = q[d] * jnp.exp2(dirs[d][3][0]).astype(BF16), kk[d]
                else:
                    if hl < 8:
                        x = -jnp.abs(g[...] - _hgrn_split_rows(g, l, hl, d))
                    else:
                        parts = {}
                        for (q0, q1), (k0, k1) in zip(q_rows, k_rows):
                            gm = g[min(q0, k0) + m_off:min(q0, k0) + m_off + 1, :]
                            parts[q0] = g[q0:q1, :] - gm
                            parts[k0] = gm - g[k0:k1, :]
                        x = jnp.concatenate([parts[r0] for r0 in sorted(parts)], axis=0)
                    e = jnp.exp2(x).astype(BF16)
                    qs, ks = q[d] * e, kk[d] * e
                acc[d] = [a + t for a, t in zip(acc[d], scores(qs, ks, mfull_ref[d, n_full]))]
            else:
                xq, xk = [], []
                for (q0, q1), (k0, k1) in zip(q_rows, k_rows):
                    gm = g[min(q0, k0) + m_off:min(q0, k0) + m_off + 1, :]
                    xq.append(g[q0:q1, :] - gm)
                    xk.append(gm - g[k0:k1, :])
                eq = jnp.exp2(jnp.concatenate(xq, axis=0)).astype(BF16)
                ek = jnp.exp2(jnp.concatenate(xk, axis=0)).astype(BF16)
                qc = jnp.concatenate([q[d][q0:q1] for q0, q1 in q_rows], axis=0) * eq
                kparts = {q0: kk[d][q0:q1] for q0, q1 in q_rows}
                for n, (k0, k1) in enumerate(k_rows):
                    kparts[k0] = kk[d][k0:k1] * ek[n * hl:(n + 1) * hl]
                kt = jnp.concatenate([kparts[r0] for r0 in sorted(kparts)], axis=0)
                terms = scores(qc, kt, mhalf_ref[d, n_half])
                for i in range(H_B):
                    rows = {k0: acc[d][i][k0:k1] for k0, k1 in k_rows}
                    for n, (q0, q1) in enumerate(q_rows):
                        rows[q0] = acc[d][i][q0:q1] + terms[i][n * hl:(n + 1) * hl]
                    acc[d][i] = jnp.concatenate([rows[r0] for r0 in sorted(rows)], axis=0)
        if hl < HGRN_COMPACT_MIN:
            n_full += 1
        else:
            n_half += 1
    for d in (0, 1):
        o_ref = dirs[d][4]
        v = dirs[d][1][0]
        last = l - 1 if d == 0 else 0
        g_all = g_scr[d]
        g_l = g_scr[d, last:last + 1, :]
        qi = q[d] * jnp.exp2(g_all).astype(BF16)
        ks = kk[d] * jnp.exp2(g_l - g_all).astype(BF16)
        dec = jnp.exp2(g_l)
        for i, hs in enumerate(heads):
            u = d * H_B + i
            st = st_scr[u]
            o = _dot_nt(qi[:, hs], st.astype(BF16)) + jnp.dot(acc[d][i], v[:, hs], preferred_element_type=F32)
            o_ref[0, :, hs] = o.astype(o_ref.dtype)
            st_scr[u] = st * dec[:, hs] + _dot_tn(v[:, hs], ks[:, hs])


def _hgrn(mix, kdec, lf, ctx_len, d_b):
    nb, tb, _ = mix.shape
    l = L_B
    dh = d_b // H_B
    assert dh == LANES and ctx_len % l == 0 and tb % l == 0
    nc, ncc = tb // l, ctx_len // l
    bw = functools.partial(_bwd_chunk, n_ctx_chunks=ncc, n_chunks=nc)
    kern = functools.partial(_hgrn_kernel, l=l, dh=dh)
    tri = jnp.asarray(_hgrn_tri(l), BF16)
    mfull, mhalf = _hgrn_masks(l)
    mfull, mhalf = jnp.asarray(mfull, BF16), jnp.asarray(mhalf, BF16)
    oshape = jax.ShapeDtypeStruct((nb, tb, d_b), BF16)
    blk = lambda col, chunk: pl.BlockSpec((1, l, d_b), lambda b, c: (b, chunk(c), col))
    ident = lambda c: c
    const = lambda a: pl.BlockSpec(a.shape, lambda b, c: (0,) * a.ndim)
    return pl.pallas_call(
        kern,
        out_shape=(oshape, oshape),
        grid=(nb, nc),
        in_specs=[blk(MIX_QB, ident), blk(MIX_IB, ident), blk(0, ident), blk(0, ident),
                  blk(MIX_QB, bw), blk(MIX_IB, bw), blk(1, bw), blk(1, bw),
                  const(tri), const(mfull), const(mhalf)],
        out_specs=(pl.BlockSpec((1, l, d_b), lambda b, c: (b, c, 0)),
                   pl.BlockSpec((1, l, d_b), lambda b, c: (b, bw(c), 0))),
        scratch_shapes=[pltpu.VMEM((2 * H_B, dh, dh), F32),
                        pltpu.VMEM((2, l, d_b), F32)],
        compiler_params=_cparams(("arbitrary", "arbitrary")),
        name="hgrn",
    )(mix, mix, kdec, lf, mix, mix, kdec, lf, tri, mfull, mhalf)


def _head_rms(x, n_heads):
    dh = x.shape[-1] // n_heads
    outs = []
    for h in range(n_heads):
        xh = x[:, h * dh:(h + 1) * dh]
        outs.append(xh * lax.rsqrt(jnp.mean(xh * xh, axis=-1, keepdims=True) + EPS))
    return jnp.concatenate(outs, axis=-1)


def _merge_kernel(*refs, ctx_len, tm, n_batch, fuse_next):
    (x_ref, haf_ref, hab_ref, hbf_ref, hbb_ref, o_ref, za_ref, zb_ref, ma_ref, mb_ref,
     mod_ref, gha_ref, ghb_ref, wa_ref, wb_ref, wo_ref) = refs[:16]
    b = pl.program_id(0)
    d = x_ref.shape[-1]
    f32 = lambda ref: ref[0].astype(F32)
    h_a = (f32(haf_ref) + f32(hab_ref)) * f32(o_ref)
    y_a = _head_rms(h_a, H_A) * gha_ref[...] * f32(za_ref)
    y_b = _head_rms(f32(hbf_ref) + f32(hbb_ref), H_B) * ghb_ref[...] * f32(zb_ref)
    pa = jnp.dot(y_a.astype(BF16), wa_ref[...], preferred_element_type=F32)
    pb = jnp.dot(y_b.astype(BF16), wb_ref[...], preferred_element_type=F32)
    y = f32(ma_ref) * pa + f32(mb_ref) * pb
    br = jnp.dot(y.astype(BF16), wo_ref[...], preferred_element_type=F32)
    is_ctx = pl.program_id(1) * tm + lax.broadcasted_iota(jnp.int32, (tm, 1), 0) < ctx_len
    gate = jnp.where(is_ctx, mod_ref[pl.ds(n_batch, 1), 2 * d:3 * d], mod_ref[pl.ds(b, 1), 2 * d:3 * d])
    x_new = x_ref[0] + gate * br
    if fuse_next:
        nmod_ref, ngn_ref, nwg_ref, nbg_ref, out_ref, h_ref, g_ref = refs[16:]
        h_ref[0], g_ref[0] = _norm_modulate(x_new, is_ctx, b, nmod_ref, ngn_ref, nwg_ref, nbg_ref, n_batch)
    else:
        out_ref, = refs[16:]
    out_ref[0] = x_new


def _merge(xall, haf, hab, hbf, hbb, sig, silu, mod_l, gha, ghb, wa, wb, wo, ctx_len, nxt=None):
    nb, tb, d = xall.shape
    tm = 256
    kern = functools.partial(_merge_kernel, ctx_len=ctx_len, tm=tm, n_batch=nb, fuse_next=nxt is not None)
    row = pl.BlockSpec((1, tm, d), lambda b, i: (b, i, 0))
    pcol = lambda col: pl.BlockSpec((1, tm, d), lambda b, i: (b, i, col))
    full = lambda shape: pl.BlockSpec(shape, lambda b, i: (0,) * len(shape))
    in_specs = [row, row, row, row, row,
                pcol(SIG_O), pcol(SILU_ZA), pcol(SILU_ZB), pcol(SIG_MA), pcol(SIG_MB),
                full((8, 3 * d)), full((1, d)), full((1, d)),
                full((d, d)), full((d, d)), full((d, d))]
    args = [xall, haf, hab, hbf, hbb, sig, silu, silu, sig, sig, mod_l, gha, ghb, wa, wb, wo]
    out_shape = jax.ShapeDtypeStruct(xall.shape, F32)
    out_specs = row
    if nxt is not None:
        in_specs += [full((8, 3 * d)), full((1, d)), full((2, d, LANES)), full((1, LANES))]
        args += list(nxt)
        out_shape = (out_shape, jax.ShapeDtypeStruct((nb, tb, d), BF16),
                     jax.ShapeDtypeStruct((nb, tb, LANES), F32))
        out_specs = (row, row, pl.BlockSpec((1, tm, LANES), lambda b, i: (b, i, 0)))
    return pl.pallas_call(
        kern,
        out_shape=out_shape,
        grid=(nb, tb // tm),
        in_specs=in_specs,
        out_specs=out_specs,
        compiler_params=_cparams(("arbitrary", "arbitrary")),
        name="merge",
    )(*args)


def _final_kernel(x_ref, g_ref, out_ref):
    x = x_ref[0]
    out_ref[0] = x * lax.rsqrt(jnp.mean(x * x, axis=-1, keepdims=True) + EPS) * g_ref[...]


def _final_norm(xall, g, ctx_len):
    nb, tb, d = xall.shape
    tm = 256
    off = ctx_len // tm
    return pl.pallas_call(
        _final_kernel,
        out_shape=jax.ShapeDtypeStruct((nb, tb - ctx_len, d), F32),
        grid=(nb, (tb - ctx_len) // tm),
        in_specs=[pl.BlockSpec((1, tm, d), lambda b, i: (b, i + off, 0)),
                  pl.BlockSpec((1, d), lambda b, i: (0, 0))],
        out_specs=pl.BlockSpec((1, tm, d), lambda b, i: (b, i, 0)),
        compiler_params=_cparams(("arbitrary", "arbitrary")),
        name="final_norm",
    )(xall, g)


def kernel(x, c, ctx, c_ctx, w_ada, b_ada, g_norm, w_in, b_in, w_conv, b_conv, lb_logits,
           g_head_a, g_head_b, w_a, w_b, w_out, g_final):
    nb, seq, d = x.shape
    ctx_len = ctx.shape[1]
    depth = w_ada.shape[0]
    d_a = g_head_a.shape[-1]
    d_b = g_head_b.shape[-1]
    assert d_a == d and d_b == d and nb + 1 <= 8
    dh_a = d_a // H_A

    g0 = 5 * d
    g1 = g0 + N_GATES
    grp = lambda n: slice(n * d, (n + 1) * d) if n < 5 else slice(g1 + (n - 5) * d, g1 + (n - 4) * d)
    QK0, QK1, V, O, ZA, QB, IB, FF, FB, ZB, MA, MB = range(12)
    order = (V, QB, IB, O, MA, MB, QK0, QK1, ZA, ZB, FF, FB)
    w_main = jnp.concatenate([w_in[:, :, grp(n)] for n in order], axis=-1).astype(BF16)
    b_main = jnp.concatenate([b_in[:, grp(n)] for n in order], axis=-1).astype(F32)
    wg = jnp.pad(w_in[:, :, g0:g1].astype(F32), ((0, 0), (0, 0), (0, LANES - N_GATES)))
    wg_hi = wg.astype(BF16)
    wg_lo = (wg - wg_hi.astype(F32)).astype(BF16)
    wg2 = jnp.stack([wg_hi, wg_lo], axis=1)
    bg = jnp.pad(b_in[:, g0:g1].astype(F32), ((0, 0), (0, LANES - N_GATES)))
    qk_scale = jnp.concatenate([jnp.ones((1, d_a), F32), jnp.full((1, d_a), dh_a ** -0.5, F32)], axis=-1)
    w9 = w_conv.reshape(depth, 9, 2 * d_a).astype(F32)

    cc = jnp.concatenate([c.astype(F32), c_ctx.astype(F32)[None, :],
                          jnp.zeros((8 - nb - 1, d), F32)], axis=0)
    mod = _ada(cc, w_ada.astype(F32), b_ada.astype(F32))
    lbs = _lbs(lb_logits)
    c_sig, c_raw, c_silu, c_dec = (int(n) for n in np.cumsum((N_MIX, N_SIG, N_RAW, N_SILU)))

    xall = jnp.concatenate([ctx.astype(F32), x.astype(F32)], axis=1)
    norm_args = lambda l: (mod[l], g_norm[l][None, :].astype(F32), wg2[l], bg[l][None, :])
    h16, g = _norm_gates(xall, *norm_args(0), ctx_len)
    for l in range(depth):
        bm = b_main[l][None, :]
        raw = _proj(h16, w_main[l], bm, c_raw, N_RAW, "raw")
        mix = _proj(h16, w_main[l], bm, 0, N_MIX, "cast")
        sig = _proj(h16, w_main[l], bm, c_sig, N_SIG, "sigmoid")
        silu = _proj(h16, w_main[l], bm, c_silu, N_SILU, "silu")
        lf, kdec = _proj(h16, w_main[l], bm, c_dec, N_DECAY, "decay", lbs[l][None, :])
        qk = _conv_silu(raw, w9[l], b_conv[l][None, :].astype(F32), qk_scale, ctx_len)
        haf, hab = _mlstm(qk, mix, g, ctx_len, d_a)
        hbf, hbb = _hgrn(mix, kdec, lf, ctx_len, d_b)
        res = _merge(xall, haf, hab, hbf, hbb, sig, silu, mod[l], g_head_a[l][None, :].astype(F32),
                     g_head_b[l][None, :].astype(F32), w_a[l].astype(BF16), w_b[l].astype(BF16),
                     w_out[l].astype(BF16), ctx_len, nxt=norm_args(l + 1) if l + 1 < depth else None)
        xall, h16, g = res if l + 1 < depth else (res, None, None)
    return _final_norm(xall, g_final[None, :].astype(F32), ctx_len).astype(x.dtype)
```

```python
import functools
import math

import numpy as np
import jax
import jax.numpy as jnp
from jax import lax
from jax.experimental import pallas as pl
from jax.experimental.pallas import tpu as pltpu

F32 = jnp.float32
BF16 = jnp.bfloat16

H_A = 4
H_B = 8
GRID_W = 64
EPS = 1e-6
NEG = -1e30
LOG2E = math.log2(math.e)
N_GATES = 4 * H_A
LANES = 128
L_A = 256
L_B = 128
HGRN_COMPACT_MIN = 16
HGRN_LOCAL = 32
HGRN_LOCAL_MAX_LOG2 = 100.0
CONV_ROWS = 256
CONV_HALO = 128
HGRN_CUMSUM_TERMS = 2
CONV_COLS = 2048
PROJ_ROWS = 256
V7X_VMEM_LIMIT = 56 * 1024 * 1024
N_MIX, N_SIG, N_RAW, N_SILU, N_DECAY = 3, 3, 2, 2, 2
MIX_V, MIX_QB, MIX_IB = 0, 1, 2
SIG_O, SIG_MA, SIG_MB = 0, 1, 2
SILU_ZA, SILU_ZB = 0, 1


def _cparams(semantics):
    return pltpu.CompilerParams(dimension_semantics=semantics, vmem_limit_bytes=V7X_VMEM_LIMIT)


def _sigmoid(x):
    return 1.0 / (1.0 + jnp.exp(-x))


def _split3(x):
    hi = x.astype(BF16)
    r1 = x - hi.astype(F32)
    mid = r1.astype(BF16)
    lo = (r1 - mid.astype(F32)).astype(BF16)
    return hi, mid, lo


def _cumsum_mm(tri, x, terms=3):
    out = None
    for part in _split3(x)[:terms]:
        term = jnp.dot(tri, part, preferred_element_type=F32)
        out = term if out is None else out + term
    return out


def _dot_nt(a, b):
    return lax.dot_general(a, b, (((1,), (1,)), ((), ())), preferred_element_type=F32)


def _dot_tn(a, b):
    return lax.dot_general(a, b, (((0,), (0,)), ((), ())), preferred_element_type=F32)


def _ada_kernel(cc_ref, w_ref, b_ref, out_ref):
    s = cc_ref[...]
    s = s * _sigmoid(s)
    w = w_ref[0]
    s_hi = s.astype(BF16)
    s_lo = (s - s_hi.astype(F32)).astype(BF16)
    w_hi = w.astype(BF16)
    w_lo = (w - w_hi.astype(F32)).astype(BF16)
    d = lambda a, b: jnp.dot(a, b, preferred_element_type=F32)
    out_ref[0] = d(s_hi, w_hi) + d(s_lo, w_hi) + d(s_hi, w_lo) + b_ref[0]


def _ada(cc, w_ada, b_ada):
    depth, d, n3 = w_ada.shape
    tn = 512
    return pl.pallas_call(
        _ada_kernel,
        out_shape=jax.ShapeDtypeStruct((depth, 8, n3), F32),
        grid=(depth, n3 // tn),
        in_specs=[pl.BlockSpec((8, d), lambda l, j: (0, 0)),
                  pl.BlockSpec((1, d, tn), lambda l, j: (l, 0, j)),
                  pl.BlockSpec((1, 1, tn), lambda l, j: (l, 0, j))],
        out_specs=pl.BlockSpec((1, 8, tn), lambda l, j: (l, 0, j)),
        compiler_params=_cparams(("arbitrary", "arbitrary")),
        name="ada",
    )(cc, w_ada, b_ada.reshape(depth, 1, n3))


def _lbs_kernel(lg_ref, out_ref):
    x = lg_ref[...]
    depth = x.shape[0]
    m = jnp.max(x, axis=0, keepdims=True)
    e = jnp.exp(x - m)
    p = e / jnp.sum(e, axis=0, keepdims=True)
    acc = jnp.zeros_like(p[0:1])
    for l in range(depth):
        acc = acc + p[l:l + 1]
        out_ref[l:l + 1, :] = acc - p[0:1]


def _lbs(lb_logits):
    return pl.pallas_call(
        _lbs_kernel, out_shape=jax.ShapeDtypeStruct(lb_logits.shape, F32), name="lbs",
    )(lb_logits.astype(F32))


def _norm_modulate(x, is_ctx, b, mod_ref, gn_ref, wg_ref, bg_ref, n_batch):
    d = x.shape[-1]
    ms = jnp.mean(x * x, axis=-1, keepdims=True)
    y = x * lax.rsqrt(ms + EPS) * gn_ref[...]
    mb = mod_ref[pl.ds(b, 1), :]
    mc = mod_ref[pl.ds(n_batch, 1), :]
    shift = jnp.where(is_ctx, mc[:, 0:d], mb[:, 0:d])
    scale = jnp.where(is_ctx, mc[:, d:2 * d], mb[:, d:2 * d])
    h = y * (1.0 + scale) + shift
    hi = h.astype(BF16)
    lo = (h - hi.astype(F32)).astype(BF16)
    dd = lambda a, bb: jnp.dot(a, bb, preferred_element_type=F32)
    return hi, dd(hi, wg_ref[0]) + dd(lo, wg_ref[0]) + dd(hi, wg_ref[1]) + bg_ref[...]


def _norm_kernel(x_ref, mod_ref, gn_ref, wg_ref, bg_ref, h_ref, g_ref, *, ctx_len, tm, n_batch):
    b = pl.program_id(0)
    row = pl.program_id(1) * tm + lax.broadcasted_iota(jnp.int32, (tm, 1), 0)
    h_ref[0], g_ref[0] = _norm_modulate(x_ref[0], row < ctx_len, b, mod_ref, gn_ref, wg_ref, bg_ref, n_batch)


def _pick_tile(n, candidates):
    for c in candidates:
        if n % c == 0:
            return c
    raise ValueError(f"no tile for {n}")


def _norm_gates(xall, mod_l, gn, wg, bg, ctx_len):
    nb, tb, d = xall.shape
    tm = _pick_tile(tb, (640, 256))
    kern = functools.partial(_norm_kernel, ctx_len=ctx_len, tm=tm, n_batch=nb)
    return pl.pallas_call(
        kern,
        out_shape=(jax.ShapeDtypeStruct((nb, tb, d), BF16),
                   jax.ShapeDtypeStruct((nb, tb, LANES), F32)),
        grid=(nb, tb // tm),
        in_specs=[pl.BlockSpec((1, tm, d), lambda b, i: (b, i, 0)),
                  pl.BlockSpec((8, 3 * d), lambda b, i: (0, 0)),
                  pl.BlockSpec((1, d), lambda b, i: (0, 0)),
                  pl.BlockSpec((2, d, LANES), lambda b, i: (0, 0, 0)),
                  pl.BlockSpec((1, LANES), lambda b, i: (0, 0))],
        out_specs=(pl.BlockSpec((1, tm, d), lambda b, i: (b, i, 0)),
                   pl.BlockSpec((1, tm, LANES), lambda b, i: (b, i, 0))),
        compiler_params=_cparams(("arbitrary", "arbitrary")),
        name="norm_gates",
    )(xall, mod_l, gn, wg, bg)


def _proj_kernel(*refs, kind, rs):
    if kind == "decay":
        h_ref, w_ref, b_ref, lb_ref, lf_ref, k_ref = refs
    else:
        h_ref, w_ref, b_ref, out_ref = refs
    tm = h_ref.shape[1]
    for r0 in range(0, tm, rs):
        rows = slice(r0, r0 + rs)
        acc = jnp.dot(h_ref[0, rows, :], w_ref[...], preferred_element_type=F32) + b_ref[...]
        if kind == "raw":
            out_ref[0, rows, :] = acc
        elif kind == "cast":
            out_ref[0, rows, :] = acc.astype(BF16)
        elif kind == "sigmoid":
            out_ref[0, rows, :] = _sigmoid(acc).astype(BF16)
        elif kind == "silu":
            out_ref[0, rows, :] = (acc * _sigmoid(acc)).astype(BF16)
        else:
            lbv = lb_ref[...]
            t = jnp.exp2(jnp.abs(acc) * (-LOG2E))
            pos = acc >= 0.0
            rc = 1.0 / (1.0 + t)
            f = jnp.where(pos, 1.0 + lbv * t, t + lbv) * rc
            lf_ref[0, rows, :] = jnp.log(f) * LOG2E
            k_ref[0, rows, :] = ((1.0 - lbv) * jnp.where(pos, t, 1.0) * rc).astype(BF16)


def _proj(h16, w_main, b_main, col0, n_groups, kind, lb=None):
    nb, tb, d = h16.shape
    tm = _pick_tile(tb, (1280, 640, 256))
    tn = n_groups * d
    off = col0 // n_groups
    assert col0 % n_groups == 0
    kern = functools.partial(_proj_kernel, kind=kind, rs=PROJ_ROWS)
    in_specs = [pl.BlockSpec((1, tm, d), lambda b, i: (b, i, 0)),
                pl.BlockSpec((d, tn), lambda b, i: (0, off)),
                pl.BlockSpec((1, tn), lambda b, i: (0, off))]
    args = [h16, w_main, b_main]
    oblk = pl.BlockSpec((1, tm, tn), lambda b, i: (b, i, 0))
    shp = lambda dt: jax.ShapeDtypeStruct((nb, tb, tn), dt)
    if kind == "decay":
        in_specs.append(pl.BlockSpec((1, tn), lambda b, i: (0, 0)))
        args.append(jnp.concatenate([lb] * n_groups, axis=-1))
        out_shape, out_specs = (shp(F32), shp(BF16)), (oblk, oblk)
    else:
        out_shape, out_specs = shp(F32 if kind == "raw" else BF16), oblk
    return pl.pallas_call(
        kern,
        out_shape=out_shape,
        grid=(nb, tb // tm),
        in_specs=in_specs,
        out_specs=out_specs,
        compiler_params=_cparams(("arbitrary", "arbitrary")),
        name="proj_" + kind,
    )(*args)


def _conv_kernel(prev_ref, cur_ref, next_ref, w_ref, bc_ref, sc_ref, out_ref, *, n_tiles, tr, halo):
    i = pl.program_id(1)
    is_ctx = i == 0
    prev_ok = i >= 2
    next_ok = jnp.logical_and(i >= 1, i < n_tiles - 1)
    margin = GRID_W + 8
    full = jnp.concatenate([jnp.where(prev_ok, prev_ref[0, halo - margin:halo, :], 0.0), cur_ref[0],
                            jnp.where(next_ok, next_ref[0, 0:margin, :], 0.0)], axis=0)
    w = w_ref[...]
    sums = []
    for dc in (-1, 0, 1):
        acc = None
        for dr in (-1, 0, 1):
            tap = (dr + 1) * 3 + (dc + 1)
            wt = w[tap:tap + 1, :]
            if dr != 0:
                wt = jnp.where(is_ctx, 0.0, wt)
            start = margin + GRID_W * dr - 8
            term = wt * full[start:start + tr + 16, :]
            acc = term if acc is None else acc + term
        sums.append(acc)
    col = lax.broadcasted_iota(jnp.int32, (tr, 1), 0) % GRID_W
    rows = tr + 16
    left = pltpu.roll(sums[0], 1, axis=0)[8:8 + tr, :]
    right = pltpu.roll(sums[2], rows - 1, axis=0)[8:8 + tr, :]
    y = (sums[1][8:8 + tr, :] + jnp.where(jnp.logical_or(is_ctx, col != 0), left, 0.0)
         + jnp.where(jnp.logical_or(is_ctx, col != GRID_W - 1), right, 0.0) + bc_ref[...])
    out_ref[0] = (y * _sigmoid(y) * sc_ref[...]).astype(out_ref.dtype)


def _conv_silu(raw, w9, bconv, scale, ctx_len):
    nb, tb, n_qk = raw.shape
    tr, halo, tc = CONV_ROWS, CONV_HALO, CONV_COLS
    assert ctx_len == tr and tb % tr == 0 and tr % GRID_W == 0 and halo % GRID_W == 0
    n_tiles = tb // tr
    per = tr // halo
    n_halo = tb // halo
    kern = functools.partial(_conv_kernel, n_tiles=n_tiles, tr=tr, halo=halo)
    return pl.pallas_call(
        kern,
        out_shape=jax.ShapeDtypeStruct((nb, tb, n_qk), BF16),
        grid=(nb, n_tiles, n_qk // tc),
        in_specs=[pl.BlockSpec((1, halo, tc), lambda b, i, j: (b, jnp.maximum(i * per - 1, 0), j)),
                  pl.BlockSpec((1, tr, tc), lambda b, i, j: (b, i, j)),
                  pl.BlockSpec((1, halo, tc), lambda b, i, j: (b, jnp.minimum((i + 1) * per, n_halo - 1), j)),
                  pl.BlockSpec((9, tc), lambda b, i, j: (0, j)),
                  pl.BlockSpec((1, tc), lambda b, i, j: (0, j)),
                  pl.BlockSpec((1, tc), lambda b, i, j: (0, j))],
        out_specs=pl.BlockSpec((1, tr, tc), lambda b, i, j: (b, i, j)),
        compiler_params=_cparams(("arbitrary", "arbitrary", "arbitrary")),
        name="conv_silu",
    )(raw, raw, raw, w9, bconv, scale)


def _mlstm_kernel(qkf_ref, vf_ref, gf_ref, qkb_ref, vb_ref, gb_ref, hf_ref, hb_ref,
                  c_scr, n_scr, m_scr, *, la, dh):
    @pl.when(pl.program_id(1) == 0)
    def _():
        c_scr[...] = jnp.zeros_like(c_scr)
        n_scr[...] = jnp.zeros_like(n_scr)
        m_scr[...] = jnp.zeros_like(m_scr)

    r = lax.broadcasted_iota(jnp.int32, (la, la), 0)
    s = lax.broadcasted_iota(jnp.int32, (la, la), 1)
    ones_blk = jnp.ones((la, LANES), BF16)
    wide = lambda x, n: jnp.concatenate([x] * (n // LANES), axis=1)
    dirs = ((qkf_ref, vf_ref, gf_ref, hf_ref), (qkb_ref, vb_ref, gb_ref, hb_ref))
    for d, (qk_ref, v_ref, g_ref, h_ref) in enumerate(dirs):
        seen = (s <= r) if d == 0 else (s >= r)
        tri = jnp.where(seen, 1.0, 0.0).astype(BF16)
        g = g_ref[0]
        lsig = jnp.minimum(g, 0.0) - jnp.log(1.0 + jnp.exp(-jnp.abs(g)))
        bc = _cumsum_mm(tri, lsig)
        b2 = pltpu.roll(bc, LANES - H_A, axis=1) * LOG2E
        r2 = g * LOG2E - b2
        r2_t = r2.T
        last = la - 1 if d == 0 else 0
        b2_l = b2[last:last + 1, :]
        m2 = m_scr[d:d + 1, :]
        ws2 = r2 + b2_l
        m2_new = jnp.maximum(b2_l + m2, jnp.max(ws2, axis=0, keepdims=True))
        decay = jnp.exp2(b2_l + m2 - m2_new)
        wse = jnp.exp2(ws2 - m2_new)
        m_scr[d:d + 1, :] = m2_new
        cols = [2 * d * H_A + h for h in range(H_A)]
        q = [qk_ref[0, :, h * dh:(h + 1) * dh] for h in range(H_A)]
        k = [qk_ref[0, :, (H_A + h) * dh:(H_A + h + 1) * dh] for h in range(H_A)]
        v = [v_ref[0, :, h * dh:(h + 1) * dh] for h in range(H_A)]
        col = lambda x, ci: jnp.broadcast_to(x[:, ci:ci + 1], (la, LANES))
        rm = [jnp.where(seen, r2_t[ci:ci + 1, :], NEG) for ci in cols]
        mx = [jnp.broadcast_to(jnp.maximum(m2[:, ci:ci + 1], jnp.max(rm_h, axis=-1, keepdims=True)),
                               (la, LANES)) for ci, rm_h in zip(cols, rm)]
        s_mat = [_dot_nt(q[h], k[h]) for h in range(H_A)]
        for h, ci in enumerate(cols):
            u = d * H_A + h
            w16 = (s_mat[h] * jnp.exp2(rm[h] - wide(mx[h], la))).astype(BF16)
            a_in = jnp.exp2(m2[:, ci:ci + 1] - mx[h])
            em = jnp.exp2(-(col(b2, ci) + mx[h]))
            c_st = c_scr[u]
            n_st = n_scr[u]
            qa = q[h] * wide(a_in.astype(BF16), dh)
            num = jnp.dot(jnp.concatenate([w16, qa], axis=1),
                          jnp.concatenate([v[h], c_st.astype(BF16)], axis=0), preferred_element_type=F32)
            qn = _dot_nt(q[h], jnp.broadcast_to(n_st, (LANES, dh)).astype(BF16))
            den = a_in * qn + jnp.dot(w16, ones_blk, preferred_element_type=F32)
            rcp = 1.0 / jnp.maximum(jnp.abs(den), em)
            h_ref[0, :, h * dh:(h + 1) * dh] = (num * wide(rcp, dh)).astype(h_ref.dtype)
            wse_u = col(wse, ci).astype(BF16)
            dec = decay[:, ci:ci + 1]
            c_scr[u] = dec * c_st + _dot_tn(k[h] * wide(wse_u, dh), v[h])
            n_scr[u] = dec * n_st + _dot_tn(wse_u, k[h])[0:1, :]


def _bwd_chunk(c, n_ctx_chunks, n_chunks):
    return jnp.where(c < n_ctx_chunks, n_ctx_chunks - 1 - c, n_chunks - 1 + n_ctx_chunks - c)


def _mlstm(qk, mix, g, ctx_len, d_a):
    nb, tb, _ = qk.shape
    la = L_A
    dh = d_a // H_A
    assert ctx_len % la == 0 and tb % la == 0
    nc, ncc = tb // la, ctx_len // la
    bw = functools.partial(_bwd_chunk, n_ctx_chunks=ncc, n_chunks=nc)
    kern = functools.partial(_mlstm_kernel, la=la, dh=dh)
    hshape = jax.ShapeDtypeStruct((nb, tb, d_a), BF16)
    return pl.pallas_call(
        kern,
        out_shape=(hshape, hshape),
        grid=(nb, nc),
        in_specs=[pl.BlockSpec((1, la, 2 * d_a), lambda b, c: (b, c, 0)),
                  pl.BlockSpec((1, la, d_a), lambda b, c: (b, c, MIX_V)),
                  pl.BlockSpec((1, la, LANES), lambda b, c: (b, c, 0)),
                  pl.BlockSpec((1, la, 2 * d_a), lambda b, c: (b, bw(c), 0)),
                  pl.BlockSpec((1, la, d_a), lambda b, c: (b, bw(c), MIX_V)),
                  pl.BlockSpec((1, la, LANES), lambda b, c: (b, bw(c), 0))],
        out_specs=(pl.BlockSpec((1, la, d_a), lambda b, c: (b, c, 0)),
                   pl.BlockSpec((1, la, d_a), lambda b, c: (b, bw(c), 0))),
        scratch_shapes=[pltpu.VMEM((2 * H_A, dh, dh), F32),
                        pltpu.VMEM((2 * H_A, 1, dh), F32),
                        pltpu.VMEM((8, LANES), F32)],
        compiler_params=_cparams(("arbitrary", "arbitrary")),
        name="mlstm",
    )(qk, mix, g, qk, mix, g)


def _hgrn_levels(l):
    return [1 << j for j in range(int(np.log2(l)))]


def _hgrn_q_rows(l, h, d):
    off = h if d == 0 else 0
    return [(k * 2 * h + off, k * 2 * h + off + h) for k in range(l // (2 * h))]


def _hgrn_masks(l):
    r = np.arange(l)[:, None]
    s = np.arange(l)[None, :]
    full, half = [[], []], [[], []]
    for d in (0, 1):
        full[d].append(r == s)
        for h in _hgrn_levels(l):
            same = (r // (2 * h)) == (s // (2 * h))
            m = same & ((r % (2 * h)) >= h) & ((s % (2 * h)) < h)
            m = m if d == 0 else m.T
            if h >= HGRN_COMPACT_MIN:
                half[d].append(np.concatenate([m[a:b] for a, b in _hgrn_q_rows(l, h, d)]))
            else:
                full[d].append(m)
    local = [(r // HGRN_LOCAL == s // HGRN_LOCAL) & (s <= r), (r // HGRN_LOCAL == s // HGRN_LOCAL) & (s >= r)]
    return np.array(full, np.float32), np.array(half, np.float32), np.array(local, np.float32)


def _hgrn_tri(l):
    r = np.arange(l)[:, None]
    s = np.arange(l)[None, :]
    return np.stack([(s <= r), (s >= r)]).astype(np.float32)


def _hgrn_split_rows(g_scr, l, h, d):
    assert h in (2, 4)
    off = h - 1 if d == 0 else h
    n = g_scr.shape[-1]
    sub = lax.broadcasted_iota(jnp.int32, (8, n), 0)
    pieces = []
    for grp in range(l // 8):
        lo = jnp.broadcast_to(g_scr[grp * 8 + off:grp * 8 + off + 1, :], (8, n))
        if h == 4:
            pieces.append(lo)
        else:
            hi = jnp.broadcast_to(g_scr[grp * 8 + 4 + off:grp * 8 + 4 + off + 1, :], (8, n))
            pieces.append(jnp.where(sub < 4, lo, hi))
    return jnp.concatenate(pieces, axis=0)


def _hgrn_block_decay(g, l):
    spans = [jnp.abs(g[b0:b0 + 1, :] - g[b0 + HGRN_LOCAL - 1:b0 + HGRN_LOCAL, :]) for b0 in range(0, l, HGRN_LOCAL)]
    return jnp.max(jnp.concatenate(spans, axis=0))


def _hgrn_chunk(dirs, mfull_ref, mhalf_ref, mlocal_ref, st_scr, g_scr, *, l, dh, fast):
    heads = [slice(h * dh, (h + 1) * dh) for h in range(H_B)]
    q = [dirs[d][0][0] for d in (0, 1)]
    kk = [dirs[d][2][0] for d in (0, 1)]

    def scores(qs, ks, mask):
        return [_dot_nt(qs[:, hs], ks[:, hs]).astype(BF16) * mask for hs in heads]

    if fast:
        acc = []
        for d in (0, 1):
            g = g_scr.at[d]
            parts = []
            for b0 in range(0, l, HGRN_LOCAL):
                base = b0 if d == 0 else b0 + HGRN_LOCAL - 1
                parts.append(g[b0:b0 + HGRN_LOCAL, :] - g[base:base + 1, :])
            x = jnp.concatenate(parts, axis=0)
            acc.append(scores(q[d] * jnp.exp2(x).astype(BF16), kk[d] * jnp.exp2(-x).astype(BF16), mlocal_ref[d]))
        levels = [hl for hl in _hgrn_levels(l) if hl >= HGRN_LOCAL]
    else:
        acc = [scores(q[d], kk[d], mfull_ref[d, 0]) for d in (0, 1)]
        levels = _hgrn_levels(l)
    all_levels = _hgrn_levels(l)
    for hl in levels:
        n_full = 1 + all_levels.index(hl)
        n_half = all_levels.index(hl) - all_levels.index(HGRN_COMPACT_MIN)
        for d in (0, 1):
            g = g_scr.at[d]
            m_off = hl - 1 if d == 0 else hl
            q_rows = _hgrn_q_rows(l, hl, d)
            k_rows = _hgrn_q_rows(l, hl, 1 - d)
            if hl < HGRN_COMPACT_MIN:
                if hl == 1:
                    qs, ks = q[d] * jnp.exp2(dirs[d][3][0]).astype(BF16), kk[d]
                else:
                    if hl < 8:
                        x = -jnp.abs(g[...] - _hgrn_split_rows(g, l, hl, d))
                    else:
                        parts = {}
                        for (q0, q1), (k0, k1) in zip(q_rows, k_rows):
                            gm = g[min(q0, k0) + m_off:min(q0, k0) + m_off + 1, :]
                            parts[q0] = g[q0:q1, :] - gm
                            parts[k0] = gm - g[k0:k1, :]
                        x = jnp.concatenate([parts[r0] for r0 in sorted(parts)], axis=0)
                    e = jnp.exp2(x).astype(BF16)
                    qs, ks = q[d] * e, kk[d] * e
                acc[d] = [a + t for a, t in zip(acc[d], scores(qs, ks, mfull_ref[d, n_full]))]
            else:
                xq, xk = [], []
                for (q0, q1), (k0, k1) in zip(q_rows, k_rows):
                    gm = g[min(q0, k0) + m_off:min(q0, k0) + m_off + 1, :]
                    xq.append(g[q0:q1, :] - gm)
                    xk.append(gm - g[k0:k1, :])
                eq = jnp.exp2(jnp.concatenate(xq, axis=0)).astype(BF16)
                ek = jnp.exp2(jnp.concatenate(xk, axis=0)).astype(BF16)
                qc = jnp.concatenate([q[d][q0:q1] for q0, q1 in q_rows], axis=0) * eq
                kparts = {q0: kk[d][q0:q1] for q0, q1 in q_rows}
                for n, (k0, k1) in enumerate(k_rows):
                    kparts[k0] = kk[d][k0:k1] * ek[n * hl:(n + 1) * hl]
                kt = jnp.concatenate([kparts[r0] for r0 in sorted(kparts)], axis=0)
                terms = scores(qc, kt, mhalf_ref[d, n_half])
                for i in range(H_B):
                    pieces = {k0: acc[d][i][k0:k1] for k0, k1 in k_rows}
                    for n, (q0, q1) in enumerate(q_rows):
                        pieces[q0] = acc[d][i][q0:q1] + terms[i][n * hl:(n + 1) * hl]
                    acc[d][i] = jnp.concatenate([pieces[r0] for r0 in sorted(pieces)], axis=0)
    for d in (0, 1):
        o_ref = dirs[d][4]
        v = dirs[d][1][0]
        last = l - 1 if d == 0 else 0
        g_all = g_scr[d]
        g_l = g_scr[d, last:last + 1, :]
        qi = q[d] * jnp.exp2(g_all).astype(BF16)
        ks = kk[d] * jnp.exp2(g_l - g_all).astype(BF16)
        dec = jnp.exp2(g_l)
        for i, hs in enumerate(heads):
            u = d * H_B + i
            st = st_scr[u]
            o = _dot_nt(qi[:, hs], st.astype(BF16)) + jnp.dot(acc[d][i], v[:, hs], preferred_element_type=F32)
            o_ref[0, :, hs] = o.astype(o_ref.dtype)
            st_scr[u] = st * dec[:, hs] + _dot_tn(v[:, hs], ks[:, hs])


def _hgrn_kernel(qf_ref, vf_ref, kf_ref, lff_ref, qb_ref, vb_ref, kb_ref, lfb_ref,
                 tri_ref, mfull_ref, mhalf_ref, mlocal_ref, of_ref, ob_ref, st_scr, g_scr, *, l, dh):
    @pl.when(pl.program_id(1) == 0)
    def _():
        st_scr[...] = jnp.zeros_like(st_scr)

    dirs = ((qf_ref, vf_ref, kf_ref, lff_ref, of_ref), (qb_ref, vb_ref, kb_ref, lfb_ref, ob_ref))
    for d in (0, 1):
        g_scr[d] = _cumsum_mm(tri_ref[d], dirs[d][3][0], terms=HGRN_CUMSUM_TERMS)
    in_range = jnp.maximum(_hgrn_block_decay(g_scr.at[0], l),
                           _hgrn_block_decay(g_scr.at[1], l)) <= HGRN_LOCAL_MAX_LOG2
    run = functools.partial(_hgrn_chunk, dirs, mfull_ref, mhalf_ref, mlocal_ref, st_scr, g_scr, l=l, dh=dh)
    pl.when(in_range)(functools.partial(run, fast=True))
    pl.when(jnp.logical_not(in_range))(functools.partial(run, fast=False))


def _hgrn(mix, kdec, lf, ctx_len, d_b):
    nb, tb, _ = mix.shape
    l = L_B
    dh = d_b // H_B
    assert dh == LANES and ctx_len % l == 0 and tb % l == 0
    nc, ncc = tb // l, ctx_len // l
    bw = functools.partial(_bwd_chunk, n_ctx_chunks=ncc, n_chunks=nc)
    kern = functools.partial(_hgrn_kernel, l=l, dh=dh)
    tri = jnp.asarray(_hgrn_tri(l), BF16)
    mfull, mhalf, mlocal = (jnp.asarray(m, BF16) for m in _hgrn_masks(l))
    oshape = jax.ShapeDtypeStruct((nb, tb, d_b), BF16)
    blk = lambda col, chunk: pl.BlockSpec((1, l, d_b), lambda b, c: (b, chunk(c), col))
    ident = lambda c: c
    const = lambda a: pl.BlockSpec(a.shape, lambda b, c: (0,) * a.ndim)
    return pl.pallas_call(
        kern,
        out_shape=(oshape, oshape),
        grid=(nb, nc),
        in_specs=[blk(MIX_QB, ident), blk(MIX_IB, ident), blk(0, ident), blk(0, ident),
                  blk(MIX_QB, bw), blk(MIX_IB, bw), blk(1, bw), blk(1, bw),
                  const(tri), const(mfull), const(mhalf), const(mlocal)],
        out_specs=(pl.BlockSpec((1, l, d_b), lambda b, c: (b, c, 0)),
                   pl.BlockSpec((1, l, d_b), lambda b, c: (b, bw(c), 0))),
        scratch_shapes=[pltpu.VMEM((2 * H_B, dh, dh), F32),
                        pltpu.VMEM((2, l, d_b), F32)],
        compiler_params=_cparams(("arbitrary", "arbitrary")),
        name="hgrn",
    )(mix, mix, kdec, lf, mix, mix, kdec, lf, tri, mfull, mhalf, mlocal)


def _head_rms(x, n_heads):
    dh = x.shape[-1] // n_heads
    outs = []
    for h in range(n_heads):
        xh = x[:, h * dh:(h + 1) * dh]
        outs.append(xh * lax.rsqrt(jnp.mean(xh * xh, axis=-1, keepdims=True) + EPS))
    return jnp.concatenate(outs, axis=-1)


def _merge_kernel(*refs, ctx_len, tm, n_batch, fuse_next):
    (x_ref, haf_ref, hab_ref, hbf_ref, hbb_ref, o_ref, za_ref, zb_ref, ma_ref, mb_ref,
     mod_ref, gha_ref, ghb_ref, wa_ref, wb_ref, wo_ref) = refs[:16]
    b = pl.program_id(0)
    d = x_ref.shape[-1]
    f32 = lambda ref: ref[0].astype(F32)
    h_a = (f32(haf_ref) + f32(hab_ref)) * f32(o_ref)
    y_a = _head_rms(h_a, H_A) * gha_ref[...] * f32(za_ref)
    y_b = _head_rms(f32(hbf_ref) + f32(hbb_ref), H_B) * ghb_ref[...] * f32(zb_ref)
    pa = jnp.dot(y_a.astype(BF16), wa_ref[...], preferred_element_type=F32)
    pb = jnp.dot(y_b.astype(BF16), wb_ref[...], preferred_element_type=F32)
    y = f32(ma_ref) * pa + f32(mb_ref) * pb
    br = jnp.dot(y.astype(BF16), wo_ref[...], preferred_element_type=F32)
    is_ctx = pl.program_id(1) * tm + lax.broadcasted_iota(jnp.int32, (tm, 1), 0) < ctx_len
    gate = jnp.where(is_ctx, mod_ref[pl.ds(n_batch, 1), 2 * d:3 * d], mod_ref[pl.ds(b, 1), 2 * d:3 * d])
    x_new = x_ref[0] + gate * br
    if fuse_next:
        nmod_ref, ngn_ref, nwg_ref, nbg_ref, out_ref, h_ref, g_ref = refs[16:]
        h_ref[0], g_ref[0] = _norm_modulate(x_new, is_ctx, b, nmod_ref, ngn_ref, nwg_ref, nbg_ref, n_batch)
    else:
        out_ref, = refs[16:]
    out_ref[0] = x_new


def _merge(xall, haf, hab, hbf, hbb, sig, silu, mod_l, gha, ghb, wa, wb, wo, ctx_len, nxt=None):
    nb, tb, d = xall.shape
    tm = 256
    kern = functools.partial(_merge_kernel, ctx_len=ctx_len, tm=tm, n_batch=nb, fuse_next=nxt is not None)
    row = pl.BlockSpec((1, tm, d), lambda b, i: (b, i, 0))
    pcol = lambda col: pl.BlockSpec((1, tm, d), lambda b, i: (b, i, col))
    full = lambda shape: pl.BlockSpec(shape, lambda b, i: (0,) * len(shape))
    in_specs = [row, row, row, row, row,
                pcol(SIG_O), pcol(SILU_ZA), pcol(SILU_ZB), pcol(SIG_MA), pcol(SIG_MB),
                full((8, 3 * d)), full((1, d)), full((1, d)),
                full((d, d)), full((d, d)), full((d, d))]
    args = [xall, haf, hab, hbf, hbb, sig, silu, silu, sig, sig, mod_l, gha, ghb, wa, wb, wo]
    out_shape = jax.ShapeDtypeStruct(xall.shape, F32)
    out_specs = row
    if nxt is not None:
        in_specs += [full((8, 3 * d)), full((1, d)), full((2, d, LANES)), full((1, LANES))]
        args += list(nxt)
        out_shape = (out_shape, jax.ShapeDtypeStruct((nb, tb, d), BF16),
                     jax.ShapeDtypeStruct((nb, tb, LANES), F32))
        out_specs = (row, row, pl.BlockSpec((1, tm, LANES), lambda b, i: (b, i, 0)))
    return pl.pallas_call(
        kern,
        out_shape=out_shape,
        grid=(nb, tb // tm),
        in_specs=in_specs,
        out_specs=out_specs,
        compiler_params=_cparams(("arbitrary", "arbitrary")),
        name="merge",
    )(*args)


def _final_kernel(x_ref, g_ref, out_ref):
    x = x_ref[0]
    out_ref[0] = x * lax.rsqrt(jnp.mean(x * x, axis=-1, keepdims=True) + EPS) * g_ref[...]


def _final_norm(xall, g, ctx_len):
    nb, tb, d = xall.shape
    tm = 256
    off = ctx_len // tm
    return pl.pallas_call(
        _final_kernel,
        out_shape=jax.ShapeDtypeStruct((nb, tb - ctx_len, d), F32),
        grid=(nb, (tb - ctx_len) // tm),
        in_specs=[pl.BlockSpec((1, tm, d), lambda b, i: (b, i + off, 0)),
                  pl.BlockSpec((1, d), lambda b, i: (0, 0))],
        out_specs=pl.BlockSpec((1, tm, d), lambda b, i: (b, i, 0)),
        compiler_params=_cparams(("arbitrary", "arbitrary")),
        name="final_norm",
    )(xall, g)


def kernel(x, c, ctx, c_ctx, w_ada, b_ada, g_norm, w_in, b_in, w_conv, b_conv, lb_logits,
           g_head_a, g_head_b, w_a, w_b, w_out, g_final):
    nb, seq, d = x.shape
    ctx_len = ctx.shape[1]
    depth = w_ada.shape[0]
    d_a = g_head_a.shape[-1]
    d_b = g_head_b.shape[-1]
    assert d_a == d and d_b == d and nb + 1 <= 8
    dh_a = d_a // H_A

    g0 = 5 * d
    g1 = g0 + N_GATES
    grp = lambda n: slice(n * d, (n + 1) * d) if n < 5 else slice(g1 + (n - 5) * d, g1 + (n - 4) * d)
    QK0, QK1, V, O, ZA, QB, IB, FF, FB, ZB, MA, MB = range(12)
    order = (V, QB, IB, O, MA, MB, QK0, QK1, ZA, ZB, FF, FB)
    w_main = jnp.concatenate([w_in[:, :, grp(n)] for n in order], axis=-1).astype(BF16)
    b_main = jnp.concatenate([b_in[:, grp(n)] for n in order], axis=-1).astype(F32)
    wg = jnp.pad(w_in[:, :, g0:g1].astype(F32), ((0, 0), (0, 0), (0, LANES - N_GATES)))
    wg_hi = wg.astype(BF16)
    wg_lo = (wg - wg_hi.astype(F32)).astype(BF16)
    wg2 = jnp.stack([wg_hi, wg_lo], axis=1)
    bg = jnp.pad(b_in[:, g0:g1].astype(F32), ((0, 0), (0, LANES - N_GATES)))
    qk_scale = jnp.concatenate([jnp.ones((1, d_a), F32), jnp.full((1, d_a), dh_a ** -0.5, F32)], axis=-1)
    w9 = w_conv.reshape(depth, 9, 2 * d_a).astype(F32)

    cc = jnp.concatenate([c.astype(F32), c_ctx.astype(F32)[None, :],
                          jnp.zeros((8 - nb - 1, d), F32)], axis=0)
    mod = _ada(cc, w_ada.astype(F32), b_ada.astype(F32))
    lbs = _lbs(lb_logits)
    c_sig, c_raw, c_silu, c_dec = (int(n) for n in np.cumsum((N_MIX, N_SIG, N_RAW, N_SILU)))

    xall = jnp.concatenate([ctx.astype(F32), x.astype(F32)], axis=1)
    norm_args = lambda l: (mod[l], g_norm[l][None, :].astype(F32), wg2[l], bg[l][None, :])
    h16, g = _norm_gates(xall, *norm_args(0), ctx_len)
    for l in range(depth):
        bm = b_main[l][None, :]
        raw = _proj(h16, w_main[l], bm, c_raw, N_RAW, "raw")
        mix = _proj(h16, w_main[l], bm, 0, N_MIX, "cast")
        sig = _proj(h16, w_main[l], bm, c_sig, N_SIG, "sigmoid")
        silu = _proj(h16, w_main[l], bm, c_silu, N_SILU, "silu")
        lf, kdec = _proj(h16, w_main[l], bm, c_dec, N_DECAY, "decay", lbs[l][None, :])
        qk = _conv_silu(raw, w9[l], b_conv[l][None, :].astype(F32), qk_scale, ctx_len)
        haf, hab = _mlstm(qk, mix, g, ctx_len, d_a)
        hbf, hbb = _hgrn(mix, kdec, lf, ctx_len, d_b)
        res = _merge(xall, haf, hab, hbf, hbb, sig, silu, mod[l], g_head_a[l][None, :].astype(F32),
                     g_head_b[l][None, :].astype(F32), w_a[l].astype(BF16), w_b[l].astype(BF16),
                     w_out[l].astype(BF16), ctx_len, nxt=norm_args(l + 1) if l + 1 < depth else None)
        xall, h16, g = res if l + 1 < depth else (res, None, None)
    return _final_norm(xall, g_final[None, :].astype(F32), ctx_len).astype(x.dtype)
```

```python
import functools
import math

import numpy as np
import jax
import jax.numpy as jnp
from jax import lax
from jax.experimental import pallas as pl
from jax.experimental.pallas import tpu as pltpu

F32 = jnp.float32
BF16 = jnp.bfloat16

H_A = 4
H_B = 8
GRID_W = 64
EPS = 1e-6
NEG = -1e30
LOG2E = math.log2(math.e)
N_GATES = 4 * H_A
LANES = 128
L_A = 256
L_B = 128
HGRN_COMPACT_MIN = 16
HGRN_LOCAL = 32
HGRN_LOCAL_MAX_LOG2 = 100.0
HGRN_STEP_CHUNKS = 2
CONV_ROWS = 256
CONV_HALO = 128
HGRN_CUMSUM_TERMS = 2
CONV_COLS = 2048
PROJ_ROWS = 256
V7X_VMEM_LIMIT = 56 * 1024 * 1024
N_MIX, N_SIG, N_RAW, N_SILU, N_DECAY = 3, 3, 2, 2, 2
MIX_V, MIX_QB, MIX_IB = 0, 1, 2
SIG_O, SIG_MA, SIG_MB = 0, 1, 2
SILU_ZA, SILU_ZB = 0, 1


def _cparams(semantics):
    return pltpu.CompilerParams(dimension_semantics=semantics, vmem_limit_bytes=V7X_VMEM_LIMIT)


def _sigmoid(x):
    return 1.0 / (1.0 + jnp.exp(-x))


def _split3(x):
    hi = x.astype(BF16)
    r1 = x - hi.astype(F32)
    mid = r1.astype(BF16)
    lo = (r1 - mid.astype(F32)).astype(BF16)
    return hi, mid, lo


def _cumsum_mm(tri, x, terms=3):
    out = None
    for part in _split3(x)[:terms]:
        term = jnp.dot(tri, part, preferred_element_type=F32)
        out = term if out is None else out + term
    return out


def _dot_nt(a, b):
    return lax.dot_general(a, b, (((1,), (1,)), ((), ())), preferred_element_type=F32)


def _dot_tn(a, b):
    return lax.dot_general(a, b, (((0,), (0,)), ((), ())), preferred_element_type=F32)


def _ada_kernel(cc_ref, w_ref, b_ref, out_ref):
    s = cc_ref[...]
    s = s * _sigmoid(s)
    w = w_ref[0]
    s_hi = s.astype(BF16)
    s_lo = (s - s_hi.astype(F32)).astype(BF16)
    w_hi = w.astype(BF16)
    w_lo = (w - w_hi.astype(F32)).astype(BF16)
    d = lambda a, b: jnp.dot(a, b, preferred_element_type=F32)
    out_ref[0] = d(s_hi, w_hi) + d(s_lo, w_hi) + d(s_hi, w_lo) + b_ref[0]


def _ada(cc, w_ada, b_ada):
    depth, d, n3 = w_ada.shape
    tn = 512
    return pl.pallas_call(
        _ada_kernel,
        out_shape=jax.ShapeDtypeStruct((depth, 8, n3), F32),
        grid=(depth, n3 // tn),
        in_specs=[pl.BlockSpec((8, d), lambda l, j: (0, 0)),
                  pl.BlockSpec((1, d, tn), lambda l, j: (l, 0, j)),
                  pl.BlockSpec((1, 1, tn), lambda l, j: (l, 0, j))],
        out_specs=pl.BlockSpec((1, 8, tn), lambda l, j: (l, 0, j)),
        compiler_params=_cparams(("arbitrary", "arbitrary")),
        name="ada",
    )(cc, w_ada, b_ada.reshape(depth, 1, n3))


def _lbs_kernel(lg_ref, out_ref):
    x = lg_ref[...]
    depth = x.shape[0]
    m = jnp.max(x, axis=0, keepdims=True)
    e = jnp.exp(x - m)
    p = e / jnp.sum(e, axis=0, keepdims=True)
    acc = jnp.zeros_like(p[0:1])
    for l in range(depth):
        acc = acc + p[l:l + 1]
        out_ref[l:l + 1, :] = acc - p[0:1]


def _lbs(lb_logits):
    return pl.pallas_call(
        _lbs_kernel, out_shape=jax.ShapeDtypeStruct(lb_logits.shape, F32), name="lbs",
    )(lb_logits.astype(F32))


def _norm_modulate(x, is_ctx, b, mod_ref, gn_ref, wg_ref, bg_ref, n_batch):
    d = x.shape[-1]
    ms = jnp.mean(x * x, axis=-1, keepdims=True)
    y = x * lax.rsqrt(ms + EPS) * gn_ref[...]
    mb = mod_ref[pl.ds(b, 1), :]
    mc = mod_ref[pl.ds(n_batch, 1), :]
    shift = jnp.where(is_ctx, mc[:, 0:d], mb[:, 0:d])
    scale = jnp.where(is_ctx, mc[:, d:2 * d], mb[:, d:2 * d])
    h = y * (1.0 + scale) + shift
    hi = h.astype(BF16)
    lo = (h - hi.astype(F32)).astype(BF16)
    dd = lambda a, bb: jnp.dot(a, bb, preferred_element_type=F32)
    return hi, dd(hi, wg_ref[0]) + dd(lo, wg_ref[0]) + dd(hi, wg_ref[1]) + bg_ref[...]


def _norm_kernel(x_ref, mod_ref, gn_ref, wg_ref, bg_ref, h_ref, g_ref, *, ctx_len, tm, n_batch):
    b = pl.program_id(0)
    row = pl.program_id(1) * tm + lax.broadcasted_iota(jnp.int32, (tm, 1), 0)
    h_ref[0], g_ref[0] = _norm_modulate(x_ref[0], row < ctx_len, b, mod_ref, gn_ref, wg_ref, bg_ref, n_batch)


def _pick_tile(n, candidates):
    for c in candidates:
        if n % c == 0:
            return c
    raise ValueError(f"no tile for {n}")


def _norm_gates(xall, mod_l, gn, wg, bg, ctx_len):
    nb, tb, d = xall.shape
    tm = _pick_tile(tb, (640, 256))
    kern = functools.partial(_norm_kernel, ctx_len=ctx_len, tm=tm, n_batch=nb)
    return pl.pallas_call(
        kern,
        out_shape=(jax.ShapeDtypeStruct((nb, tb, d), BF16),
                   jax.ShapeDtypeStruct((nb, tb, LANES), F32)),
        grid=(nb, tb // tm),
        in_specs=[pl.BlockSpec((1, tm, d), lambda b, i: (b, i, 0)),
                  pl.BlockSpec((8, 3 * d), lambda b, i: (0, 0)),
                  pl.BlockSpec((1, d), lambda b, i: (0, 0)),
                  pl.BlockSpec((2, d, LANES), lambda b, i: (0, 0, 0)),
                  pl.BlockSpec((1, LANES), lambda b, i: (0, 0))],
        out_specs=(pl.BlockSpec((1, tm, d), lambda b, i: (b, i, 0)),
                   pl.BlockSpec((1, tm, LANES), lambda b, i: (b, i, 0))),
        compiler_params=_cparams(("arbitrary", "arbitrary")),
        name="norm_gates",
    )(xall, mod_l, gn, wg, bg)


def _proj_kernel(*refs, kind, rs):
    if kind == "decay":
        h_ref, w_ref, b_ref, lb_ref, lf_ref, k_ref = refs
    else:
        h_ref, w_ref, b_ref, out_ref = refs
    tm = h_ref.shape[1]
    for r0 in range(0, tm, rs):
        rows = slice(r0, r0 + rs)
        acc = jnp.dot(h_ref[0, rows, :], w_ref[...], preferred_element_type=F32) + b_ref[...]
        if kind == "raw":
            out_ref[0, rows, :] = acc
        elif kind == "cast":
            out_ref[0, rows, :] = acc.astype(BF16)
        elif kind == "sigmoid":
            out_ref[0, rows, :] = _sigmoid(acc).astype(BF16)
        elif kind == "silu":
            out_ref[0, rows, :] = (acc * _sigmoid(acc)).astype(BF16)
        else:
            lbv = lb_ref[...]
            t = jnp.exp2(jnp.abs(acc) * (-LOG2E))
            pos = acc >= 0.0
            rc = 1.0 / (1.0 + t)
            f = jnp.where(pos, 1.0 + lbv * t, t + lbv) * rc
            lf_ref[0, rows, :] = jnp.log(f) * LOG2E
            k_ref[0, rows, :] = ((1.0 - lbv) * jnp.where(pos, t, 1.0) * rc).astype(BF16)


def _proj(h16, w_main, b_main, col0, n_groups, kind, lb=None):
    nb, tb, d = h16.shape
    tm = _pick_tile(tb, (1280, 640, 256))
    tn = n_groups * d
    off = col0 // n_groups
    assert col0 % n_groups == 0
    kern = functools.partial(_proj_kernel, kind=kind, rs=PROJ_ROWS)
    in_specs = [pl.BlockSpec((1, tm, d), lambda b, i: (b, i, 0)),
                pl.BlockSpec((d, tn), lambda b, i: (0, off)),
                pl.BlockSpec((1, tn), lambda b, i: (0, off))]
    args = [h16, w_main, b_main]
    oblk = pl.BlockSpec((1, tm, tn), lambda b, i: (b, i, 0))
    shp = lambda dt: jax.ShapeDtypeStruct((nb, tb, tn), dt)
    if kind == "decay":
        in_specs.append(pl.BlockSpec((1, tn), lambda b, i: (0, 0)))
        args.append(jnp.concatenate([lb] * n_groups, axis=-1))
        out_shape, out_specs = (shp(F32), shp(BF16)), (oblk, oblk)
    else:
        out_shape, out_specs = shp(F32 if kind == "raw" else BF16), oblk
    return pl.pallas_call(
        kern,
        out_shape=out_shape,
        grid=(nb, tb // tm),
        in_specs=in_specs,
        out_specs=out_specs,
        compiler_params=_cparams(("arbitrary", "arbitrary")),
        name="proj_" + kind,
    )(*args)


def _conv_kernel(prev_ref, cur_ref, next_ref, w_ref, bc_ref, sc_ref, out_ref, *, n_tiles, tr, halo):
    i = pl.program_id(1)
    is_ctx = i == 0
    prev_ok = i >= 2
    next_ok = jnp.logical_and(i >= 1, i < n_tiles - 1)
    margin = GRID_W + 8
    full = jnp.concatenate([jnp.where(prev_ok, prev_ref[0, halo - margin:halo, :], 0.0), cur_ref[0],
                            jnp.where(next_ok, next_ref[0, 0:margin, :], 0.0)], axis=0)
    w = w_ref[...]
    sums = []
    for dc in (-1, 0, 1):
        acc = None
        for dr in (-1, 0, 1):
            tap = (dr + 1) * 3 + (dc + 1)
            wt = w[tap:tap + 1, :]
            if dr != 0:
                wt = jnp.where(is_ctx, 0.0, wt)
            start = margin + GRID_W * dr - 8
            term = wt * full[start:start + tr + 16, :]
            acc = term if acc is None else acc + term
        sums.append(acc)
    col = lax.broadcasted_iota(jnp.int32, (tr, 1), 0) % GRID_W
    rows = tr + 16
    left = pltpu.roll(sums[0], 1, axis=0)[8:8 + tr, :]
    right = pltpu.roll(sums[2], rows - 1, axis=0)[8:8 + tr, :]
    y = (sums[1][8:8 + tr, :] + jnp.where(jnp.logical_or(is_ctx, col != 0), left, 0.0)
         + jnp.where(jnp.logical_or(is_ctx, col != GRID_W - 1), right, 0.0) + bc_ref[...])
    out_ref[0] = (y * _sigmoid(y) * sc_ref[...]).astype(out_ref.dtype)


def _conv_silu(raw, w9, bconv, scale, ctx_len):
    nb, tb, n_qk = raw.shape
    tr, halo, tc = CONV_ROWS, CONV_HALO, CONV_COLS
    assert ctx_len == tr and tb % tr == 0 and tr % GRID_W == 0 and halo % GRID_W == 0
    n_tiles = tb // tr
    per = tr // halo
    n_halo = tb // halo
    kern = functools.partial(_conv_kernel, n_tiles=n_tiles, tr=tr, halo=halo)
    return pl.pallas_call(
        kern,
        out_shape=jax.ShapeDtypeStruct((nb, tb, n_qk), BF16),
        grid=(nb, n_tiles, n_qk // tc),
        in_specs=[pl.BlockSpec((1, halo, tc), lambda b, i, j: (b, jnp.maximum(i * per - 1, 0), j)),
                  pl.BlockSpec((1, tr, tc), lambda b, i, j: (b, i, j)),
                  pl.BlockSpec((1, halo, tc), lambda b, i, j: (b, jnp.minimum((i + 1) * per, n_halo - 1), j)),
                  pl.BlockSpec((9, tc), lambda b, i, j: (0, j)),
                  pl.BlockSpec((1, tc), lambda b, i, j: (0, j)),
                  pl.BlockSpec((1, tc), lambda b, i, j: (0, j))],
        out_specs=pl.BlockSpec((1, tr, tc), lambda b, i, j: (b, i, j)),
        compiler_params=_cparams(("arbitrary", "arbitrary", "arbitrary")),
        name="conv_silu",
    )(raw, raw, raw, w9, bconv, scale)


def _mlstm_kernel(qkf_ref, vf_ref, gf_ref, qkb_ref, vb_ref, gb_ref, hf_ref, hb_ref,
                  c_scr, n_scr, m_scr, *, la, dh):
    @pl.when(pl.program_id(1) == 0)
    def _():
        c_scr[...] = jnp.zeros_like(c_scr)
        n_scr[...] = jnp.zeros_like(n_scr)
        m_scr[...] = jnp.zeros_like(m_scr)

    r = lax.broadcasted_iota(jnp.int32, (la, la), 0)
    s = lax.broadcasted_iota(jnp.int32, (la, la), 1)
    ones_blk = jnp.ones((la, LANES), BF16)
    wide = lambda x, n: jnp.concatenate([x] * (n // LANES), axis=1)
    dirs = ((qkf_ref, vf_ref, gf_ref, hf_ref), (qkb_ref, vb_ref, gb_ref, hb_ref))
    for d, (qk_ref, v_ref, g_ref, h_ref) in enumerate(dirs):
        seen = (s <= r) if d == 0 else (s >= r)
        tri = jnp.where(seen, 1.0, 0.0).astype(BF16)
        g = g_ref[0]
        lsig = jnp.minimum(g, 0.0) - jnp.log(1.0 + jnp.exp(-jnp.abs(g)))
        bc = _cumsum_mm(tri, lsig)
        b2 = pltpu.roll(bc, LANES - H_A, axis=1) * LOG2E
        r2 = g * LOG2E - b2
        r2_t = r2.T
        last = la - 1 if d == 0 else 0
        b2_l = b2[last:last + 1, :]
        m2 = m_scr[d:d + 1, :]
        ws2 = r2 + b2_l
        m2_new = jnp.maximum(b2_l + m2, jnp.max(ws2, axis=0, keepdims=True))
        decay = jnp.exp2(b2_l + m2 - m2_new)
        wse = jnp.exp2(ws2 - m2_new)
        m_scr[d:d + 1, :] = m2_new
        cols = [2 * d * H_A + h for h in range(H_A)]
        q = [qk_ref[0, :, h * dh:(h + 1) * dh] for h in range(H_A)]
        k = [qk_ref[0, :, (H_A + h) * dh:(H_A + h + 1) * dh] for h in range(H_A)]
        v = [v_ref[0, :, h * dh:(h + 1) * dh] for h in range(H_A)]
        col = lambda x, ci: jnp.broadcast_to(x[:, ci:ci + 1], (la, LANES))
        rm = [jnp.where(seen, r2_t[ci:ci + 1, :], NEG) for ci in cols]
        mx = [jnp.broadcast_to(jnp.maximum(m2[:, ci:ci + 1], jnp.max(rm_h, axis=-1, keepdims=True)),
                               (la, LANES)) for ci, rm_h in zip(cols, rm)]
        s_mat = [_dot_nt(q[h], k[h]) for h in range(H_A)]
        for h, ci in enumerate(cols):
            u = d * H_A + h
            w16 = (s_mat[h] * jnp.exp2(rm[h] - wide(mx[h], la))).astype(BF16)
            a_in = jnp.exp2(m2[:, ci:ci + 1] - mx[h])
            em = jnp.exp2(-(col(b2, ci) + mx[h]))
            c_st = c_scr[u]
            n_st = n_scr[u]
            qa = q[h] * wide(a_in.astype(BF16), dh)
            num = jnp.dot(jnp.concatenate([w16, qa], axis=1),
                          jnp.concatenate([v[h], c_st.astype(BF16)], axis=0), preferred_element_type=F32)
            qn = _dot_nt(q[h], jnp.broadcast_to(n_st, (LANES, dh)).astype(BF16))
            den = a_in * qn + jnp.dot(w16, ones_blk, preferred_element_type=F32)
            rcp = 1.0 / jnp.maximum(jnp.abs(den), em)
            h_ref[0, :, h * dh:(h + 1) * dh] = (num * wide(rcp, dh)).astype(h_ref.dtype)
            wse_u = col(wse, ci).astype(BF16)
            dec = decay[:, ci:ci + 1]
            c_scr[u] = dec * c_st + _dot_tn(k[h] * wide(wse_u, dh), v[h])
            n_scr[u] = dec * n_st + _dot_tn(wse_u, k[h])[0:1, :]


def _bwd_chunk(c, n_ctx_chunks, n_chunks):
    return jnp.where(c < n_ctx_chunks, n_ctx_chunks - 1 - c, n_chunks - 1 + n_ctx_chunks - c)


def _mlstm(qk, mix, g, ctx_len, d_a):
    nb, tb, _ = qk.shape
    la = L_A
    dh = d_a // H_A
    assert ctx_len % la == 0 and tb % la == 0
    nc, ncc = tb // la, ctx_len // la
    bw = functools.partial(_bwd_chunk, n_ctx_chunks=ncc, n_chunks=nc)
    kern = functools.partial(_mlstm_kernel, la=la, dh=dh)
    hshape = jax.ShapeDtypeStruct((nb, tb, d_a), BF16)
    return pl.pallas_call(
        kern,
        out_shape=(hshape, hshape),
        grid=(nb, nc),
        in_specs=[pl.BlockSpec((1, la, 2 * d_a), lambda b, c: (b, c, 0)),
                  pl.BlockSpec((1, la, d_a), lambda b, c: (b, c, MIX_V)),
                  pl.BlockSpec((1, la, LANES), lambda b, c: (b, c, 0)),
                  pl.BlockSpec((1, la, 2 * d_a), lambda b, c: (b, bw(c), 0)),
                  pl.BlockSpec((1, la, d_a), lambda b, c: (b, bw(c), MIX_V)),
                  pl.BlockSpec((1, la, LANES), lambda b, c: (b, bw(c), 0))],
        out_specs=(pl.BlockSpec((1, la, d_a), lambda b, c: (b, c, 0)),
                   pl.BlockSpec((1, la, d_a), lambda b, c: (b, bw(c), 0))),
        scratch_shapes=[pltpu.VMEM((2 * H_A, dh, dh), F32),
                        pltpu.VMEM((2 * H_A, 1, dh), F32),
                        pltpu.VMEM((8, LANES), F32)],
        compiler_params=_cparams(("arbitrary", "arbitrary")),
        name="mlstm",
    )(qk, mix, g, qk, mix, g)


def _hgrn_levels(l):
    return [1 << j for j in range(int(np.log2(l)))]


def _hgrn_q_rows(l, h, d):
    off = h if d == 0 else 0
    return [(k * 2 * h + off, k * 2 * h + off + h) for k in range(l // (2 * h))]


def _hgrn_masks(l):
    r = np.arange(l)[:, None]
    s = np.arange(l)[None, :]
    full, half = [[], []], [[], []]
    for d in (0, 1):
        full[d].append(r == s)
        for h in _hgrn_levels(l):
            same = (r // (2 * h)) == (s // (2 * h))
            m = same & ((r % (2 * h)) >= h) & ((s % (2 * h)) < h)
            m = m if d == 0 else m.T
            if h >= HGRN_COMPACT_MIN:
                half[d].append(np.concatenate([m[a:b] for a, b in _hgrn_q_rows(l, h, d)]))
            else:
                full[d].append(m)
    local = [(r // HGRN_LOCAL == s // HGRN_LOCAL) & (s <= r), (r // HGRN_LOCAL == s // HGRN_LOCAL) & (s >= r)]
    return np.array(full, np.float32), np.array(half, np.float32), np.array(local, np.float32)


def _hgrn_tri(l):
    r = np.arange(l)[:, None]
    s = np.arange(l)[None, :]
    return np.stack([(s <= r), (s >= r)]).astype(np.float32)


def _hgrn_split_rows(g_scr, l, h, d):
    assert h in (2, 4)
    off = h - 1 if d == 0 else h
    n = g_scr.shape[-1]
    sub = lax.broadcasted_iota(jnp.int32, (8, n), 0)
    pieces = []
    for grp in range(l // 8):
        lo = jnp.broadcast_to(g_scr[grp * 8 + off:grp * 8 + off + 1, :], (8, n))
        if h == 4:
            pieces.append(lo)
        else:
            hi = jnp.broadcast_to(g_scr[grp * 8 + 4 + off:grp * 8 + 4 + off + 1, :], (8, n))
            pieces.append(jnp.where(sub < 4, lo, hi))
    return jnp.concatenate(pieces, axis=0)


def _hgrn_block_decay(g, l):
    spans = [jnp.abs(g[b0:b0 + 1, :] - g[b0 + HGRN_LOCAL - 1:b0 + HGRN_LOCAL, :]) for b0 in range(0, l, HGRN_LOCAL)]
    return jnp.max(jnp.concatenate(spans, axis=0))


def _hgrn_chunk(dirs, mfull_ref, mhalf_ref, mlocal_ref, st_scr, g_scr, *, l, dh, fast, rows):
    heads = [slice(h * dh, (h + 1) * dh) for h in range(H_B)]
    q = [dirs[d][0][0, rows[d], :] for d in (0, 1)]
    kk = [dirs[d][2][0, rows[d], :] for d in (0, 1)]

    def scores(qs, ks, mask):
        return [_dot_nt(qs[:, hs], ks[:, hs]).astype(BF16) * mask for hs in heads]

    if fast:
        acc = []
        for d in (0, 1):
            g = g_scr.at[d]
            parts = []
            for b0 in range(0, l, HGRN_LOCAL):
                base = b0 if d == 0 else b0 + HGRN_LOCAL - 1
                parts.append(g[b0:b0 + HGRN_LOCAL, :] - g[base:base + 1, :])
            x = jnp.concatenate(parts, axis=0)
            acc.append(scores(q[d] * jnp.exp2(x).astype(BF16), kk[d] * jnp.exp2(-x).astype(BF16), mlocal_ref[d]))
        levels = [hl for hl in _hgrn_levels(l) if hl >= HGRN_LOCAL]
    else:
        acc = [scores(q[d], kk[d], mfull_ref[d, 0]) for d in (0, 1)]
        levels = _hgrn_levels(l)
    all_levels = _hgrn_levels(l)
    for hl in levels:
        n_full = 1 + all_levels.index(hl)
        n_half = all_levels.index(hl) - all_levels.index(HGRN_COMPACT_MIN)
        for d in (0, 1):
            g = g_scr.at[d]
            m_off = hl - 1 if d == 0 else hl
            q_rows = _hgrn_q_rows(l, hl, d)
            k_rows = _hgrn_q_rows(l, hl, 1 - d)
            if hl < HGRN_COMPACT_MIN:
                if hl == 1:
                    qs, ks = q[d] * jnp.exp2(dirs[d][3][0, rows[d], :]).astype(BF16), kk[d]
                else:
                    if hl < 8:
                        x = -jnp.abs(g[...] - _hgrn_split_rows(g, l, hl, d))
                    else:
                        parts = {}
                        for (q0, q1), (k0, k1) in zip(q_rows, k_rows):
                            gm = g[min(q0, k0) + m_off:min(q0, k0) + m_off + 1, :]
                            parts[q0] = g[q0:q1, :] - gm
                            parts[k0] = gm - g[k0:k1, :]
                        x = jnp.concatenate([parts[r0] for r0 in sorted(parts)], axis=0)
                    e = jnp.exp2(x).astype(BF16)
                    qs, ks = q[d] * e, kk[d] * e
                acc[d] = [a + t for a, t in zip(acc[d], scores(qs, ks, mfull_ref[d, n_full]))]
            else:
                xq, xk = [], []
                for (q0, q1), (k0, k1) in zip(q_rows, k_rows):
                    gm = g[min(q0, k0) + m_off:min(q0, k0) + m_off + 1, :]
                    xq.append(g[q0:q1, :] - gm)
                    xk.append(gm - g[k0:k1, :])
                eq = jnp.exp2(jnp.concatenate(xq, axis=0)).astype(BF16)
                ek = jnp.exp2(jnp.concatenate(xk, axis=0)).astype(BF16)
                qc = jnp.concatenate([q[d][q0:q1] for q0, q1 in q_rows], axis=0) * eq
                kparts = {q0: kk[d][q0:q1] for q0, q1 in q_rows}
                for n, (k0, k1) in enumerate(k_rows):
                    kparts[k0] = kk[d][k0:k1] * ek[n * hl:(n + 1) * hl]
                kt = jnp.concatenate([kparts[r0] for r0 in sorted(kparts)], axis=0)
                terms = scores(qc, kt, mhalf_ref[d, n_half])
                for i in range(H_B):
                    pieces = {k0: acc[d][i][k0:k1] for k0, k1 in k_rows}
                    for n, (q0, q1) in enumerate(q_rows):
                        pieces[q0] = acc[d][i][q0:q1] + terms[i][n * hl:(n + 1) * hl]
                    acc[d][i] = jnp.concatenate([pieces[r0] for r0 in sorted(pieces)], axis=0)
    for d in (0, 1):
        o_ref = dirs[d][4]
        v = dirs[d][1][0, rows[d], :]
        last = l - 1 if d == 0 else 0
        g_all = g_scr[d]
        g_l = g_scr[d, last:last + 1, :]
        qi = q[d] * jnp.exp2(g_all).astype(BF16)
        ks = kk[d] * jnp.exp2(g_l - g_all).astype(BF16)
        dec = jnp.exp2(g_l)
        for i, hs in enumerate(heads):
            u = d * H_B + i
            st = st_scr[u]
            o = _dot_nt(qi[:, hs], st.astype(BF16)) + jnp.dot(acc[d][i], v[:, hs], preferred_element_type=F32)
            o_ref[0, rows[d], hs] = o.astype(o_ref.dtype)
            st_scr[u] = st * dec[:, hs] + _dot_tn(v[:, hs], ks[:, hs])


def _hgrn_kernel(qf_ref, vf_ref, kf_ref, lff_ref, qb_ref, vb_ref, kb_ref, lfb_ref,
                 tri_ref, mfull_ref, mhalf_ref, mlocal_ref, of_ref, ob_ref, st_scr, g_scr, *, l, dh):
    @pl.when(pl.program_id(1) == 0)
    def _():
        st_scr[...] = jnp.zeros_like(st_scr)

    dirs = ((qf_ref, vf_ref, kf_ref, lff_ref, of_ref), (qb_ref, vb_ref, kb_ref, lfb_ref, ob_ref))
    n_sub = qf_ref.shape[1] // l
    rows = [(slice(j * l, (j + 1) * l), slice((n_sub - 1 - j) * l, (n_sub - j) * l)) for j in range(n_sub)]
    worst = None
    for j in range(n_sub):
        for d in (0, 1):
            g_scr[j, d] = _cumsum_mm(tri_ref[d], dirs[d][3][0, rows[j][d], :], terms=HGRN_CUMSUM_TERMS)
            span = _hgrn_block_decay(g_scr.at[j, d], l)
            worst = span if worst is None else jnp.maximum(worst, span)
    in_range = worst <= HGRN_LOCAL_MAX_LOG2

    def run(fast):
        for j in range(n_sub):
            _hgrn_chunk(dirs, mfull_ref, mhalf_ref, mlocal_ref, st_scr, g_scr.at[j], l=l, dh=dh, fast=fast,
                        rows=rows[j])

    pl.when(in_range)(functools.partial(run, True))
    pl.when(jnp.logical_not(in_range))(functools.partial(run, False))


def _hgrn(mix, kdec, lf, ctx_len, d_b):
    nb, tb, _ = mix.shape
    l = L_B
    dh = d_b // H_B
    assert dh == LANES and ctx_len % l == 0 and tb % l == 0
    step = HGRN_STEP_CHUNKS * l
    assert ctx_len % step == 0 and tb % step == 0
    nc, ncc = tb // step, ctx_len // step
    bw = functools.partial(_bwd_chunk, n_ctx_chunks=ncc, n_chunks=nc)
    kern = functools.partial(_hgrn_kernel, l=l, dh=dh)
    tri = jnp.asarray(_hgrn_tri(l), BF16)
    mfull, mhalf, mlocal = (jnp.asarray(m, BF16) for m in _hgrn_masks(l))
    oshape = jax.ShapeDtypeStruct((nb, tb, d_b), BF16)
    blk = lambda col, chunk: pl.BlockSpec((1, step, d_b), lambda b, c: (b, chunk(c), col))
    ident = lambda c: c
    const = lambda a: pl.BlockSpec(a.shape, lambda b, c: (0,) * a.ndim)
    return pl.pallas_call(
        kern,
        out_shape=(oshape, oshape),
        grid=(nb, nc),
        in_specs=[blk(MIX_QB, ident), blk(MIX_IB, ident), blk(0, ident), blk(0, ident),
                  blk(MIX_QB, bw), blk(MIX_IB, bw), blk(1, bw), blk(1, bw),
                  const(tri), const(mfull), const(mhalf), const(mlocal)],
        out_specs=(blk(0, ident), blk(0, bw)),
        scratch_shapes=[pltpu.VMEM((2 * H_B, dh, dh), F32),
                        pltpu.VMEM((HGRN_STEP_CHUNKS, 2, l, d_b), F32)],
        compiler_params=_cparams(("arbitrary", "arbitrary")),
        name="hgrn",
    )(mix, mix, kdec, lf, mix, mix, kdec, lf, tri, mfull, mhalf, mlocal)


def _head_rms(x, n_heads):
    dh = x.shape[-1] // n_heads
    outs = []
    for h in range(n_heads):
        xh = x[:, h * dh:(h + 1) * dh]
        outs.append(xh * lax.rsqrt(jnp.mean(xh * xh, axis=-1, keepdims=True) + EPS))
    return jnp.concatenate(outs, axis=-1)


def _merge_kernel(*refs, ctx_len, tm, n_batch, fuse_next):
    (x_ref, haf_ref, hab_ref, hbf_ref, hbb_ref, o_ref, za_ref, zb_ref, ma_ref, mb_ref,
     mod_ref, gha_ref, ghb_ref, wa_ref, wb_ref, wo_ref) = refs[:16]
    b = pl.program_id(0)
    d = x_ref.shape[-1]
    f32 = lambda ref: ref[0].astype(F32)
    h_a = (f32(haf_ref) + f32(hab_ref)) * f32(o_ref)
    y_a = _head_rms(h_a, H_A) * gha_ref[...] * f32(za_ref)
    y_b = _head_rms(f32(hbf_ref) + f32(hbb_ref), H_B) * ghb_ref[...] * f32(zb_ref)
    pa = jnp.dot(y_a.astype(BF16), wa_ref[...], preferred_element_type=F32)
    pb = jnp.dot(y_b.astype(BF16), wb_ref[...], preferred_element_type=F32)
    y = f32(ma_ref) * pa + f32(mb_ref) * pb
    br = jnp.dot(y.astype(BF16), wo_ref[...], preferred_element_type=F32)
    is_ctx = pl.program_id(1) * tm + lax.broadcasted_iota(jnp.int32, (tm, 1), 0) < ctx_len
    gate = jnp.where(is_ctx, mod_ref[pl.ds(n_batch, 1), 2 * d:3 * d], mod_ref[pl.ds(b, 1), 2 * d:3 * d])
    x_new = x_ref[0] + gate * br
    if fuse_next:
        nmod_ref, ngn_ref, nwg_ref, nbg_ref, out_ref, h_ref, g_ref = refs[16:]
        h_ref[0], g_ref[0] = _norm_modulate(x_new, is_ctx, b, nmod_ref, ngn_ref, nwg_ref, nbg_ref, n_batch)
    else:
        out_ref, = refs[16:]
    out_ref[0] = x_new


def _merge(xall, haf, hab, hbf, hbb, sig, silu, mod_l, gha, ghb, wa, wb, wo, ctx_len, nxt=None):
    nb, tb, d = xall.shape
    tm = 256
    kern = functools.partial(_merge_kernel, ctx_len=ctx_len, tm=tm, n_batch=nb, fuse_next=nxt is not None)
    row = pl.BlockSpec((1, tm, d), lambda b, i: (b, i, 0))
    pcol = lambda col: pl.BlockSpec((1, tm, d), lambda b, i: (b, i, col))
    full = lambda shape: pl.BlockSpec(shape, lambda b, i: (0,) * len(shape))
    in_specs = [row, row, row, row, row,
                pcol(SIG_O), pcol(SILU_ZA), pcol(SILU_ZB), pcol(SIG_MA), pcol(SIG_MB),
                full((8, 3 * d)), full((1, d)), full((1, d)),
                full((d, d)), full((d, d)), full((d, d))]
    args = [xall, haf, hab, hbf, hbb, sig, silu, silu, sig, sig, mod_l, gha, ghb, wa, wb, wo]
    out_shape = jax.ShapeDtypeStruct(xall.shape, F32)
    out_specs = row
    if nxt is not None:
        in_specs += [full((8, 3 * d)), full((1, d)), full((2, d, LANES)), full((1, LANES))]
        args += list(nxt)
        out_shape = (out_shape, jax.ShapeDtypeStruct((nb, tb, d), BF16),
                     jax.ShapeDtypeStruct((nb, tb, LANES), F32))
        out_specs = (row, row, pl.BlockSpec((1, tm, LANES), lambda b, i: (b, i, 0)))
    return pl.pallas_call(
        kern,
        out_shape=out_shape,
        grid=(nb, tb // tm),
        in_specs=in_specs,
        out_specs=out_specs,
        compiler_params=_cparams(("arbitrary", "arbitrary")),
        name="merge",
    )(*args)


def _final_kernel(x_ref, g_ref, out_ref):
    x = x_ref[0]
    out_ref[0] = x * lax.rsqrt(jnp.mean(x * x, axis=-1, keepdims=True) + EPS) * g_ref[...]


def _final_norm(xall, g, ctx_len):
    nb, tb, d = xall.shape
    tm = 256
    off = ctx_len // tm
    return pl.pallas_call(
        _final_kernel,
        out_shape=jax.ShapeDtypeStruct((nb, tb - ctx_len, d), F32),
        grid=(nb, (tb - ctx_len) // tm),
        in_specs=[pl.BlockSpec((1, tm, d), lambda b, i: (b, i + off, 0)),
                  pl.BlockSpec((1, d), lambda b, i: (0, 0))],
        out_specs=pl.BlockSpec((1, tm, d), lambda b, i: (b, i, 0)),
        compiler_params=_cparams(("arbitrary", "arbitrary")),
        name="final_norm",
    )(xall, g)


def kernel(x, c, ctx, c_ctx, w_ada, b_ada, g_norm, w_in, b_in, w_conv, b_conv, lb_logits,
           g_head_a, g_head_b, w_a, w_b, w_out, g_final):
    nb, seq, d = x.shape
    ctx_len = ctx.shape[1]
    depth = w_ada.shape[0]
    d_a = g_head_a.shape[-1]
    d_b = g_head_b.shape[-1]
    assert d_a == d and d_b == d and nb + 1 <= 8
    dh_a = d_a // H_A

    g0 = 5 * d
    g1 = g0 + N_GATES
    grp = lambda n: slice(n * d, (n + 1) * d) if n < 5 else slice(g1 + (n - 5) * d, g1 + (n - 4) * d)
    QK0, QK1, V, O, ZA, QB, IB, FF, FB, ZB, MA, MB = range(12)
    order = (V, QB, IB, O, MA, MB, QK0, QK1, ZA, ZB, FF, FB)
    w_main = jnp.concatenate([w_in[:, :, grp(n)] for n in order], axis=-1).astype(BF16)
    b_main = jnp.concatenate([b_in[:, grp(n)] for n in order], axis=-1).astype(F32)
    wg = jnp.pad(w_in[:, :, g0:g1].astype(F32), ((0, 0), (0, 0), (0, LANES - N_GATES)))
    wg_hi = wg.astype(BF16)
    wg_lo = (wg - wg_hi.astype(F32)).astype(BF16)
    wg2 = jnp.stack([wg_hi, wg_lo], axis=1)
    bg = jnp.pad(b_in[:, g0:g1].astype(F32), ((0, 0), (0, LANES - N_GATES)))
    qk_scale = jnp.concatenate([jnp.ones((1, d_a), F32), jnp.full((1, d_a), dh_a ** -0.5, F32)], axis=-1)
    w9 = w_conv.reshape(depth, 9, 2 * d_a).astype(F32)

    cc = jnp.concatenate([c.astype(F32), c_ctx.astype(F32)[None, :],
                          jnp.zeros((8 - nb - 1, d), F32)], axis=0)
    mod = _ada(cc, w_ada.astype(F32), b_ada.astype(F32))
    lbs = _lbs(lb_logits)
    c_sig, c_raw, c_silu, c_dec = (int(n) for n in np.cumsum((N_MIX, N_SIG, N_RAW, N_SILU)))

    xall = jnp.concatenate([ctx.astype(F32), x.astype(F32)], axis=1)
    norm_args = lambda l: (mod[l], g_norm[l][None, :].astype(F32), wg2[l], bg[l][None, :])
    h16, g = _norm_gates(xall, *norm_args(0), ctx_len)
    for l in range(depth):
        bm = b_main[l][None, :]
        raw = _proj(h16, w_main[l], bm, c_raw, N_RAW, "raw")
        mix = _proj(h16, w_main[l], bm, 0, N_MIX, "cast")
        sig = _proj(h16, w_main[l], bm, c_sig, N_SIG, "sigmoid")
        silu = _proj(h16, w_main[l], bm, c_silu, N_SILU, "silu")
        lf, kdec = _proj(h16, w_main[l], bm, c_dec, N_DECAY, "decay", lbs[l][None, :])
        qk = _conv_silu(raw, w9[l], b_conv[l][None, :].astype(F32), qk_scale, ctx_len)
        haf, hab = _mlstm(qk, mix, g, ctx_len, d_a)
        hbf, hbb = _hgrn(mix, kdec, lf, ctx_len, d_b)
        res = _merge(xall, haf, hab, hbf, hbb, sig, silu, mod[l], g_head_a[l][None, :].astype(F32),
                     g_head_b[l][None, :].astype(F32), w_a[l].astype(BF16), w_b[l].astype(BF16),
                     w_out[l].astype(BF16), ctx_len, nxt=norm_args(l + 1) if l + 1 < depth else None)
        xall, h16, g = res if l + 1 < depth else (res, None, None)
    return _final_norm(xall, g_final[None, :].astype(F32), ctx_len).astype(x.dtype)
```

```python
import functools
import math

import numpy as np
import jax
import jax.numpy as jnp
from jax import lax
from jax.experimental import pallas as pl
from jax.experimental.pallas import tpu as pltpu

F32 = jnp.float32
BF16 = jnp.bfloat16

H_A = 4
H_B = 8
GRID_W = 64
EPS = 1e-6
NEG = -1e30
LOG2E = math.log2(math.e)
N_GATES = 4 * H_A
LANES = 128
L_A = 256
L_B = 128
HGRN_COMPACT_MIN = 16
HGRN_LOCAL = 32
HGRN_LOCAL_MAX_LOG2 = 100.0
HGRN_STEP_CHUNKS = 2
CONV_ROWS = 256
CONV_HALO = 128
HGRN_CUMSUM_TERMS = 2
MLSTM_CUMSUM_TERMS = 2
CONV_COLS = 2048
PROJ_ROWS = 256
MERGE_ROWS = 256
V7X_VMEM_LIMIT = 56 * 1024 * 1024
N_MIX, N_SIG, N_RAW, N_SILU, N_DECAY = 3, 3, 2, 2, 2
MIX_V, MIX_QB, MIX_IB = 0, 1, 2
SIG_O, SIG_MA, SIG_MB = 0, 1, 2
SILU_ZA, SILU_ZB = 0, 1


def _cparams(semantics):
    return pltpu.CompilerParams(dimension_semantics=semantics, vmem_limit_bytes=V7X_VMEM_LIMIT)


def _sigmoid(x):
    return 1.0 / (1.0 + jnp.exp(-x))


def _sigmoid_tanh(x):
    return 0.5 * jnp.tanh(0.5 * x) + 0.5


def _split3(x):
    hi = x.astype(BF16)
    r1 = x - hi.astype(F32)
    mid = r1.astype(BF16)
    lo = (r1 - mid.astype(F32)).astype(BF16)
    return hi, mid, lo


def _cumsum_mm(tri, x, terms=3):
    out = None
    for part in _split3(x)[:terms]:
        term = jnp.dot(tri, part, preferred_element_type=F32)
        out = term if out is None else out + term
    return out


def _dot_nt(a, b):
    return lax.dot_general(a, b, (((1,), (1,)), ((), ())), preferred_element_type=F32)


def _dot_tn(a, b):
    return lax.dot_general(a, b, (((0,), (0,)), ((), ())), preferred_element_type=F32)


def _ada_kernel(cc_ref, w_ref, b_ref, out_ref):
    s = cc_ref[...]
    s = s * _sigmoid(s)
    w = w_ref[0]
    s_hi = s.astype(BF16)
    s_lo = (s - s_hi.astype(F32)).astype(BF16)
    w_hi = w.astype(BF16)
    w_lo = (w - w_hi.astype(F32)).astype(BF16)
    d = lambda a, b: jnp.dot(a, b, preferred_element_type=F32)
    out_ref[0] = d(s_hi, w_hi) + d(s_lo, w_hi) + d(s_hi, w_lo) + b_ref[0]


def _ada(cc, w_ada, b_ada):
    depth, d, n3 = w_ada.shape
    tn = 512
    return pl.pallas_call(
        _ada_kernel,
        out_shape=jax.ShapeDtypeStruct((depth, 8, n3), F32),
        grid=(depth, n3 // tn),
        in_specs=[pl.BlockSpec((8, d), lambda l, j: (0, 0)),
                  pl.BlockSpec((1, d, tn), lambda l, j: (l, 0, j)),
                  pl.BlockSpec((1, 1, tn), lambda l, j: (l, 0, j))],
        out_specs=pl.BlockSpec((1, 8, tn), lambda l, j: (l, 0, j)),
        compiler_params=_cparams(("arbitrary", "arbitrary")),
        name="ada",
    )(cc, w_ada, b_ada.reshape(depth, 1, n3))


def _lbs_kernel(lg_ref, out_ref):
    x = lg_ref[...]
    depth = x.shape[0]
    m = jnp.max(x, axis=0, keepdims=True)
    e = jnp.exp(x - m)
    p = e / jnp.sum(e, axis=0, keepdims=True)
    acc = jnp.zeros_like(p[0:1])
    for l in range(depth):
        acc = acc + p[l:l + 1]
        out_ref[l:l + 1, :] = acc - p[0:1]


def _lbs(lb_logits):
    return pl.pallas_call(
        _lbs_kernel, out_shape=jax.ShapeDtypeStruct(lb_logits.shape, F32), name="lbs",
    )(lb_logits.astype(F32))


def _norm_modulate(x, is_ctx, b, mod_ref, gn_ref, wg_ref, bg_ref, n_batch):
    d = x.shape[-1]
    ms = jnp.mean(x * x, axis=-1, keepdims=True)
    y = x * lax.rsqrt(ms + EPS) * gn_ref[...]
    mb = mod_ref[pl.ds(b, 1), :]
    mc = mod_ref[pl.ds(n_batch, 1), :]
    shift = jnp.where(is_ctx, mc[:, 0:d], mb[:, 0:d])
    scale = jnp.where(is_ctx, mc[:, d:2 * d], mb[:, d:2 * d])
    h = y * (1.0 + scale) + shift
    hi = h.astype(BF16)
    lo = (h - hi.astype(F32)).astype(BF16)
    dd = lambda a, bb: jnp.dot(a, bb, preferred_element_type=F32)
    return hi, dd(hi, wg_ref[0]) + dd(lo, wg_ref[0]) + dd(hi, wg_ref[1]) + bg_ref[...]


def _norm_kernel(x_ref, mod_ref, gn_ref, wg_ref, bg_ref, h_ref, g_ref, *, ctx_len, tm, n_batch):
    b = pl.program_id(0)
    row = pl.program_id(1) * tm + lax.broadcasted_iota(jnp.int32, (tm, 1), 0)
    h_ref[0], g_ref[0] = _norm_modulate(x_ref[0], row < ctx_len, b, mod_ref, gn_ref, wg_ref, bg_ref, n_batch)


def _pick_tile(n, candidates):
    for c in candidates:
        if n % c == 0:
            return c
    raise ValueError(f"no tile for {n}")


def _norm_gates(xall, mod_l, gn, wg, bg, ctx_len):
    nb, tb, d = xall.shape
    tm = _pick_tile(tb, (640, 256))
    kern = functools.partial(_norm_kernel, ctx_len=ctx_len, tm=tm, n_batch=nb)
    return pl.pallas_call(
        kern,
        out_shape=(jax.ShapeDtypeStruct((nb, tb, d), BF16),
                   jax.ShapeDtypeStruct((nb, tb, LANES), F32)),
        grid=(nb, tb // tm),
        in_specs=[pl.BlockSpec((1, tm, d), lambda b, i: (b, i, 0)),
                  pl.BlockSpec((8, 3 * d), lambda b, i: (0, 0)),
                  pl.BlockSpec((1, d), lambda b, i: (0, 0)),
                  pl.BlockSpec((2, d, LANES), lambda b, i: (0, 0, 0)),
                  pl.BlockSpec((1, LANES), lambda b, i: (0, 0))],
        out_specs=(pl.BlockSpec((1, tm, d), lambda b, i: (b, i, 0)),
                   pl.BlockSpec((1, tm, LANES), lambda b, i: (b, i, 0))),
        compiler_params=_cparams(("arbitrary", "arbitrary")),
        name="norm_gates",
    )(xall, mod_l, gn, wg, bg)


def _proj_kernel(*refs, kind, rs):
    if kind == "decay":
        h_ref, w_ref, b_ref, lb_ref, lf_ref, k_ref = refs
    else:
        h_ref, w_ref, b_ref, out_ref = refs
    tm = h_ref.shape[1]
    for r0 in range(0, tm, rs):
        rows = slice(r0, r0 + rs)
        acc = jnp.dot(h_ref[0, rows, :], w_ref[...], preferred_element_type=F32) + b_ref[...]
        if kind == "raw":
            out_ref[0, rows, :] = acc
        elif kind == "cast":
            out_ref[0, rows, :] = acc.astype(BF16)
        elif kind == "sigmoid":
            out_ref[0, rows, :] = _sigmoid_tanh(acc).astype(BF16)
        elif kind == "silu":
            out_ref[0, rows, :] = (acc * _sigmoid_tanh(acc)).astype(BF16)
        else:
            lbv = lb_ref[...]
            t = jnp.exp2(jnp.abs(acc) * (-LOG2E))
            pos = acc >= 0.0
            rc = 1.0 / (1.0 + t)
            f = jnp.where(pos, 1.0 + lbv * t, t + lbv) * rc
            lf_ref[0, rows, :] = jnp.log(f) * LOG2E
            k_ref[0, rows, :] = ((1.0 - lbv) * jnp.where(pos, t, 1.0) * rc).astype(BF16)


def _proj(h16, w_main, b_main, col0, n_groups, kind, lb=None):
    nb, tb, d = h16.shape
    tm = _pick_tile(tb, (1280, 640, 256))
    tn = n_groups * d
    off = col0 // n_groups
    assert col0 % n_groups == 0
    kern = functools.partial(_proj_kernel, kind=kind, rs=PROJ_ROWS)
    in_specs = [pl.BlockSpec((1, tm, d), lambda b, i: (b, i, 0)),
                pl.BlockSpec((d, tn), lambda b, i: (0, off)),
                pl.BlockSpec((1, tn), lambda b, i: (0, off))]
    args = [h16, w_main, b_main]
    oblk = pl.BlockSpec((1, tm, tn), lambda b, i: (b, i, 0))
    shp = lambda dt: jax.ShapeDtypeStruct((nb, tb, tn), dt)
    if kind == "decay":
        in_specs.append(pl.BlockSpec((1, tn), lambda b, i: (0, 0)))
        args.append(jnp.concatenate([lb] * n_groups, axis=-1))
        out_shape, out_specs = (shp(F32), shp(BF16)), (oblk, oblk)
    else:
        out_shape, out_specs = shp(F32 if kind == "raw" else BF16), oblk
    return pl.pallas_call(
        kern,
        out_shape=out_shape,
        grid=(nb, tb // tm),
        in_specs=in_specs,
        out_specs=out_specs,
        compiler_params=_cparams(("arbitrary", "arbitrary")),
        name="proj_" + kind,
    )(*args)


def _conv_kernel(prev_ref, cur_ref, next_ref, w_ref, bc_ref, sc_ref, out_ref, *, n_tiles, tr, halo):
    i = pl.program_id(1)
    is_ctx = i == 0
    prev_ok = i >= 2
    next_ok = jnp.logical_and(i >= 1, i < n_tiles - 1)
    margin = GRID_W + 8
    full = jnp.concatenate([jnp.where(prev_ok, prev_ref[0, halo - margin:halo, :], 0.0), cur_ref[0],
                            jnp.where(next_ok, next_ref[0, 0:margin, :], 0.0)], axis=0)
    w = w_ref[...]
    sums = []
    for dc in (-1, 0, 1):
        acc = None
        for dr in (-1, 0, 1):
            tap = (dr + 1) * 3 + (dc + 1)
            wt = w[tap:tap + 1, :]
            if dr != 0:
                wt = jnp.where(is_ctx, 0.0, wt)
            start = margin + GRID_W * dr - 8
            term = wt * full[start:start + tr + 16, :]
            acc = term if acc is None else acc + term
        sums.append(acc)
    col = lax.broadcasted_iota(jnp.int32, (tr, 1), 0) % GRID_W
    rows = tr + 16
    left = pltpu.roll(sums[0], 1, axis=0)[8:8 + tr, :]
    right = pltpu.roll(sums[2], rows - 1, axis=0)[8:8 + tr, :]
    y = (sums[1][8:8 + tr, :] + jnp.where(jnp.logical_or(is_ctx, col != 0), left, 0.0)
         + jnp.where(jnp.logical_or(is_ctx, col != GRID_W - 1), right, 0.0) + bc_ref[...])
    out_ref[0] = (y * _sigmoid(y) * sc_ref[...]).astype(out_ref.dtype)


def _conv_silu(raw, w9, bconv, scale, ctx_len):
    nb, tb, n_qk = raw.shape
    tr, halo, tc = CONV_ROWS, CONV_HALO, CONV_COLS
    assert ctx_len == tr and tb % tr == 0 and tr % GRID_W == 0 and halo % GRID_W == 0
    n_tiles = tb // tr
    per = tr // halo
    n_halo = tb // halo
    kern = functools.partial(_conv_kernel, n_tiles=n_tiles, tr=tr, halo=halo)
    return pl.pallas_call(
        kern,
        out_shape=jax.ShapeDtypeStruct((nb, tb, n_qk), BF16),
        grid=(nb, n_tiles, n_qk // tc),
        in_specs=[pl.BlockSpec((1, halo, tc), lambda b, i, j: (b, jnp.maximum(i * per - 1, 0), j)),
                  pl.BlockSpec((1, tr, tc), lambda b, i, j: (b, i, j)),
                  pl.BlockSpec((1, halo, tc), lambda b, i, j: (b, jnp.minimum((i + 1) * per, n_halo - 1), j)),
                  pl.BlockSpec((9, tc), lambda b, i, j: (0, j)),
                  pl.BlockSpec((1, tc), lambda b, i, j: (0, j)),
                  pl.BlockSpec((1, tc), lambda b, i, j: (0, j))],
        out_specs=pl.BlockSpec((1, tr, tc), lambda b, i, j: (b, i, j)),
        compiler_params=_cparams(("arbitrary", "arbitrary", "arbitrary")),
        name="conv_silu",
    )(raw, raw, raw, w9, bconv, scale)


def _mlstm_kernel(qkf_ref, vf_ref, gf_ref, qkb_ref, vb_ref, gb_ref, hf_ref, hb_ref,
                  c_scr, n_scr, m_scr, *, la, dh):
    @pl.when(pl.program_id(1) == 0)
    def _():
        c_scr[...] = jnp.zeros_like(c_scr)
        n_scr[...] = jnp.zeros_like(n_scr)
        m_scr[...] = jnp.zeros_like(m_scr)

    r = lax.broadcasted_iota(jnp.int32, (la, la), 0)
    s = lax.broadcasted_iota(jnp.int32, (la, la), 1)
    ones_blk = jnp.ones((la, LANES), BF16)
    wide = lambda x, n: jnp.concatenate([x] * (n // LANES), axis=1)
    dirs = ((qkf_ref, vf_ref, gf_ref, hf_ref), (qkb_ref, vb_ref, gb_ref, hb_ref))
    for d, (qk_ref, v_ref, g_ref, h_ref) in enumerate(dirs):
        seen = (s <= r) if d == 0 else (s >= r)
        tri = jnp.where(seen, 1.0, 0.0).astype(BF16)
        g = g_ref[0]
        lsig = jnp.minimum(g, 0.0) - jnp.log(1.0 + jnp.exp(-jnp.abs(g)))
        bc = _cumsum_mm(tri, lsig, terms=MLSTM_CUMSUM_TERMS)
        b2 = pltpu.roll(bc, LANES - H_A, axis=1) * LOG2E
        r2 = g * LOG2E - b2
        r2_t = r2.T
        last = la - 1 if d == 0 else 0
        b2_l = b2[last:last + 1, :]
        m2 = m_scr[d:d + 1, :]
        ws2 = r2 + b2_l
        m2_new = jnp.maximum(b2_l + m2, jnp.max(ws2, axis=0, keepdims=True))
        decay = jnp.exp2(b2_l + m2 - m2_new)
        wse = jnp.exp2(ws2 - m2_new)
        m_scr[d:d + 1, :] = m2_new
        cols = [2 * d * H_A + h for h in range(H_A)]
        q = [qk_ref[0, :, h * dh:(h + 1) * dh] for h in range(H_A)]
        k = [qk_ref[0, :, (H_A + h) * dh:(H_A + h + 1) * dh] for h in range(H_A)]
        v = [v_ref[0, :, h * dh:(h + 1) * dh] for h in range(H_A)]
        col = lambda x, ci: jnp.broadcast_to(x[:, ci:ci + 1], (la, LANES))
        rm = [jnp.where(seen, r2_t[ci:ci + 1, :], NEG) for ci in cols]
        mx = [jnp.broadcast_to(jnp.maximum(m2[:, ci:ci + 1], jnp.max(rm_h, axis=-1, keepdims=True)),
                               (la, LANES)) for ci, rm_h in zip(cols, rm)]
        s_mat = [_dot_nt(q[h], k[h]) for h in range(H_A)]
        for h, ci in enumerate(cols):
            u = d * H_A + h
            w16 = (s_mat[h] * jnp.exp2(rm[h] - wide(mx[h], la))).astype(BF16)
            a_in = jnp.exp2(m2[:, ci:ci + 1] - mx[h])
            em = jnp.exp2(-(col(b2, ci) + mx[h]))
            c_st = c_scr[u]
            n_st = n_scr[u]
            qa = q[h] * wide(a_in.astype(BF16), dh)
            num = jnp.dot(jnp.concatenate([w16, qa], axis=1),
                          jnp.concatenate([v[h], c_st.astype(BF16)], axis=0), preferred_element_type=F32)
            qn = _dot_nt(q[h], jnp.broadcast_to(n_st, (LANES, dh)).astype(BF16))
            den = a_in * qn + jnp.dot(w16, ones_blk, preferred_element_type=F32)
            rcp = 1.0 / jnp.maximum(jnp.abs(den), em)
            h_ref[0, :, h * dh:(h + 1) * dh] = (num * wide(rcp, dh)).astype(h_ref.dtype)
            wse_u = col(wse, ci).astype(BF16)
            dec = decay[:, ci:ci + 1]
            c_scr[u] = dec * c_st + _dot_tn(k[h] * wide(wse_u, dh), v[h])
            n_scr[u] = dec * n_st + _dot_tn(wse_u, k[h])[0:1, :]


def _bwd_chunk(c, n_ctx_chunks, n_chunks):
    return jnp.where(c < n_ctx_chunks, n_ctx_chunks - 1 - c, n_chunks - 1 + n_ctx_chunks - c)


def _mlstm(qk, mix, g, ctx_len, d_a):
    nb, tb, _ = qk.shape
    la = L_A
    dh = d_a // H_A
    assert ctx_len % la == 0 and tb % la == 0
    nc, ncc = tb // la, ctx_len // la
    bw = functools.partial(_bwd_chunk, n_ctx_chunks=ncc, n_chunks=nc)
    kern = functools.partial(_mlstm_kernel, la=la, dh=dh)
    hshape = jax.ShapeDtypeStruct((nb, tb, d_a), BF16)
    return pl.pallas_call(
        kern,
        out_shape=(hshape, hshape),
        grid=(nb, nc),
        in_specs=[pl.BlockSpec((1, la, 2 * d_a), lambda b, c: (b, c, 0)),
                  pl.BlockSpec((1, la, d_a), lambda b, c: (b, c, MIX_V)),
                  pl.BlockSpec((1, la, LANES), lambda b, c: (b, c, 0)),
                  pl.BlockSpec((1, la, 2 * d_a), lambda b, c: (b, bw(c), 0)),
                  pl.BlockSpec((1, la, d_a), lambda b, c: (b, bw(c), MIX_V)),
                  pl.BlockSpec((1, la, LANES), lambda b, c: (b, bw(c), 0))],
        out_specs=(pl.BlockSpec((1, la, d_a), lambda b, c: (b, c, 0)),
                   pl.BlockSpec((1, la, d_a), lambda b, c: (b, bw(c), 0))),
        scratch_shapes=[pltpu.VMEM((2 * H_A, dh, dh), F32),
                        pltpu.VMEM((2 * H_A, 1, dh), F32),
                        pltpu.VMEM((8, LANES), F32)],
        compiler_params=_cparams(("arbitrary", "arbitrary")),
        name="mlstm",
    )(qk, mix, g, qk, mix, g)


def _hgrn_levels(l):
    return [1 << j for j in range(int(np.log2(l)))]


def _hgrn_q_rows(l, h, d):
    off = h if d == 0 else 0
    return [(k * 2 * h + off, k * 2 * h + off + h) for k in range(l // (2 * h))]


def _hgrn_masks(l):
    r = np.arange(l)[:, None]
    s = np.arange(l)[None, :]
    full, half = [[], []], [[], []]
    for d in (0, 1):
        full[d].append(r == s)
        for h in _hgrn_levels(l):
            same = (r // (2 * h)) == (s // (2 * h))
            m = same & ((r % (2 * h)) >= h) & ((s % (2 * h)) < h)
            m = m if d == 0 else m.T
            if h >= HGRN_COMPACT_MIN:
                half[d].append(np.concatenate([m[a:b] for a, b in _hgrn_q_rows(l, h, d)]))
            else:
                full[d].append(m)
    local = [(r // HGRN_LOCAL == s // HGRN_LOCAL) & (s <= r), (r // HGRN_LOCAL == s // HGRN_LOCAL) & (s >= r)]
    return np.array(full, np.float32), np.array(half, np.float32), np.array(local, np.float32)


def _hgrn_tri(l):
    r = np.arange(l)[:, None]
    s = np.arange(l)[None, :]
    return np.stack([(s <= r), (s >= r)]).astype(np.float32)


def _hgrn_split_rows(g_scr, l, h, d):
    assert h in (2, 4)
    off = h - 1 if d == 0 else h
    n = g_scr.shape[-1]
    sub = lax.broadcasted_iota(jnp.int32, (8, n), 0)
    pieces = []
    for grp in range(l // 8):
        lo = jnp.broadcast_to(g_scr[grp * 8 + off:grp * 8 + off + 1, :], (8, n))
        if h == 4:
            pieces.append(lo)
        else:
            hi = jnp.broadcast_to(g_scr[grp * 8 + 4 + off:grp * 8 + 4 + off + 1, :], (8, n))
            pieces.append(jnp.where(sub < 4, lo, hi))
    return jnp.concatenate(pieces, axis=0)


def _hgrn_block_decay(g, l):
    spans = [jnp.abs(g[b0:b0 + 1, :] - g[b0 + HGRN_LOCAL - 1:b0 + HGRN_LOCAL, :]) for b0 in range(0, l, HGRN_LOCAL)]
    return jnp.max(jnp.concatenate(spans, axis=0))


def _hgrn_chunk(dirs, mfull_ref, mhalf_ref, mlocal_ref, st_scr, g_scr, *, l, dh, fast, rows):
    heads = [slice(h * dh, (h + 1) * dh) for h in range(H_B)]
    q = [dirs[d][0][0, rows[d], :] for d in (0, 1)]
    kk = [dirs[d][2][0, rows[d], :] for d in (0, 1)]

    def scores(qs, ks, mask):
        return [_dot_nt(qs[:, hs], ks[:, hs]).astype(BF16) * mask for hs in heads]

    if fast:
        acc = []
        for d in (0, 1):
            g = g_scr.at[d]
            parts = []
            for b0 in range(0, l, HGRN_LOCAL):
                base = b0 if d == 0 else b0 + HGRN_LOCAL - 1
                parts.append(g[b0:b0 + HGRN_LOCAL, :] - g[base:base + 1, :])
            x = jnp.concatenate(parts, axis=0)
            acc.append(scores(q[d] * jnp.exp2(x).astype(BF16), kk[d] * jnp.exp2(-x).astype(BF16), mlocal_ref[d]))
        levels = [hl for hl in _hgrn_levels(l) if hl >= HGRN_LOCAL]
    else:
        acc = [scores(q[d], kk[d], mfull_ref[d, 0]) for d in (0, 1)]
        levels = _hgrn_levels(l)
    all_levels = _hgrn_levels(l)
    for hl in levels:
        n_full = 1 + all_levels.index(hl)
        n_half = all_levels.index(hl) - all_levels.index(HGRN_COMPACT_MIN)
        for d in (0, 1):
            g = g_scr.at[d]
            m_off = hl - 1 if d == 0 else hl
            q_rows = _hgrn_q_rows(l, hl, d)
            k_rows = _hgrn_q_rows(l, hl, 1 - d)
            if hl < HGRN_COMPACT_MIN:
                if hl == 1:
                    qs, ks = q[d] * jnp.exp2(dirs[d][3][0, rows[d], :]).astype(BF16), kk[d]
                else:
                    if hl < 8:
                        x = -jnp.abs(g[...] - _hgrn_split_rows(g, l, hl, d))
                    else:
                        parts = {}
                        for (q0, q1), (k0, k1) in zip(q_rows, k_rows):
                            gm = g[min(q0, k0) + m_off:min(q0, k0) + m_off + 1, :]
                            parts[q0] = g[q0:q1, :] - gm
                            parts[k0] = gm - g[k0:k1, :]
                        x = jnp.concatenate([parts[r0] for r0 in sorted(parts)], axis=0)
                    e = jnp.exp2(x).astype(BF16)
                    qs, ks = q[d] * e, kk[d] * e
                acc[d] = [a + t for a, t in zip(acc[d], scores(qs, ks, mfull_ref[d, n_full]))]
            else:
                xq, xk = [], []
                for (q0, q1), (k0, k1) in zip(q_rows, k_rows):
                    gm = g[min(q0, k0) + m_off:min(q0, k0) + m_off + 1, :]
                    xq.append(g[q0:q1, :] - gm)
                    xk.append(gm - g[k0:k1, :])
                eq = jnp.exp2(jnp.concatenate(xq, axis=0)).astype(BF16)
                ek = jnp.exp2(jnp.concatenate(xk, axis=0)).astype(BF16)
                qc = jnp.concatenate([q[d][q0:q1] for q0, q1 in q_rows], axis=0) * eq
                kparts = {q0: kk[d][q0:q1] for q0, q1 in q_rows}
                for n, (k0, k1) in enumerate(k_rows):
                    kparts[k0] = kk[d][k0:k1] * ek[n * hl:(n + 1) * hl]
                kt = jnp.concatenate([kparts[r0] for r0 in sorted(kparts)], axis=0)
                terms = scores(qc, kt, mhalf_ref[d, n_half])
                for i in range(H_B):
                    pieces = {k0: acc[d][i][k0:k1] for k0, k1 in k_rows}
                    for n, (q0, q1) in enumerate(q_rows):
                        pieces[q0] = acc[d][i][q0:q1] + terms[i][n * hl:(n + 1) * hl]
                    acc[d][i] = jnp.concatenate([pieces[r0] for r0 in sorted(pieces)], axis=0)
    for d in (0, 1):
        o_ref = dirs[d][4]
        v = dirs[d][1][0, rows[d], :]
        last = l - 1 if d == 0 else 0
        g_all = g_scr[d]
        g_l = g_scr[d, last:last + 1, :]
        qi = q[d] * jnp.exp2(g_all).astype(BF16)
        ks = kk[d] * jnp.exp2(g_l - g_all).astype(BF16)
        dec = jnp.exp2(g_l)
        for i, hs in enumerate(heads):
            u = d * H_B + i
            st = st_scr[u]
            o = _dot_nt(qi[:, hs], st.astype(BF16)) + jnp.dot(acc[d][i], v[:, hs], preferred_element_type=F32)
            o_ref[0, rows[d], hs] = o.astype(o_ref.dtype)
            st_scr[u] = st * dec[:, hs] + _dot_tn(v[:, hs], ks[:, hs])


def _hgrn_kernel(qf_ref, vf_ref, kf_ref, lff_ref, qb_ref, vb_ref, kb_ref, lfb_ref,
                 tri_ref, mfull_ref, mhalf_ref, mlocal_ref, of_ref, ob_ref, st_scr, g_scr, *, l, dh):
    @pl.when(pl.program_id(1) == 0)
    def _():
        st_scr[...] = jnp.zeros_like(st_scr)

    dirs = ((qf_ref, vf_ref, kf_ref, lff_ref, of_ref), (qb_ref, vb_ref, kb_ref, lfb_ref, ob_ref))
    n_sub = qf_ref.shape[1] // l
    rows = [(slice(j * l, (j + 1) * l), slice((n_sub - 1 - j) * l, (n_sub - j) * l)) for j in range(n_sub)]
    worst = None
    for j in range(n_sub):
        for d in (0, 1):
            g_scr[j, d] = _cumsum_mm(tri_ref[d], dirs[d][3][0, rows[j][d], :], terms=HGRN_CUMSUM_TERMS)
            span = _hgrn_block_decay(g_scr.at[j, d], l)
            worst = span if worst is None else jnp.maximum(worst, span)
    in_range = worst <= HGRN_LOCAL_MAX_LOG2

    def run(fast):
        for j in range(n_sub):
            _hgrn_chunk(dirs, mfull_ref, mhalf_ref, mlocal_ref, st_scr, g_scr.at[j], l=l, dh=dh, fast=fast,
                        rows=rows[j])

    pl.when(in_range)(functools.partial(run, True))
    pl.when(jnp.logical_not(in_range))(functools.partial(run, False))


def _hgrn(mix, kdec, lf, ctx_len, d_b):
    nb, tb, _ = mix.shape
    l = L_B
    dh = d_b // H_B
    assert dh == LANES and ctx_len % l == 0 and tb % l == 0
    step = HGRN_STEP_CHUNKS * l
    assert ctx_len % step == 0 and tb % step == 0
    nc, ncc = tb // step, ctx_len // step
    bw = functools.partial(_bwd_chunk, n_ctx_chunks=ncc, n_chunks=nc)
    kern = functools.partial(_hgrn_kernel, l=l, dh=dh)
    tri = jnp.asarray(_hgrn_tri(l), BF16)
    mfull, mhalf, mlocal = (jnp.asarray(m, BF16) for m in _hgrn_masks(l))
    oshape = jax.ShapeDtypeStruct((nb, tb, d_b), BF16)
    blk = lambda col, chunk: pl.BlockSpec((1, step, d_b), lambda b, c: (b, chunk(c), col))
    ident = lambda c: c
    const = lambda a: pl.BlockSpec(a.shape, lambda b, c: (0,) * a.ndim)
    return pl.pallas_call(
        kern,
        out_shape=(oshape, oshape),
        grid=(nb, nc),
        in_specs=[blk(MIX_QB, ident), blk(MIX_IB, ident), blk(0, ident), blk(0, ident),
                  blk(MIX_QB, bw), blk(MIX_IB, bw), blk(1, bw), blk(1, bw),
                  const(tri), const(mfull), const(mhalf), const(mlocal)],
        out_specs=(blk(0, ident), blk(0, bw)),
        scratch_shapes=[pltpu.VMEM((2 * H_B, dh, dh), F32),
                        pltpu.VMEM((HGRN_STEP_CHUNKS, 2, l, d_b), F32)],
        compiler_params=_cparams(("arbitrary", "arbitrary")),
        name="hgrn",
    )(mix, mix, kdec, lf, mix, mix, kdec, lf, tri, mfull, mhalf, mlocal)


def _head_rms(x, n_heads):
    dh = x.shape[-1] // n_heads
    outs = []
    for h in range(n_heads):
        xh = x[:, h * dh:(h + 1) * dh]
        outs.append(xh * lax.rsqrt(jnp.mean(xh * xh, axis=-1, keepdims=True) + EPS))
    return jnp.concatenate(outs, axis=-1)


def _merge_kernel(*refs, ctx_len, tm, n_batch, fuse_next):
    (x_ref, haf_ref, hab_ref, hbf_ref, hbb_ref, o_ref, za_ref, zb_ref, ma_ref, mb_ref,
     mod_ref, gha_ref, ghb_ref, wa_ref, wb_ref, wo_ref) = refs[:16]
    b = pl.program_id(0)
    d = x_ref.shape[-1]
    f32 = lambda ref: ref[0].astype(F32)
    h_a = (f32(haf_ref) + f32(hab_ref)) * f32(o_ref)
    y_a = _head_rms(h_a, H_A) * gha_ref[...] * f32(za_ref)
    y_b = _head_rms(f32(hbf_ref) + f32(hbb_ref), H_B) * ghb_ref[...] * f32(zb_ref)
    pa = jnp.dot(y_a.astype(BF16), wa_ref[...], preferred_element_type=F32)
    pb = jnp.dot(y_b.astype(BF16), wb_ref[...], preferred_element_type=F32)
    y = f32(ma_ref) * pa + f32(mb_ref) * pb
    br = jnp.dot(y.astype(BF16), wo_ref[...], preferred_element_type=F32)
    is_ctx = pl.program_id(1) * tm + lax.broadcasted_iota(jnp.int32, (tm, 1), 0) < ctx_len
    gate = jnp.where(is_ctx, mod_ref[pl.ds(n_batch, 1), 2 * d:3 * d], mod_ref[pl.ds(b, 1), 2 * d:3 * d])
    x_new = x_ref[0] + gate * br
    if fuse_next:
        nmod_ref, ngn_ref, nwg_ref, nbg_ref, out_ref, h_ref, g_ref = refs[16:]
        h_ref[0], g_ref[0] = _norm_modulate(x_new, is_ctx, b, nmod_ref, ngn_ref, nwg_ref, nbg_ref, n_batch)
        out_ref[0] = x_new
    else:
        gf_ref, out_ref = refs[16:]
        out_ref[0] = x_new * lax.rsqrt(jnp.mean(x_new * x_new, axis=-1, keepdims=True) + EPS) * gf_ref[...]


def _merge(xall, haf, hab, hbf, hbb, sig, silu, mod_l, gha, ghb, wa, wb, wo, ctx_len, nxt=None, g_final=None):
    nb, tb, d = xall.shape
    tm = MERGE_ROWS
    assert (nxt is None) != (g_final is None) and ctx_len % tm == 0 and tb % tm == 0
    kern = functools.partial(_merge_kernel, ctx_len=ctx_len, tm=tm, n_batch=nb, fuse_next=nxt is not None)
    row = pl.BlockSpec((1, tm, d), lambda b, i: (b, i, 0))
    pcol = lambda col: pl.BlockSpec((1, tm, d), lambda b, i: (b, i, col))
    full = lambda shape: pl.BlockSpec(shape, lambda b, i: (0,) * len(shape))
    in_specs = [row, row, row, row, row,
                pcol(SIG_O), pcol(SILU_ZA), pcol(SILU_ZB), pcol(SIG_MA), pcol(SIG_MB),
                full((8, 3 * d)), full((1, d)), full((1, d)),
                full((d, d)), full((d, d)), full((d, d))]
    args = [xall, haf, hab, hbf, hbb, sig, silu, silu, sig, sig, mod_l, gha, ghb, wa, wb, wo]
    out_shape = jax.ShapeDtypeStruct(xall.shape, F32)
    out_specs = row
    if nxt is not None:
        in_specs += [full((8, 3 * d)), full((1, d)), full((2, d, LANES)), full((1, LANES))]
        args += list(nxt)
        out_shape = (out_shape, jax.ShapeDtypeStruct((nb, tb, d), BF16),
                     jax.ShapeDtypeStruct((nb, tb, LANES), F32))
        out_specs = (row, row, pl.BlockSpec((1, tm, LANES), lambda b, i: (b, i, 0)))
    else:
        in_specs.append(full((1, d)))
        args.append(g_final)
        ctx_tiles = ctx_len // tm
        out_shape = jax.ShapeDtypeStruct((nb, tb - ctx_len, d), F32)
        out_specs = pl.BlockSpec((1, tm, d), lambda b, i: (b, jnp.maximum(i - ctx_tiles, 0), 0))
    return pl.pallas_call(
        kern,
        out_shape=out_shape,
        grid=(nb, tb // tm),
        in_specs=in_specs,
        out_specs=out_specs,
        compiler_params=_cparams(("arbitrary", "arbitrary")),
        name="merge",
    )(*args)


def kernel(x, c, ctx, c_ctx, w_ada, b_ada, g_norm, w_in, b_in, w_conv, b_conv, lb_logits,
           g_head_a, g_head_b, w_a, w_b, w_out, g_final):
    nb, seq, d = x.shape
    ctx_len = ctx.shape[1]
    depth = w_ada.shape[0]
    d_a = g_head_a.shape[-1]
    d_b = g_head_b.shape[-1]
    assert d_a == d and d_b == d and nb + 1 <= 8
    dh_a = d_a // H_A

    g0 = 5 * d
    g1 = g0 + N_GATES
    grp = lambda n: slice(n * d, (n + 1) * d) if n < 5 else slice(g1 + (n - 5) * d, g1 + (n - 4) * d)
    QK0, QK1, V, O, ZA, QB, IB, FF, FB, ZB, MA, MB = range(12)
    order = (V, QB, IB, O, MA, MB, QK0, QK1, ZA, ZB, FF, FB)
    w_main = jnp.concatenate([w_in[:, :, grp(n)] for n in order], axis=-1).astype(BF16)
    b_main = jnp.concatenate([b_in[:, grp(n)] for n in order], axis=-1).astype(F32)
    wg = jnp.pad(w_in[:, :, g0:g1].astype(F32), ((0, 0), (0, 0), (0, LANES - N_GATES)))
    wg_hi = wg.astype(BF16)
    wg_lo = (wg - wg_hi.astype(F32)).astype(BF16)
    wg2 = jnp.stack([wg_hi, wg_lo], axis=1)
    bg = jnp.pad(b_in[:, g0:g1].astype(F32), ((0, 0), (0, LANES - N_GATES)))
    qk_scale = jnp.concatenate([jnp.ones((1, d_a), F32), jnp.full((1, d_a), dh_a ** -0.5, F32)], axis=-1)
    w9 = w_conv.reshape(depth, 9, 2 * d_a).astype(F32)

    cc = jnp.concatenate([c.astype(F32), c_ctx.astype(F32)[None, :],
                          jnp.zeros((8 - nb - 1, d), F32)], axis=0)
    mod = _ada(cc, w_ada.astype(F32), b_ada.astype(F32))
    lbs = _lbs(lb_logits)
    c_sig, c_raw, c_silu, c_dec = (int(n) for n in np.cumsum((N_MIX, N_SIG, N_RAW, N_SILU)))

    xall = jnp.concatenate([ctx.astype(F32), x.astype(F32)], axis=1)
    norm_args = lambda l: (mod[l], g_norm[l][None, :].astype(F32), wg2[l], bg[l][None, :])
    h16, g = _norm_gates(xall, *norm_args(0), ctx_len)
    for l in range(depth):
        bm = b_main[l][None, :]
        raw = _proj(h16, w_main[l], bm, c_raw, N_RAW, "raw")
        mix = _proj(h16, w_main[l], bm, 0, N_MIX, "cast")
        sig = _proj(h16, w_main[l], bm, c_sig, N_SIG, "sigmoid")
        silu = _proj(h16, w_main[l], bm, c_silu, N_SILU, "silu")
        lf, kdec = _proj(h16, w_main[l], bm, c_dec, N_DECAY, "decay", lbs[l][None, :])
        qk = _conv_silu(raw, w9[l], b_conv[l][None, :].astype(F32), qk_scale, ctx_len)
        haf, hab = _mlstm(qk, mix, g, ctx_len, d_a)
        hbf, hbb = _hgrn(mix, kdec, lf, ctx_len, d_b)
        last = l + 1 == depth
        res = _merge(xall, haf, hab, hbf, hbb, sig, silu, mod[l], g_head_a[l][None, :].astype(F32),
                     g_head_b[l][None, :].astype(F32), w_a[l].astype(BF16), w_b[l].astype(BF16),
                     w_out[l].astype(BF16), ctx_len, nxt=None if last else norm_args(l + 1),
                     g_final=g_final[None, :].astype(F32) if last else None)
        xall, h16, g = (res, None, None) if last else res
    return xall.astype(x.dtype)
```

```python
import functools
import math

import numpy as np
import jax
import jax.numpy as jnp
from jax import lax
from jax.experimental import pallas as pl
from jax.experimental.pallas import tpu as pltpu

F32 = jnp.float32
BF16 = jnp.bfloat16

H_A = 4
H_B = 8
GRID_W = 64
EPS = 1e-6
NEG = -1e30
LOG2E = math.log2(math.e)
N_GATES = 4 * H_A
LANES = 128
L_A = 256
L_B = 128
HGRN_COMPACT_MIN = 16
HGRN_LOCAL = 32
HGRN_LOCAL_MAX_LOG2 = 100.0
HGRN_STEP_CHUNKS = 2
CONV_ROWS = 256
CONV_HALO = 128
HGRN_CUMSUM_TERMS = 2
MLSTM_CUMSUM_TERMS = 2
CONV_COLS = 2048
PROJ_ROWS = 256
MERGE_ROWS = 256
V7X_VMEM_LIMIT = 56 * 1024 * 1024
N_MIX, N_SIG, N_RAW, N_SILU, N_DECAY = 3, 3, 2, 2, 2
MIX_V, MIX_QB, MIX_IB = 0, 1, 2
SIG_O, SIG_MA, SIG_MB = 0, 1, 2
SILU_ZA, SILU_ZB = 0, 1


def _cparams(semantics):
    return pltpu.CompilerParams(dimension_semantics=semantics, vmem_limit_bytes=V7X_VMEM_LIMIT)


def _sigmoid(x):
    return 1.0 / (1.0 + jnp.exp(-x))


def _sigmoid_tanh(x):
    return 0.5 * jnp.tanh(0.5 * x) + 0.5


def _split3(x):
    hi = x.astype(BF16)
    r1 = x - hi.astype(F32)
    mid = r1.astype(BF16)
    lo = (r1 - mid.astype(F32)).astype(BF16)
    return hi, mid, lo


def _cumsum_mm(tri, x, terms=3):
    out = None
    for part in _split3(x)[:terms]:
        term = jnp.dot(tri, part, preferred_element_type=F32)
        out = term if out is None else out + term
    return out


def _dot_nt(a, b):
    return lax.dot_general(a, b, (((1,), (1,)), ((), ())), preferred_element_type=F32)


def _dot_tn(a, b):
    return lax.dot_general(a, b, (((0,), (0,)), ((), ())), preferred_element_type=F32)


def _ada_kernel(cc_ref, w_ref, b_ref, out_ref):
    s = cc_ref[...]
    s = s * _sigmoid(s)
    w = w_ref[0]
    s_hi = s.astype(BF16)
    s_lo = (s - s_hi.astype(F32)).astype(BF16)
    w_hi = w.astype(BF16)
    w_lo = (w - w_hi.astype(F32)).astype(BF16)
    d = lambda a, b: jnp.dot(a, b, preferred_element_type=F32)
    out_ref[0] = d(s_hi, w_hi) + d(s_lo, w_hi) + d(s_hi, w_lo) + b_ref[0]


def _ada(cc, w_ada, b_ada):
    depth, d, n3 = w_ada.shape
    tn = 512
    return pl.pallas_call(
        _ada_kernel,
        out_shape=jax.ShapeDtypeStruct((depth, 8, n3), F32),
        grid=(depth, n3 // tn),
        in_specs=[pl.BlockSpec((8, d), lambda l, j: (0, 0)),
                  pl.BlockSpec((1, d, tn), lambda l, j: (l, 0, j)),
                  pl.BlockSpec((1, 1, tn), lambda l, j: (l, 0, j))],
        out_specs=pl.BlockSpec((1, 8, tn), lambda l, j: (l, 0, j)),
        compiler_params=_cparams(("arbitrary", "arbitrary")),
        name="ada",
    )(cc, w_ada, b_ada.reshape(depth, 1, n3))


def _lbs_kernel(lg_ref, out_ref):
    x = lg_ref[...]
    depth = x.shape[0]
    m = jnp.max(x, axis=0, keepdims=True)
    e = jnp.exp(x - m)
    p = e / jnp.sum(e, axis=0, keepdims=True)
    acc = jnp.zeros_like(p[0:1])
    for l in range(depth):
        acc = acc + p[l:l + 1]
        out_ref[l:l + 1, :] = acc - p[0:1]


def _lbs(lb_logits):
    return pl.pallas_call(
        _lbs_kernel, out_shape=jax.ShapeDtypeStruct(lb_logits.shape, F32), name="lbs",
    )(lb_logits.astype(F32))


def _norm_modulate(x, is_ctx, b, mod_ref, gn_ref, wg_ref, bg_ref, n_batch):
    d = x.shape[-1]
    ms = jnp.mean(x * x, axis=-1, keepdims=True)
    y = x * lax.rsqrt(ms + EPS) * gn_ref[...]
    mb = mod_ref[pl.ds(b, 1), :]
    mc = mod_ref[pl.ds(n_batch, 1), :]
    shift = jnp.where(is_ctx, mc[:, 0:d], mb[:, 0:d])
    scale = jnp.where(is_ctx, mc[:, d:2 * d], mb[:, d:2 * d])
    h = y * (1.0 + scale) + shift
    hi = h.astype(BF16)
    lo = (h - hi.astype(F32)).astype(BF16)
    dd = lambda a, bb: jnp.dot(a, bb, preferred_element_type=F32)
    return hi, dd(hi, wg_ref[0]) + dd(lo, wg_ref[0]) + dd(hi, wg_ref[1]) + bg_ref[...]


def _norm_kernel(x_ref, mod_ref, gn_ref, wg_ref, bg_ref, h_ref, g_ref, *, ctx_len, tm, n_batch):
    b = pl.program_id(0)
    row = pl.program_id(1) * tm + lax.broadcasted_iota(jnp.int32, (tm, 1), 0)
    h_ref[0], g_ref[0] = _norm_modulate(x_ref[0], row < ctx_len, b, mod_ref, gn_ref, wg_ref, bg_ref, n_batch)


def _pick_tile(n, candidates):
    for c in candidates:
        if n % c == 0:
            return c
    raise ValueError(f"no tile for {n}")


def _norm_gates(xall, mod_l, gn, wg, bg, ctx_len):
    nb, tb, d = xall.shape
    tm = _pick_tile(tb, (640, 256))
    kern = functools.partial(_norm_kernel, ctx_len=ctx_len, tm=tm, n_batch=nb)
    return pl.pallas_call(
        kern,
        out_shape=(jax.ShapeDtypeStruct((nb, tb, d), BF16),
                   jax.ShapeDtypeStruct((nb, tb, LANES), F32)),
        grid=(nb, tb // tm),
        in_specs=[pl.BlockSpec((1, tm, d), lambda b, i: (b, i, 0)),
                  pl.BlockSpec((8, 3 * d), lambda b, i: (0, 0)),
                  pl.BlockSpec((1, d), lambda b, i: (0, 0)),
                  pl.BlockSpec((2, d, LANES), lambda b, i: (0, 0, 0)),
                  pl.BlockSpec((1, LANES), lambda b, i: (0, 0))],
        out_specs=(pl.BlockSpec((1, tm, d), lambda b, i: (b, i, 0)),
                   pl.BlockSpec((1, tm, LANES), lambda b, i: (b, i, 0))),
        compiler_params=_cparams(("arbitrary", "arbitrary")),
        name="norm_gates",
    )(xall, mod_l, gn, wg, bg)


def _proj_kernel(*refs, kind, rs):
    if kind == "decay":
        h_ref, w_ref, b_ref, lb_ref, lf_ref, k_ref = refs
    else:
        h_ref, w_ref, b_ref, out_ref = refs
    tm = h_ref.shape[1]
    for r0 in range(0, tm, rs):
        rows = slice(r0, r0 + rs)
        acc = jnp.dot(h_ref[0, rows, :], w_ref[...], preferred_element_type=F32) + b_ref[...]
        if kind == "raw":
            out_ref[0, rows, :] = acc
        elif kind == "cast":
            out_ref[0, rows, :] = acc.astype(BF16)
        elif kind == "sigmoid":
            out_ref[0, rows, :] = _sigmoid_tanh(acc).astype(BF16)
        elif kind == "silu":
            out_ref[0, rows, :] = (acc * _sigmoid_tanh(acc)).astype(BF16)
        else:
            lbv = lb_ref[...]
            t = jnp.exp2(jnp.abs(acc) * (-LOG2E))
            pos = acc >= 0.0
            rc = 1.0 / (1.0 + t)
            f = jnp.where(pos, 1.0 + lbv * t, t + lbv) * rc
            lf_ref[0, rows, :] = jnp.log(f) * LOG2E
            k_ref[0, rows, :] = ((1.0 - lbv) * jnp.where(pos, t, 1.0) * rc).astype(BF16)


def _proj(h16, w_main, b_main, col0, n_groups, kind, lb=None):
    nb, tb, d = h16.shape
    tm = _pick_tile(tb, (1280, 640, 256))
    tn = n_groups * d
    off = col0 // n_groups
    assert col0 % n_groups == 0
    kern = functools.partial(_proj_kernel, kind=kind, rs=PROJ_ROWS)
    in_specs = [pl.BlockSpec((1, tm, d), lambda b, i: (b, i, 0)),
                pl.BlockSpec((d, tn), lambda b, i: (0, off)),
                pl.BlockSpec((1, tn), lambda b, i: (0, off))]
    args = [h16, w_main, b_main]
    oblk = pl.BlockSpec((1, tm, tn), lambda b, i: (b, i, 0))
    shp = lambda dt: jax.ShapeDtypeStruct((nb, tb, tn), dt)
    if kind == "decay":
        in_specs.append(pl.BlockSpec((1, tn), lambda b, i: (0, 0)))
        args.append(jnp.concatenate([lb] * n_groups, axis=-1))
        out_shape, out_specs = (shp(F32), shp(BF16)), (oblk, oblk)
    else:
        out_shape, out_specs = shp(F32 if kind == "raw" else BF16), oblk
    return pl.pallas_call(
        kern,
        out_shape=out_shape,
        grid=(nb, tb // tm),
        in_specs=in_specs,
        out_specs=out_specs,
        compiler_params=_cparams(("arbitrary", "arbitrary")),
        name="proj_" + kind,
    )(*args)


def _conv_kernel(prev_ref, cur_ref, next_ref, w_ref, bc_ref, sc_ref, out_ref, *, n_tiles, tr, halo):
    i = pl.program_id(1)
    is_ctx = i == 0
    prev_ok = i >= 2
    next_ok = jnp.logical_and(i >= 1, i < n_tiles - 1)
    margin = GRID_W + 8
    full = jnp.concatenate([jnp.where(prev_ok, prev_ref[0, halo - margin:halo, :], 0.0), cur_ref[0],
                            jnp.where(next_ok, next_ref[0, 0:margin, :], 0.0)], axis=0)
    w = w_ref[...]
    sums = []
    for dc in (-1, 0, 1):
        acc = None
        for dr in (-1, 0, 1):
            tap = (dr + 1) * 3 + (dc + 1)
            wt = w[tap:tap + 1, :]
            if dr != 0:
                wt = jnp.where(is_ctx, 0.0, wt)
            start = margin + GRID_W * dr - 8
            term = wt * full[start:start + tr + 16, :]
            acc = term if acc is None else acc + term
        sums.append(acc)
    col = lax.broadcasted_iota(jnp.int32, (tr, 1), 0) % GRID_W
    rows = tr + 16
    left = pltpu.roll(sums[0], 1, axis=0)[8:8 + tr, :]
    right = pltpu.roll(sums[2], rows - 1, axis=0)[8:8 + tr, :]
    y = (sums[1][8:8 + tr, :] + jnp.where(jnp.logical_or(is_ctx, col != 0), left, 0.0)
         + jnp.where(jnp.logical_or(is_ctx, col != GRID_W - 1), right, 0.0) + bc_ref[...])
    out_ref[0] = (y * _sigmoid(y) * sc_ref[...]).astype(out_ref.dtype)


def _conv_silu(raw, w9, bconv, scale, ctx_len):
    nb, tb, n_qk = raw.shape
    tr, halo, tc = CONV_ROWS, CONV_HALO, CONV_COLS
    assert ctx_len == tr and tb % tr == 0 and tr % GRID_W == 0 and halo % GRID_W == 0
    n_tiles = tb // tr
    per = tr // halo
    n_halo = tb // halo
    kern = functools.partial(_conv_kernel, n_tiles=n_tiles, tr=tr, halo=halo)
    return pl.pallas_call(
        kern,
        out_shape=jax.ShapeDtypeStruct((nb, tb, n_qk), BF16),
        grid=(nb, n_tiles, n_qk // tc),
        in_specs=[pl.BlockSpec((1, halo, tc), lambda b, i, j: (b, jnp.maximum(i * per - 1, 0), j)),
                  pl.BlockSpec((1, tr, tc), lambda b, i, j: (b, i, j)),
                  pl.BlockSpec((1, halo, tc), lambda b, i, j: (b, jnp.minimum((i + 1) * per, n_halo - 1), j)),
                  pl.BlockSpec((9, tc), lambda b, i, j: (0, j)),
                  pl.BlockSpec((1, tc), lambda b, i, j: (0, j)),
                  pl.BlockSpec((1, tc), lambda b, i, j: (0, j))],
        out_specs=pl.BlockSpec((1, tr, tc), lambda b, i, j: (b, i, j)),
        compiler_params=_cparams(("arbitrary", "arbitrary", "arbitrary")),
        name="conv_silu",
    )(raw, raw, raw, w9, bconv, scale)


def _mlstm_chunk(qkf_ref, vf_ref, gf_ref, qkb_ref, vb_ref, gb_ref, hf_ref, hb_ref,
                 c_scr, n_scr, m_scr, *, la, dh):
    r = lax.broadcasted_iota(jnp.int32, (la, la), 0)
    s = lax.broadcasted_iota(jnp.int32, (la, la), 1)
    ones_blk = jnp.ones((la, LANES), BF16)
    wide = lambda x, n: jnp.concatenate([x] * (n // LANES), axis=1)
    dirs = ((qkf_ref, vf_ref, gf_ref, hf_ref), (qkb_ref, vb_ref, gb_ref, hb_ref))
    for d, (qk_ref, v_ref, g_ref, h_ref) in enumerate(dirs):
        seen = (s <= r) if d == 0 else (s >= r)
        tri = jnp.where(seen, 1.0, 0.0).astype(BF16)
        g = g_ref[0]
        lsig = jnp.minimum(g, 0.0) - jnp.log(1.0 + jnp.exp(-jnp.abs(g)))
        bc = _cumsum_mm(tri, lsig, terms=MLSTM_CUMSUM_TERMS)
        b2 = pltpu.roll(bc, LANES - H_A, axis=1) * LOG2E
        r2 = g * LOG2E - b2
        r2_t = r2.T
        last = la - 1 if d == 0 else 0
        b2_l = b2[last:last + 1, :]
        m2 = m_scr[d:d + 1, :]
        ws2 = r2 + b2_l
        m2_new = jnp.maximum(b2_l + m2, jnp.max(ws2, axis=0, keepdims=True))
        decay = jnp.exp2(b2_l + m2 - m2_new)
        wse = jnp.exp2(ws2 - m2_new)
        m_scr[d:d + 1, :] = m2_new
        cols = [2 * d * H_A + h for h in range(H_A)]
        q = [qk_ref[0, :, h * dh:(h + 1) * dh] for h in range(H_A)]
        k = [qk_ref[0, :, (H_A + h) * dh:(H_A + h + 1) * dh] for h in range(H_A)]
        v = [v_ref[0, :, h * dh:(h + 1) * dh] for h in range(H_A)]
        col = lambda x, ci: jnp.broadcast_to(x[:, ci:ci + 1], (la, LANES))
        rm = [jnp.where(seen, r2_t[ci:ci + 1, :], NEG) for ci in cols]
        mx = [jnp.broadcast_to(jnp.maximum(m2[:, ci:ci + 1], jnp.max(rm_h, axis=-1, keepdims=True)),
                               (la, LANES)) for ci, rm_h in zip(cols, rm)]
        s_mat = [_dot_nt(q[h], k[h]) for h in range(H_A)]
        for h, ci in enumerate(cols):
            u = d * H_A + h
            w16 = (s_mat[h] * jnp.exp2(rm[h] - wide(mx[h], la))).astype(BF16)
            a_in = jnp.exp2(m2[:, ci:ci + 1] - mx[h])
            em = jnp.exp2(-(col(b2, ci) + mx[h]))
            c_st = c_scr[u]
            n_st = n_scr[u]
            qa = q[h] * wide(a_in.astype(BF16), dh)
            num = jnp.dot(jnp.concatenate([w16, qa], axis=1),
                          jnp.concatenate([v[h], c_st.astype(BF16)], axis=0), preferred_element_type=F32)
            qn = _dot_nt(q[h], jnp.broadcast_to(n_st, (LANES, dh)).astype(BF16))
            den = a_in * qn + jnp.dot(w16, ones_blk, preferred_element_type=F32)
            rcp = 1.0 / jnp.maximum(jnp.abs(den), em)
            h_ref[0, :, h * dh:(h + 1) * dh] = (num * wide(rcp, dh)).astype(h_ref.dtype)
            wse_u = col(wse, ci).astype(BF16)
            dec = decay[:, ci:ci + 1]
            c_scr[u] = dec * c_st + _dot_tn(k[h] * wide(wse_u, dh), v[h])
            n_scr[u] = dec * n_st + _dot_tn(wse_u, k[h])[0:1, :]


def _bwd_chunk(c, n_ctx_chunks, n_chunks):
    return jnp.where(c < n_ctx_chunks, n_ctx_chunks - 1 - c, n_chunks - 1 + n_ctx_chunks - c)


def _hgrn_levels(l):
    return [1 << j for j in range(int(np.log2(l)))]


def _hgrn_q_rows(l, h, d):
    off = h if d == 0 else 0
    return [(k * 2 * h + off, k * 2 * h + off + h) for k in range(l // (2 * h))]


def _hgrn_masks(l):
    r = np.arange(l)[:, None]
    s = np.arange(l)[None, :]
    full, half = [[], []], [[], []]
    for d in (0, 1):
        full[d].append(r == s)
        for h in _hgrn_levels(l):
            same = (r // (2 * h)) == (s // (2 * h))
            m = same & ((r % (2 * h)) >= h) & ((s % (2 * h)) < h)
            m = m if d == 0 else m.T
            if h >= HGRN_COMPACT_MIN:
                half[d].append(np.concatenate([m[a:b] for a, b in _hgrn_q_rows(l, h, d)]))
            else:
                full[d].append(m)
    local = [(r // HGRN_LOCAL == s // HGRN_LOCAL) & (s <= r), (r // HGRN_LOCAL == s // HGRN_LOCAL) & (s >= r)]
    return np.array(full, np.float32), np.array(half, np.float32), np.array(local, np.float32)


def _hgrn_tri(l):
    r = np.arange(l)[:, None]
    s = np.arange(l)[None, :]
    return np.stack([(s <= r), (s >= r)]).astype(np.float32)


def _hgrn_split_rows(g_scr, l, h, d):
    assert h in (2, 4)
    off = h - 1 if d == 0 else h
    n = g_scr.shape[-1]
    sub = lax.broadcasted_iota(jnp.int32, (8, n), 0)
    pieces = []
    for grp in range(l // 8):
        lo = jnp.broadcast_to(g_scr[grp * 8 + off:grp * 8 + off + 1, :], (8, n))
        if h == 4:
            pieces.append(lo)
        else:
            hi = jnp.broadcast_to(g_scr[grp * 8 + 4 + off:grp * 8 + 4 + off + 1, :], (8, n))
            pieces.append(jnp.where(sub < 4, lo, hi))
    return jnp.concatenate(pieces, axis=0)


def _hgrn_block_decay(g, l):
    spans = [jnp.abs(g[b0:b0 + 1, :] - g[b0 + HGRN_LOCAL - 1:b0 + HGRN_LOCAL, :]) for b0 in range(0, l, HGRN_LOCAL)]
    return jnp.max(jnp.concatenate(spans, axis=0))


def _hgrn_chunk(dirs, mfull_ref, mhalf_ref, mlocal_ref, st_scr, g_scr, *, l, dh, fast, rows):
    heads = [slice(h * dh, (h + 1) * dh) for h in range(H_B)]
    q = [dirs[d][0][0, rows[d], :] for d in (0, 1)]
    kk = [dirs[d][2][0, rows[d], :] for d in (0, 1)]

    def scores(qs, ks, mask):
        return [_dot_nt(qs[:, hs], ks[:, hs]).astype(BF16) * mask for hs in heads]

    if fast:
        acc = []
        for d in (0, 1):
            g = g_scr.at[d]
            parts = []
            for b0 in range(0, l, HGRN_LOCAL):
                base = b0 if d == 0 else b0 + HGRN_LOCAL - 1
                parts.append(g[b0:b0 + HGRN_LOCAL, :] - g[base:base + 1, :])
            x = jnp.concatenate(parts, axis=0)
            acc.append(scores(q[d] * jnp.exp2(x).astype(BF16), kk[d] * jnp.exp2(-x).astype(BF16), mlocal_ref[d]))
        levels = [hl for hl in _hgrn_levels(l) if hl >= HGRN_LOCAL]
    else:
        acc = [scores(q[d], kk[d], mfull_ref[d, 0]) for d in (0, 1)]
        levels = _hgrn_levels(l)
    all_levels = _hgrn_levels(l)
    for hl in levels:
        n_full = 1 + all_levels.index(hl)
        n_half = all_levels.index(hl) - all_levels.index(HGRN_COMPACT_MIN)
        for d in (0, 1):
            g = g_scr.at[d]
            m_off = hl - 1 if d == 0 else hl
            q_rows = _hgrn_q_rows(l, hl, d)
            k_rows = _hgrn_q_rows(l, hl, 1 - d)
            if hl < HGRN_COMPACT_MIN:
                if hl == 1:
                    qs, ks = q[d] * jnp.exp2(dirs[d][3][0, rows[d], :]).astype(BF16), kk[d]
                else:
                    if hl < 8:
                        x = -jnp.abs(g[...] - _hgrn_split_rows(g, l, hl, d))
                    else:
                        parts = {}
                        for (q0, q1), (k0, k1) in zip(q_rows, k_rows):
                            gm = g[min(q0, k0) + m_off:min(q0, k0) + m_off + 1, :]
                            parts[q0] = g[q0:q1, :] - gm
                            parts[k0] = gm - g[k0:k1, :]
                        x = jnp.concatenate([parts[r0] for r0 in sorted(parts)], axis=0)
                    e = jnp.exp2(x).astype(BF16)
                    qs, ks = q[d] * e, kk[d] * e
                acc[d] = [a + t for a, t in zip(acc[d], scores(qs, ks, mfull_ref[d, n_full]))]
            else:
                xq, xk = [], []
                for (q0, q1), (k0, k1) in zip(q_rows, k_rows):
                    gm = g[min(q0, k0) + m_off:min(q0, k0) + m_off + 1, :]
                    xq.append(g[q0:q1, :] - gm)
                    xk.append(gm - g[k0:k1, :])
                eq = jnp.exp2(jnp.concatenate(xq, axis=0)).astype(BF16)
                ek = jnp.exp2(jnp.concatenate(xk, axis=0)).astype(BF16)
                qc = jnp.concatenate([q[d][q0:q1] for q0, q1 in q_rows], axis=0) * eq
                kparts = {q0: kk[d][q0:q1] for q0, q1 in q_rows}
                for n, (k0, k1) in enumerate(k_rows):
                    kparts[k0] = kk[d][k0:k1] * ek[n * hl:(n + 1) * hl]
                kt = jnp.concatenate([kparts[r0] for r0 in sorted(kparts)], axis=0)
                terms = scores(qc, kt, mhalf_ref[d, n_half])
                for i in range(H_B):
                    pieces = {k0: acc[d][i][k0:k1] for k0, k1 in k_rows}
                    for n, (q0, q1) in enumerate(q_rows):
                        pieces[q0] = acc[d][i][q0:q1] + terms[i][n * hl:(n + 1) * hl]
                    acc[d][i] = jnp.concatenate([pieces[r0] for r0 in sorted(pieces)], axis=0)
    for d in (0, 1):
        o_ref = dirs[d][4]
        v = dirs[d][1][0, rows[d], :]
        last = l - 1 if d == 0 else 0
        g_all = g_scr[d]
        g_l = g_scr[d, last:last + 1, :]
        qi = q[d] * jnp.exp2(g_all).astype(BF16)
        ks = kk[d] * jnp.exp2(g_l - g_all).astype(BF16)
        dec = jnp.exp2(g_l)
        for i, hs in enumerate(heads):
            u = d * H_B + i
            st = st_scr[u]
            o = _dot_nt(qi[:, hs], st.astype(BF16)) + jnp.dot(acc[d][i], v[:, hs], preferred_element_type=F32)
            o_ref[0, rows[d], hs] = o.astype(o_ref.dtype)
            st_scr[u] = st * dec[:, hs] + _dot_tn(v[:, hs], ks[:, hs])


def _hgrn_kernel(qf_ref, vf_ref, kf_ref, lff_ref, qb_ref, vb_ref, kb_ref, lfb_ref,
                 tri_ref, mfull_ref, mhalf_ref, mlocal_ref, of_ref, ob_ref, st_scr, g_scr, *, l, dh):
    @pl.when(pl.program_id(1) == 0)
    def _():
        st_scr[...] = jnp.zeros_like(st_scr)

    dirs = ((qf_ref, vf_ref, kf_ref, lff_ref, of_ref), (qb_ref, vb_ref, kb_ref, lfb_ref, ob_ref))
    n_sub = qf_ref.shape[1] // l
    rows = [(slice(j * l, (j + 1) * l), slice((n_sub - 1 - j) * l, (n_sub - j) * l)) for j in range(n_sub)]
    worst = None
    for j in range(n_sub):
        for d in (0, 1):
            g_scr[j, d] = _cumsum_mm(tri_ref[d], dirs[d][3][0, rows[j][d], :], terms=HGRN_CUMSUM_TERMS)
            span = _hgrn_block_decay(g_scr.at[j, d], l)
            worst = span if worst is None else jnp.maximum(worst, span)
    in_range = worst <= HGRN_LOCAL_MAX_LOG2

    def run(fast):
        for j in range(n_sub):
            _hgrn_chunk(dirs, mfull_ref, mhalf_ref, mlocal_ref, st_scr, g_scr.at[j], l=l, dh=dh, fast=fast,
                        rows=rows[j])

    pl.when(in_range)(functools.partial(run, True))
    pl.when(jnp.logical_not(in_range))(functools.partial(run, False))


def _scan_kernel(*refs, la, dh_a, lb, dh_b):
    m_in, h_in = refs[0:6], refs[6:18]
    ha_f, ha_b, hb_f, hb_b = refs[18:22]
    c_scr, n_scr, m_scr, st_scr, g_scr = refs[22:]

    @pl.when(pl.program_id(1) == 0)
    def _():
        for scr in (c_scr, n_scr, m_scr):
            scr[...] = jnp.zeros_like(scr)

    _mlstm_chunk(*m_in, ha_f, ha_b, c_scr, n_scr, m_scr, la=la, dh=dh_a)
    _hgrn_kernel(*h_in, hb_f, hb_b, st_scr, g_scr, l=lb, dh=dh_b)


def _scans(qk, mix, g, kdec, lf, ctx_len, d_a, d_b):
    nb, tb, _ = qk.shape
    la, lb = L_A, L_B
    dh_a, dh_b = d_a // H_A, d_b // H_B
    assert dh_b == LANES and la == HGRN_STEP_CHUNKS * lb and ctx_len % la == 0 and tb % la == 0
    nc, ncc = tb // la, ctx_len // la
    bw = functools.partial(_bwd_chunk, n_ctx_chunks=ncc, n_chunks=nc)
    kern = functools.partial(_scan_kernel, la=la, dh_a=dh_a, lb=lb, dh_b=dh_b)
    tri = jnp.asarray(_hgrn_tri(lb), BF16)
    mfull, mhalf, mlocal = (jnp.asarray(m, BF16) for m in _hgrn_masks(lb))
    ident = lambda c: c
    blk = lambda w, col, chunk: pl.BlockSpec((1, la, w), lambda b, c: (b, chunk(c), col))
    const = lambda a: pl.BlockSpec(a.shape, lambda b, c: (0,) * a.ndim)
    hshape = jax.ShapeDtypeStruct((nb, tb, d_a), BF16)
    oshape = jax.ShapeDtypeStruct((nb, tb, d_b), BF16)
    return pl.pallas_call(
        kern,
        out_shape=(hshape, hshape, oshape, oshape),
        grid=(nb, nc),
        in_specs=[blk(2 * d_a, 0, ident), blk(d_a, MIX_V, ident), blk(LANES, 0, ident),
                  blk(2 * d_a, 0, bw), blk(d_a, MIX_V, bw), blk(LANES, 0, bw),
                  blk(d_b, MIX_QB, ident), blk(d_b, MIX_IB, ident), blk(d_b, 0, ident), blk(d_b, 0, ident),
                  blk(d_b, MIX_QB, bw), blk(d_b, MIX_IB, bw), blk(d_b, 1, bw), blk(d_b, 1, bw),
                  const(tri), const(mfull), const(mhalf), const(mlocal)],
        out_specs=(blk(d_a, 0, ident), blk(d_a, 0, bw), blk(d_b, 0, ident), blk(d_b, 0, bw)),
        scratch_shapes=[pltpu.VMEM((2 * H_A, dh_a, dh_a), F32),
                        pltpu.VMEM((2 * H_A, 1, dh_a), F32),
                        pltpu.VMEM((8, LANES), F32),
                        pltpu.VMEM((2 * H_B, dh_b, dh_b), F32),
                        pltpu.VMEM((HGRN_STEP_CHUNKS, 2, lb, d_b), F32)],
        compiler_params=_cparams(("arbitrary", "arbitrary")),
        name="scans",
    )(qk, mix, g, qk, mix, g, mix, mix, kdec, lf, mix, mix, kdec, lf, tri, mfull, mhalf, mlocal)


def _head_rms(x, n_heads):
    dh = x.shape[-1] // n_heads
    outs = []
    for h in range(n_heads):
        xh = x[:, h * dh:(h + 1) * dh]
        outs.append(xh * lax.rsqrt(jnp.mean(xh * xh, axis=-1, keepdims=True) + EPS))
    return jnp.concatenate(outs, axis=-1)


def _merge_kernel(*refs, ctx_len, tm, n_batch, fuse_next):
    (x_ref, haf_ref, hab_ref, hbf_ref, hbb_ref, o_ref, za_ref, zb_ref, ma_ref, mb_ref,
     mod_ref, gha_ref, ghb_ref, wa_ref, wb_ref, wo_ref) = refs[:16]
    b = pl.program_id(0)
    d = x_ref.shape[-1]
    f32 = lambda ref: ref[0].astype(F32)
    h_a = (f32(haf_ref) + f32(hab_ref)) * f32(o_ref)
    y_a = _head_rms(h_a, H_A) * gha_ref[...] * f32(za_ref)
    y_b = _head_rms(f32(hbf_ref) + f32(hbb_ref), H_B) * ghb_ref[...] * f32(zb_ref)
    pa = jnp.dot(y_a.astype(BF16), wa_ref[...], preferred_element_type=F32)
    pb = jnp.dot(y_b.astype(BF16), wb_ref[...], preferred_element_type=F32)
    y = f32(ma_ref) * pa + f32(mb_ref) * pb
    br = jnp.dot(y.astype(BF16), wo_ref[...], preferred_element_type=F32)
    is_ctx = pl.program_id(1) * tm + lax.broadcasted_iota(jnp.int32, (tm, 1), 0) < ctx_len
    gate = jnp.where(is_ctx, mod_ref[pl.ds(n_batch, 1), 2 * d:3 * d], mod_ref[pl.ds(b, 1), 2 * d:3 * d])
    x_new = x_ref[0] + gate * br
    if fuse_next:
        nmod_ref, ngn_ref, nwg_ref, nbg_ref, out_ref, h_ref, g_ref = refs[16:]
        h_ref[0], g_ref[0] = _norm_modulate(x_new, is_ctx, b, nmod_ref, ngn_ref, nwg_ref, nbg_ref, n_batch)
        out_ref[0] = x_new
    else:
        gf_ref, out_ref = refs[16:]
        out_ref[0] = x_new * lax.rsqrt(jnp.mean(x_new * x_new, axis=-1, keepdims=True) + EPS) * gf_ref[...]


def _merge(xall, haf, hab, hbf, hbb, sig, silu, mod_l, gha, ghb, wa, wb, wo, ctx_len, nxt=None, g_final=None):
    nb, tb, d = xall.shape
    tm = MERGE_ROWS
    assert (nxt is None) != (g_final is None) and ctx_len % tm == 0 and tb % tm == 0
    kern = functools.partial(_merge_kernel, ctx_len=ctx_len, tm=tm, n_batch=nb, fuse_next=nxt is not None)
    row = pl.BlockSpec((1, tm, d), lambda b, i: (b, i, 0))
    pcol = lambda col: pl.BlockSpec((1, tm, d), lambda b, i: (b, i, col))
    full = lambda shape: pl.BlockSpec(shape, lambda b, i: (0,) * len(shape))
    in_specs = [row, row, row, row, row,
                pcol(SIG_O), pcol(SILU_ZA), pcol(SILU_ZB), pcol(SIG_MA), pcol(SIG_MB),
                full((8, 3 * d)), full((1, d)), full((1, d)),
                full((d, d)), full((d, d)), full((d, d))]
    args = [xall, haf, hab, hbf, hbb, sig, silu, silu, sig, sig, mod_l, gha, ghb, wa, wb, wo]
    out_shape = jax.ShapeDtypeStruct(xall.shape, F32)
    out_specs = row
    if nxt is not None:
        in_specs += [full((8, 3 * d)), full((1, d)), full((2, d, LANES)), full((1, LANES))]
        args += list(nxt)
        out_shape = (out_shape, jax.ShapeDtypeStruct((nb, tb, d), BF16),
                     jax.ShapeDtypeStruct((nb, tb, LANES), F32))
        out_specs = (row, row, pl.BlockSpec((1, tm, LANES), lambda b, i: (b, i, 0)))
    else:
        in_specs.append(full((1, d)))
        args.append(g_final)
        ctx_tiles = ctx_len // tm
        out_shape = jax.ShapeDtypeStruct((nb, tb - ctx_len, d), F32)
        out_specs = pl.BlockSpec((1, tm, d), lambda b, i: (b, jnp.maximum(i - ctx_tiles, 0), 0))
    return pl.pallas_call(
        kern,
        out_shape=out_shape,
        grid=(nb, tb // tm),
        in_specs=in_specs,
        out_specs=out_specs,
        compiler_params=_cparams(("arbitrary", "arbitrary")),
        name="merge",
    )(*args)


def kernel(x, c, ctx, c_ctx, w_ada, b_ada, g_norm, w_in, b_in, w_conv, b_conv, lb_logits,
           g_head_a, g_head_b, w_a, w_b, w_out, g_final):
    nb, seq, d = x.shape
    ctx_len = ctx.shape[1]
    depth = w_ada.shape[0]
    d_a = g_head_a.shape[-1]
    d_b = g_head_b.shape[-1]
    assert d_a == d and d_b == d and nb + 1 <= 8
    dh_a = d_a // H_A

    g0 = 5 * d
    g1 = g0 + N_GATES
    grp = lambda n: slice(n * d, (n + 1) * d) if n < 5 else slice(g1 + (n - 5) * d, g1 + (n - 4) * d)
    QK0, QK1, V, O, ZA, QB, IB, FF, FB, ZB, MA, MB = range(12)
    order = (V, QB, IB, O, MA, MB, QK0, QK1, ZA, ZB, FF, FB)
    w16 = w_in.astype(BF16)
    w_main = jnp.concatenate([w16[:, :, grp(n)] for n in order], axis=-1)
    b_main = jnp.concatenate([b_in[:, grp(n)] for n in order], axis=-1).astype(F32)
    wg = jnp.pad(w_in[:, :, g0:g1].astype(F32), ((0, 0), (0, 0), (0, LANES - N_GATES)))
    wg_hi = wg.astype(BF16)
    wg_lo = (wg - wg_hi.astype(F32)).astype(BF16)
    wg2 = jnp.stack([wg_hi, wg_lo], axis=1)
    bg = jnp.pad(b_in[:, g0:g1].astype(F32), ((0, 0), (0, LANES - N_GATES)))
    qk_scale = jnp.concatenate([jnp.ones((1, d_a), F32), jnp.full((1, d_a), dh_a ** -0.5, F32)], axis=-1)
    w9 = w_conv.reshape(depth, 9, 2 * d_a).astype(F32)

    cc = jnp.concatenate([c.astype(F32), c_ctx.astype(F32)[None, :],
                          jnp.zeros((8 - nb - 1, d), F32)], axis=0)
    mod = _ada(cc, w_ada.astype(F32), b_ada.astype(F32))
    lbs = _lbs(lb_logits)
    c_sig, c_raw, c_silu, c_dec = (int(n) for n in np.cumsum((N_MIX, N_SIG, N_RAW, N_SILU)))

    xall = jnp.concatenate([ctx.astype(F32), x.astype(F32)], axis=1)
    norm_args = lambda l: (mod[l], g_norm[l][None, :].astype(F32), wg2[l], bg[l][None, :])
    h16, g = _norm_gates(xall, *norm_args(0), ctx_len)
    for l in range(depth):
        bm = b_main[l][None, :]
        raw = _proj(h16, w_main[l], bm, c_raw, N_RAW, "raw")
        mix = _proj(h16, w_main[l], bm, 0, N_MIX, "cast")
        sig = _proj(h16, w_main[l], bm, c_sig, N_SIG, "sigmoid")
        silu = _proj(h16, w_main[l], bm, c_silu, N_SILU, "silu")
        lf, kdec = _proj(h16, w_main[l], bm, c_dec, N_DECAY, "decay", lbs[l][None, :])
        qk = _conv_silu(raw, w9[l], b_conv[l][None, :].astype(F32), qk_scale, ctx_len)
        haf, hab, hbf, hbb = _scans(qk, mix, g, kdec, lf, ctx_len, d_a, d_b)
        last = l + 1 == depth
        res = _merge(xall, haf, hab, hbf, hbb, sig, silu, mod[l], g_head_a[l][None, :].astype(F32),
                     g_head_b[l][None, :].astype(F32), w_a[l].astype(BF16), w_b[l].astype(BF16),
                     w_out[l].astype(BF16), ctx_len, nxt=None if last else norm_args(l + 1),
                     g_final=g_final[None, :].astype(F32) if last else None)
        xall, h16, g = (res, None, None) if last else res
    return xall.astype(x.dtype)
```

```python
import functools
import math

import numpy as np
import jax
import jax.numpy as jnp
from jax import lax
from jax.experimental import pallas as pl
from jax.experimental.pallas import tpu as pltpu

F32 = jnp.float32
BF16 = jnp.bfloat16

H_A = 4
H_B = 8
GRID_W = 64
EPS = 1e-6
NEG = -1e30
LOG2E = math.log2(math.e)
N_GATES = 4 * H_A
LANES = 128
L_A = 256
L_B = 128
HGRN_COMPACT_MIN = 16
HGRN_LOCAL = 32
HGRN_LOCAL_MAX_LOG2 = 100.0
HGRN_STEP_CHUNKS = 2
CONV_ROWS = 256
CONV_HALO = 128
HGRN_CUMSUM_TERMS = 2
MLSTM_CUMSUM_TERMS = 2
CONV_COLS = 2048
PROJ_ROWS = 256
MERGE_ROWS = 256
V7X_VMEM_LIMIT = 56 * 1024 * 1024
N_MIX, N_SIG, N_RAW, N_SILU, N_DECAY = 3, 3, 2, 2, 2
MIX_V, MIX_QB, MIX_IB = 0, 1, 2
SIG_O, SIG_MA, SIG_MB = 0, 1, 2
SILU_ZA, SILU_ZB = 0, 1


def _cparams(semantics):
    return pltpu.CompilerParams(dimension_semantics=semantics, vmem_limit_bytes=V7X_VMEM_LIMIT)


def _sigmoid(x):
    return 1.0 / (1.0 + jnp.exp(-x))


def _sigmoid_tanh(x):
    return 0.5 * jnp.tanh(0.5 * x) + 0.5


def _split3(x):
    hi = x.astype(BF16)
    r1 = x - hi.astype(F32)
    mid = r1.astype(BF16)
    lo = (r1 - mid.astype(F32)).astype(BF16)
    return hi, mid, lo


def _cumsum_mm(tri, x, terms=3):
    out = None
    for part in _split3(x)[:terms]:
        term = jnp.dot(tri, part, preferred_element_type=F32)
        out = term if out is None else out + term
    return out


def _dot_nt(a, b):
    return lax.dot_general(a, b, (((1,), (1,)), ((), ())), preferred_element_type=F32)


def _dot_tn(a, b):
    return lax.dot_general(a, b, (((0,), (0,)), ((), ())), preferred_element_type=F32)


def _ada_kernel(cc_ref, w_ref, b_ref, out_ref):
    s = cc_ref[...]
    s = s * _sigmoid(s)
    w = w_ref[0]
    s_hi = s.astype(BF16)
    s_lo = (s - s_hi.astype(F32)).astype(BF16)
    w_hi = w.astype(BF16)
    w_lo = (w - w_hi.astype(F32)).astype(BF16)
    d = lambda a, b: jnp.dot(a, b, preferred_element_type=F32)
    out_ref[0] = d(s_hi, w_hi) + d(s_lo, w_hi) + d(s_hi, w_lo) + b_ref[0]


def _ada(cc, w_ada, b_ada):
    depth, d, n3 = w_ada.shape
    tn = 512
    return pl.pallas_call(
        _ada_kernel,
        out_shape=jax.ShapeDtypeStruct((depth, 8, n3), F32),
        grid=(depth, n3 // tn),
        in_specs=[pl.BlockSpec((8, d), lambda l, j: (0, 0)),
                  pl.BlockSpec((1, d, tn), lambda l, j: (l, 0, j)),
                  pl.BlockSpec((1, 1, tn), lambda l, j: (l, 0, j))],
        out_specs=pl.BlockSpec((1, 8, tn), lambda l, j: (l, 0, j)),
        compiler_params=_cparams(("arbitrary", "arbitrary")),
        name="ada",
    )(cc, w_ada, b_ada.reshape(depth, 1, n3))


def _lbs_kernel(lg_ref, out_ref):
    x = lg_ref[...]
    depth = x.shape[0]
    m = jnp.max(x, axis=0, keepdims=True)
    e = jnp.exp(x - m)
    p = e / jnp.sum(e, axis=0, keepdims=True)
    acc = jnp.zeros_like(p[0:1])
    for l in range(depth):
        acc = acc + p[l:l + 1]
        out_ref[l:l + 1, :] = acc - p[0:1]


def _lbs(lb_logits):
    return pl.pallas_call(
        _lbs_kernel, out_shape=jax.ShapeDtypeStruct(lb_logits.shape, F32), name="lbs",
    )(lb_logits.astype(F32))


def _norm_modulate(x, is_ctx, b, mod_ref, gn_ref, wg_ref, bg_ref, n_batch):
    d = x.shape[-1]
    ms = jnp.mean(x * x, axis=-1, keepdims=True)
    y = x * lax.rsqrt(ms + EPS) * gn_ref[...]
    m = mod_ref[pl.ds(jnp.where(is_ctx, n_batch, b), 1), :]
    shift, scale = m[:, 0:d], m[:, d:2 * d]
    h = y * (1.0 + scale) + shift
    hi = h.astype(BF16)
    lo = (h - hi.astype(F32)).astype(BF16)
    dd = lambda a, bb: jnp.dot(a, bb, preferred_element_type=F32)
    return hi, dd(hi, wg_ref[0]) + dd(lo, wg_ref[0]) + dd(hi, wg_ref[1]) + bg_ref[...]


def _norm_kernel(x_ref, mod_ref, gn_ref, wg_ref, bg_ref, h_ref, g_ref, *, ctx_len, tm, n_batch):
    b = pl.program_id(0)
    is_ctx = pl.program_id(1) * tm < ctx_len
    h_ref[0], g_ref[0] = _norm_modulate(x_ref[0], is_ctx, b, mod_ref, gn_ref, wg_ref, bg_ref, n_batch)


def _pick_tile(n, candidates):
    for c in candidates:
        if n % c == 0:
            return c
    raise ValueError(f"no tile for {n}")


def _norm_gates(xall, mod_l, gn, wg, bg, ctx_len):
    nb, tb, d = xall.shape
    tm = MERGE_ROWS
    assert ctx_len % tm == 0 and tb % tm == 0
    kern = functools.partial(_norm_kernel, ctx_len=ctx_len, tm=tm, n_batch=nb)
    return pl.pallas_call(
        kern,
        out_shape=(jax.ShapeDtypeStruct((nb, tb, d), BF16),
                   jax.ShapeDtypeStruct((nb, tb, LANES), F32)),
        grid=(nb, tb // tm),
        in_specs=[pl.BlockSpec((1, tm, d), lambda b, i: (b, i, 0)),
                  pl.BlockSpec((8, 3 * d), lambda b, i: (0, 0)),
                  pl.BlockSpec((1, d), lambda b, i: (0, 0)),
                  pl.BlockSpec((2, d, LANES), lambda b, i: (0, 0, 0)),
                  pl.BlockSpec((1, LANES), lambda b, i: (0, 0))],
        out_specs=(pl.BlockSpec((1, tm, d), lambda b, i: (b, i, 0)),
                   pl.BlockSpec((1, tm, LANES), lambda b, i: (b, i, 0))),
        compiler_params=_cparams(("arbitrary", "arbitrary")),
        name="norm_gates",
    )(xall, mod_l, gn, wg, bg)


def _proj_kernel(*refs, kind, rs):
    if kind == "decay":
        h_ref, w_ref, b_ref, lb_ref, lf_ref, k_ref = refs
    else:
        h_ref, w_ref, b_ref, out_ref = refs
    tm = h_ref.shape[1]
    for r0 in range(0, tm, rs):
        rows = slice(r0, r0 + rs)
        acc = jnp.dot(h_ref[0, rows, :], w_ref[...], preferred_element_type=F32) + b_ref[...]
        if kind == "raw":
            out_ref[0, rows, :] = acc
        elif kind == "cast":
            out_ref[0, rows, :] = acc.astype(BF16)
        elif kind == "sigmoid":
            out_ref[0, rows, :] = _sigmoid_tanh(acc).astype(BF16)
        elif kind == "silu":
            out_ref[0, rows, :] = (acc * _sigmoid_tanh(acc)).astype(BF16)
        else:
            lbv = lb_ref[...]
            t = jnp.exp2(jnp.abs(acc) * (-LOG2E))
            pos = acc >= 0.0
            rc = 1.0 / (1.0 + t)
            f = jnp.where(pos, 1.0 + lbv * t, t + lbv) * rc
            lf_ref[0, rows, :] = jnp.log(f) * LOG2E
            k_ref[0, rows, :] = ((1.0 - lbv) * jnp.where(pos, t, 1.0) * rc).astype(BF16)


def _proj(h16, w_main, b_main, col0, n_groups, kind, lb=None):
    nb, tb, d = h16.shape
    tm = _pick_tile(tb, (1280, 640, 256))
    tn = n_groups * d
    off = col0 // n_groups
    assert col0 % n_groups == 0
    kern = functools.partial(_proj_kernel, kind=kind, rs=PROJ_ROWS)
    in_specs = [pl.BlockSpec((1, tm, d), lambda b, i: (b, i, 0)),
                pl.BlockSpec((d, tn), lambda b, i: (0, off)),
                pl.BlockSpec((1, tn), lambda b, i: (0, off))]
    args = [h16, w_main, b_main]
    oblk = pl.BlockSpec((1, tm, tn), lambda b, i: (b, i, 0))
    shp = lambda dt: jax.ShapeDtypeStruct((nb, tb, tn), dt)
    if kind == "decay":
        in_specs.append(pl.BlockSpec((1, tn), lambda b, i: (0, 0)))
        args.append(jnp.concatenate([lb] * n_groups, axis=-1))
        out_shape, out_specs = (shp(F32), shp(BF16)), (oblk, oblk)
    else:
        out_shape, out_specs = shp(F32 if kind == "raw" else BF16), oblk
    return pl.pallas_call(
        kern,
        out_shape=out_shape,
        grid=(nb, tb // tm),
        in_specs=in_specs,
        out_specs=out_specs,
        compiler_params=_cparams(("arbitrary", "arbitrary")),
        name="proj_" + kind,
    )(*args)


def _conv_kernel(prev_ref, cur_ref, next_ref, w_ref, bc_ref, sc_ref, out_ref, *, n_tiles, tr, halo):
    i = pl.program_id(1)
    is_ctx = i == 0
    prev_ok = i >= 2
    next_ok = jnp.logical_and(i >= 1, i < n_tiles - 1)
    margin = GRID_W + 8
    full = jnp.concatenate([jnp.where(prev_ok, prev_ref[0, halo - margin:halo, :], 0.0), cur_ref[0],
                            jnp.where(next_ok, next_ref[0, 0:margin, :], 0.0)], axis=0)
    w = w_ref[...]
    sums = []
    for dc in (-1, 0, 1):
        acc = None
        for dr in (-1, 0, 1):
            tap = (dr + 1) * 3 + (dc + 1)
            wt = w[tap:tap + 1, :]
            if dr != 0:
                wt = jnp.where(is_ctx, 0.0, wt)
            start = margin + GRID_W * dr - 8
            term = wt * full[start:start + tr + 16, :]
            acc = term if acc is None else acc + term
        sums.append(acc)
    col = lax.broadcasted_iota(jnp.int32, (tr, 1), 0) % GRID_W
    rows = tr + 16
    left = pltpu.roll(sums[0], 1, axis=0)[8:8 + tr, :]
    right = pltpu.roll(sums[2], rows - 1, axis=0)[8:8 + tr, :]
    y = (sums[1][8:8 + tr, :] + jnp.where(jnp.logical_or(is_ctx, col != 0), left, 0.0)
         + jnp.where(jnp.logical_or(is_ctx, col != GRID_W - 1), right, 0.0) + bc_ref[...])
    out_ref[0] = (y * _sigmoid(y) * sc_ref[...]).astype(out_ref.dtype)


def _conv_silu(raw, w9, bconv, scale, ctx_len):
    nb, tb, n_qk = raw.shape
    tr, halo, tc = CONV_ROWS, CONV_HALO, CONV_COLS
    assert ctx_len == tr and tb % tr == 0 and tr % GRID_W == 0 and halo % GRID_W == 0
    n_tiles = tb // tr
    per = tr // halo
    n_halo = tb // halo
    kern = functools.partial(_conv_kernel, n_tiles=n_tiles, tr=tr, halo=halo)
    return pl.pallas_call(
        kern,
        out_shape=jax.ShapeDtypeStruct((nb, tb, n_qk), BF16),
        grid=(nb, n_tiles, n_qk // tc),
        in_specs=[pl.BlockSpec((1, halo, tc), lambda b, i, j: (b, jnp.maximum(i * per - 1, 0), j)),
                  pl.BlockSpec((1, tr, tc), lambda b, i, j: (b, i, j)),
                  pl.BlockSpec((1, halo, tc), lambda b, i, j: (b, jnp.minimum((i + 1) * per, n_halo - 1), j)),
                  pl.BlockSpec((9, tc), lambda b, i, j: (0, j)),
                  pl.BlockSpec((1, tc), lambda b, i, j: (0, j)),
                  pl.BlockSpec((1, tc), lambda b, i, j: (0, j))],
        out_specs=pl.BlockSpec((1, tr, tc), lambda b, i, j: (b, i, j)),
        compiler_params=_cparams(("arbitrary", "arbitrary", "arbitrary")),
        name="conv_silu",
    )(raw, raw, raw, w9, bconv, scale)


def _mlstm_chunk(qkf_ref, vf_ref, gf_ref, qkb_ref, vb_ref, gb_ref, hf_ref, hb_ref,
                 c_scr, n_scr, m_scr, *, la, dh):
    r = lax.broadcasted_iota(jnp.int32, (la, la), 0)
    s = lax.broadcasted_iota(jnp.int32, (la, la), 1)
    ones_blk = jnp.ones((la, LANES), BF16)
    wide = lambda x, n: jnp.concatenate([x] * (n // LANES), axis=1)
    dirs = ((qkf_ref, vf_ref, gf_ref, hf_ref), (qkb_ref, vb_ref, gb_ref, hb_ref))
    for d, (qk_ref, v_ref, g_ref, h_ref) in enumerate(dirs):
        seen = (s <= r) if d == 0 else (s >= r)
        tri = jnp.where(seen, 1.0, 0.0).astype(BF16)
        g = g_ref[0]
        lsig = jnp.minimum(g, 0.0) - jnp.log(1.0 + jnp.exp(-jnp.abs(g)))
        bc = _cumsum_mm(tri, lsig, terms=MLSTM_CUMSUM_TERMS)
        b2 = pltpu.roll(bc, LANES - H_A, axis=1) * LOG2E
        r2 = g * LOG2E - b2
        r2_t = r2.T
        last = la - 1 if d == 0 else 0
        b2_l = b2[last:last + 1, :]
        m2 = m_scr[d:d + 1, :]
        ws2 = r2 + b2_l
        m2_new = jnp.maximum(b2_l + m2, jnp.max(ws2, axis=0, keepdims=True))
        decay = jnp.exp2(b2_l + m2 - m2_new)
        wse = jnp.exp2(ws2 - m2_new)
        m_scr[d:d + 1, :] = m2_new
        cols = [2 * d * H_A + h for h in range(H_A)]
        q = [qk_ref[0, :, h * dh:(h + 1) * dh] for h in range(H_A)]
        k = [qk_ref[0, :, (H_A + h) * dh:(H_A + h + 1) * dh] for h in range(H_A)]
        v = [v_ref[0, :, h * dh:(h + 1) * dh] for h in range(H_A)]
        col = lambda x, ci: jnp.broadcast_to(x[:, ci:ci + 1], (la, LANES))
        rm = [jnp.where(seen, r2_t[ci:ci + 1, :], NEG) for ci in cols]
        mx = [jnp.broadcast_to(jnp.maximum(m2[:, ci:ci + 1], jnp.max(rm_h, axis=-1, keepdims=True)),
                               (la, LANES)) for ci, rm_h in zip(cols, rm)]
        s_mat = [_dot_nt(q[h], k[h]) for h in range(H_A)]
        for h, ci in enumerate(cols):
            u = d * H_A + h
            w16 = (s_mat[h] * jnp.exp2(rm[h] - wide(mx[h], la))).astype(BF16)
            a_in = jnp.exp2(m2[:, ci:ci + 1] - mx[h])
            em = jnp.exp2(-(col(b2, ci) + mx[h]))
            c_st = c_scr[u]
            n_st = n_scr[u]
            qa = q[h] * wide(a_in.astype(BF16), dh)
            num = jnp.dot(jnp.concatenate([w16, qa], axis=1),
                          jnp.concatenate([v[h], c_st.astype(BF16)], axis=0), preferred_element_type=F32)
            qn = _dot_nt(q[h], jnp.broadcast_to(n_st, (LANES, dh)).astype(BF16))
            den = a_in * qn + jnp.dot(w16, ones_blk, preferred_element_type=F32)
            rcp = 1.0 / jnp.maximum(jnp.abs(den), em)
            h_ref[0, :, h * dh:(h + 1) * dh] = (num * wide(rcp, dh)).astype(h_ref.dtype)
            wse_u = col(wse, ci).astype(BF16)
            dec = decay[:, ci:ci + 1]
            c_scr[u] = dec * c_st + _dot_tn(k[h] * wide(wse_u, dh), v[h])
            n_scr[u] = dec * n_st + _dot_tn(wse_u, k[h])[0:1, :]


def _bwd_chunk(c, n_ctx_chunks, n_chunks):
    return jnp.where(c < n_ctx_chunks, n_ctx_chunks - 1 - c, n_chunks - 1 + n_ctx_chunks - c)


def _hgrn_levels(l):
    return [1 << j for j in range(int(np.log2(l)))]


def _hgrn_q_rows(l, h, d):
    off = h if d == 0 else 0
    return [(k * 2 * h + off, k * 2 * h + off + h) for k in range(l // (2 * h))]


def _hgrn_masks(l):
    r = np.arange(l)[:, None]
    s = np.arange(l)[None, :]
    full, half = [[], []], [[], []]
    for d in (0, 1):
        full[d].append(r == s)
        for h in _hgrn_levels(l):
            same = (r // (2 * h)) == (s // (2 * h))
            m = same & ((r % (2 * h)) >= h) & ((s % (2 * h)) < h)
            m = m if d == 0 else m.T
            if h >= HGRN_COMPACT_MIN:
                half[d].append(np.concatenate([m[a:b] for a, b in _hgrn_q_rows(l, h, d)]))
            else:
                full[d].append(m)
    local = [(r // HGRN_LOCAL == s // HGRN_LOCAL) & (s <= r), (r // HGRN_LOCAL == s // HGRN_LOCAL) & (s >= r)]
    return np.array(full, np.float32), np.array(half, np.float32), np.array(local, np.float32)


def _hgrn_tri(l):
    r = np.arange(l)[:, None]
    s = np.arange(l)[None, :]
    return np.stack([(s <= r), (s >= r)]).astype(np.float32)


def _hgrn_split_rows(g_scr, l, h, d):
    assert h in (2, 4)
    off = h - 1 if d == 0 else h
    n = g_scr.shape[-1]
    sub = lax.broadcasted_iota(jnp.int32, (8, n), 0)
    pieces = []
    for grp in range(l // 8):
        lo = jnp.broadcast_to(g_scr[grp * 8 + off:grp * 8 + off + 1, :], (8, n))
        if h == 4:
            pieces.append(lo)
        else:
            hi = jnp.broadcast_to(g_scr[grp * 8 + 4 + off:grp * 8 + 4 + off + 1, :], (8, n))
            pieces.append(jnp.where(sub < 4, lo, hi))
    return jnp.concatenate(pieces, axis=0)


def _hgrn_block_decay(g, l):
    spans = [jnp.abs(g[b0:b0 + 1, :] - g[b0 + HGRN_LOCAL - 1:b0 + HGRN_LOCAL, :]) for b0 in range(0, l, HGRN_LOCAL)]
    return jnp.max(jnp.concatenate(spans, axis=0))


def _hgrn_chunk(dirs, mfull_ref, mhalf_ref, mlocal_ref, st_scr, g_scr, *, l, dh, fast, rows):
    heads = [slice(h * dh, (h + 1) * dh) for h in range(H_B)]
    q = [dirs[d][0][0, rows[d], :] for d in (0, 1)]
    kk = [dirs[d][2][0, rows[d], :] for d in (0, 1)]

    def scores(qs, ks, mask):
        return [_dot_nt(qs[:, hs], ks[:, hs]).astype(BF16) * mask for hs in heads]

    e_loc = []
    if fast:
        acc = []
        for d in (0, 1):
            g = g_scr.at[d]
            parts = []
            for b0 in range(0, l, HGRN_LOCAL):
                base = b0 if d == 0 else b0 + HGRN_LOCAL - 1
                parts.append(g[b0:b0 + HGRN_LOCAL, :] - g[base:base + 1, :])
            x = jnp.concatenate(parts, axis=0)
            e_loc.append((jnp.exp2(x), jnp.exp2(-x)))
            acc.append(scores(q[d] * e_loc[d][0].astype(BF16), kk[d] * e_loc[d][1].astype(BF16), mlocal_ref[d]))
        levels = [hl for hl in _hgrn_levels(l) if hl >= HGRN_LOCAL]
    else:
        acc = [scores(q[d], kk[d], mfull_ref[d, 0]) for d in (0, 1)]
        levels = _hgrn_levels(l)

    def local_base(d, r0):
        b0 = r0 - r0 % HGRN_LOCAL
        return b0 if d == 0 else b0 + HGRN_LOCAL - 1

    def rescaled(d, which, r0, r1, row):
        g = g_scr.at[d]
        out = []
        for b0 in range(r0, r1, HGRN_LOCAL):
            gb = g[local_base(d, b0):local_base(d, b0) + 1, :]
            out.append(e_loc[d][which][b0:b0 + HGRN_LOCAL] * jnp.exp2(gb - row if which == 0 else row - gb))
        return jnp.concatenate(out, axis=0) if len(out) > 1 else out[0]

    all_levels = _hgrn_levels(l)
    for hl in levels:
        n_full = 1 + all_levels.index(hl)
        n_half = all_levels.index(hl) - all_levels.index(HGRN_COMPACT_MIN)
        for d in (0, 1):
            g = g_scr.at[d]
            m_off = hl - 1 if d == 0 else hl
            q_rows = _hgrn_q_rows(l, hl, d)
            k_rows = _hgrn_q_rows(l, hl, 1 - d)
            if hl < HGRN_COMPACT_MIN:
                if hl == 1:
                    qs, ks = q[d] * jnp.exp2(dirs[d][3][0, rows[d], :]).astype(BF16), kk[d]
                else:
                    if hl < 8:
                        x = -jnp.abs(g[...] - _hgrn_split_rows(g, l, hl, d))
                    else:
                        parts = {}
                        for (q0, q1), (k0, k1) in zip(q_rows, k_rows):
                            gm = g[min(q0, k0) + m_off:min(q0, k0) + m_off + 1, :]
                            parts[q0] = g[q0:q1, :] - gm
                            parts[k0] = gm - g[k0:k1, :]
                        x = jnp.concatenate([parts[r0] for r0 in sorted(parts)], axis=0)
                    e = jnp.exp2(x).astype(BF16)
                    qs, ks = q[d] * e, kk[d] * e
                acc[d] = [a + t for a, t in zip(acc[d], scores(qs, ks, mfull_ref[d, n_full]))]
            else:
                xq, xk = [], []
                for (q0, q1), (k0, k1) in zip(q_rows, k_rows):
                    gm = g[min(q0, k0) + m_off:min(q0, k0) + m_off + 1, :]
                    if fast:
                        xq.append(rescaled(d, 0, q0, q1, gm))
                        xk.append(rescaled(d, 1, k0, k1, gm))
                    else:
                        xq.append(jnp.exp2(g[q0:q1, :] - gm))
                        xk.append(jnp.exp2(gm - g[k0:k1, :]))
                eq = jnp.concatenate(xq, axis=0).astype(BF16)
                ek = jnp.concatenate(xk, axis=0).astype(BF16)
                qc = jnp.concatenate([q[d][q0:q1] for q0, q1 in q_rows], axis=0) * eq
                kparts = {q0: kk[d][q0:q1] for q0, q1 in q_rows}
                for n, (k0, k1) in enumerate(k_rows):
                    kparts[k0] = kk[d][k0:k1] * ek[n * hl:(n + 1) * hl]
                kt = jnp.concatenate([kparts[r0] for r0 in sorted(kparts)], axis=0)
                terms = scores(qc, kt, mhalf_ref[d, n_half])
                for i in range(H_B):
                    pieces = {k0: acc[d][i][k0:k1] for k0, k1 in k_rows}
                    for n, (q0, q1) in enumerate(q_rows):
                        pieces[q0] = acc[d][i][q0:q1] + terms[i][n * hl:(n + 1) * hl]
                    acc[d][i] = jnp.concatenate([pieces[r0] for r0 in sorted(pieces)], axis=0)
    for d in (0, 1):
        o_ref = dirs[d][4]
        v = dirs[d][1][0, rows[d], :]
        last = l - 1 if d == 0 else 0
        g_all = g_scr[d]
        g_l = g_scr[d, last:last + 1, :]
        if fast:
            qi = q[d] * rescaled(d, 0, 0, l, 0.0).astype(BF16)
            ks = kk[d] * rescaled(d, 1, 0, l, g_l).astype(BF16)
        else:
            qi = q[d] * jnp.exp2(g_all).astype(BF16)
            ks = kk[d] * jnp.exp2(g_l - g_all).astype(BF16)
        dec = jnp.exp2(g_l)
        for i, hs in enumerate(heads):
            u = d * H_B + i
            st = st_scr[u]
            o = _dot_nt(qi[:, hs], st.astype(BF16)) + jnp.dot(acc[d][i], v[:, hs], preferred_element_type=F32)
            o_ref[0, rows[d], hs] = o.astype(o_ref.dtype)
            st_scr[u] = st * dec[:, hs] + _dot_tn(v[:, hs], ks[:, hs])


def _hgrn_kernel(qf_ref, vf_ref, kf_ref, lff_ref, qb_ref, vb_ref, kb_ref, lfb_ref,
                 tri_ref, mfull_ref, mhalf_ref, mlocal_ref, of_ref, ob_ref, st_scr, g_scr, *, l, dh):
    @pl.when(pl.program_id(1) == 0)
    def _():
        st_scr[...] = jnp.zeros_like(st_scr)

    dirs = ((qf_ref, vf_ref, kf_ref, lff_ref, of_ref), (qb_ref, vb_ref, kb_ref, lfb_ref, ob_ref))
    n_sub = qf_ref.shape[1] // l
    rows = [(slice(j * l, (j + 1) * l), slice((n_sub - 1 - j) * l, (n_sub - j) * l)) for j in range(n_sub)]
    worst = None
    for j in range(n_sub):
        for d in (0, 1):
            g_scr[j, d] = _cumsum_mm(tri_ref[d], dirs[d][3][0, rows[j][d], :], terms=HGRN_CUMSUM_TERMS)
            span = _hgrn_block_decay(g_scr.at[j, d], l)
            worst = span if worst is None else jnp.maximum(worst, span)
    in_range = worst <= HGRN_LOCAL_MAX_LOG2

    def run(fast):
        for j in range(n_sub):
            _hgrn_chunk(dirs, mfull_ref, mhalf_ref, mlocal_ref, st_scr, g_scr.at[j], l=l, dh=dh, fast=fast,
                        rows=rows[j])

    pl.when(in_range)(functools.partial(run, True))
    pl.when(jnp.logical_not(in_range))(functools.partial(run, False))


def _scan_kernel(*refs, la, dh_a, lb, dh_b):
    m_in, h_in = refs[0:6], refs[6:18]
    ha_f, ha_b, hb_f, hb_b = refs[18:22]
    c_scr, n_scr, m_scr, st_scr, g_scr = refs[22:]

    @pl.when(pl.program_id(1) == 0)
    def _():
        for scr in (c_scr, n_scr, m_scr):
            scr[...] = jnp.zeros_like(scr)

    _mlstm_chunk(*m_in, ha_f, ha_b, c_scr, n_scr, m_scr, la=la, dh=dh_a)
    _hgrn_kernel(*h_in, hb_f, hb_b, st_scr, g_scr, l=lb, dh=dh_b)


def _scans(qk, mix, g, kdec, lf, ctx_len, d_a, d_b):
    nb, tb, _ = qk.shape
    la, lb = L_A, L_B
    dh_a, dh_b = d_a // H_A, d_b // H_B
    assert dh_b == LANES and la == HGRN_STEP_CHUNKS * lb and ctx_len % la == 0 and tb % la == 0
    nc, ncc = tb // la, ctx_len // la
    bw = functools.partial(_bwd_chunk, n_ctx_chunks=ncc, n_chunks=nc)
    kern = functools.partial(_scan_kernel, la=la, dh_a=dh_a, lb=lb, dh_b=dh_b)
    tri = jnp.asarray(_hgrn_tri(lb), BF16)
    mfull, mhalf, mlocal = (jnp.asarray(m, BF16) for m in _hgrn_masks(lb))
    ident = lambda c: c
    blk = lambda w, col, chunk: pl.BlockSpec((1, la, w), lambda b, c: (b, chunk(c), col))
    const = lambda a: pl.BlockSpec(a.shape, lambda b, c: (0,) * a.ndim)
    hshape = jax.ShapeDtypeStruct((nb, tb, d_a), BF16)
    oshape = jax.ShapeDtypeStruct((nb, tb, d_b), BF16)
    return pl.pallas_call(
        kern,
        out_shape=(hshape, hshape, oshape, oshape),
        grid=(nb, nc),
        in_specs=[blk(2 * d_a, 0, ident), blk(d_a, MIX_V, ident), blk(LANES, 0, ident),
                  blk(2 * d_a, 0, bw), blk(d_a, MIX_V, bw), blk(LANES, 0, bw),
                  blk(d_b, MIX_QB, ident), blk(d_b, MIX_IB, ident), blk(d_b, 0, ident), blk(d_b, 0, ident),
                  blk(d_b, MIX_QB, bw), blk(d_b, MIX_IB, bw), blk(d_b, 1, bw), blk(d_b, 1, bw),
                  const(tri), const(mfull), const(mhalf), const(mlocal)],
        out_specs=(blk(d_a, 0, ident), blk(d_a, 0, bw), blk(d_b, 0, ident), blk(d_b, 0, bw)),
        scratch_shapes=[pltpu.VMEM((2 * H_A, dh_a, dh_a), F32),
                        pltpu.VMEM((2 * H_A, 1, dh_a), F32),
                        pltpu.VMEM((8, LANES), F32),
                        pltpu.VMEM((2 * H_B, dh_b, dh_b), F32),
                        pltpu.VMEM((HGRN_STEP_CHUNKS, 2, lb, d_b), F32)],
        compiler_params=_cparams(("arbitrary", "arbitrary")),
        name="scans",
    )(qk, mix, g, qk, mix, g, mix, mix, kdec, lf, mix, mix, kdec, lf, tri, mfull, mhalf, mlocal)


def _head_rms(x, n_heads):
    dh = x.shape[-1] // n_heads
    outs = []
    for h in range(n_heads):
        xh = x[:, h * dh:(h + 1) * dh]
        outs.append(xh * lax.rsqrt(jnp.mean(xh * xh, axis=-1, keepdims=True) + EPS))
    return jnp.concatenate(outs, axis=-1)


def _merge_kernel(*refs, ctx_len, tm, n_batch, fuse_next):
    (x_ref, haf_ref, hab_ref, hbf_ref, hbb_ref, o_ref, za_ref, zb_ref, ma_ref, mb_ref,
     mod_ref, gha_ref, ghb_ref, wa_ref, wb_ref, wo_ref) = refs[:16]
    b = pl.program_id(0)
    d = x_ref.shape[-1]
    f32 = lambda ref: ref[0].astype(F32)
    h_a = (f32(haf_ref) + f32(hab_ref)) * f32(o_ref)
    y_a = _head_rms(h_a, H_A) * gha_ref[...] * f32(za_ref)
    y_b = _head_rms(f32(hbf_ref) + f32(hbb_ref), H_B) * ghb_ref[...] * f32(zb_ref)
    pa = jnp.dot(y_a.astype(BF16), wa_ref[...], preferred_element_type=F32)
    pb = jnp.dot(y_b.astype(BF16), wb_ref[...], preferred_element_type=F32)
    y = f32(ma_ref) * pa + f32(mb_ref) * pb
    br = jnp.dot(y.astype(BF16), wo_ref[...], preferred_element_type=F32)
    is_ctx = pl.program_id(1) * tm < ctx_len
    gate = mod_ref[pl.ds(jnp.where(is_ctx, n_batch, b), 1), 2 * d:3 * d]
    x_new = x_ref[0] + gate * br
    if fuse_next:
        nmod_ref, ngn_ref, nwg_ref, nbg_ref, out_ref, h_ref, g_ref = refs[16:]
        h_ref[0], g_ref[0] = _norm_modulate(x_new, is_ctx, b, nmod_ref, ngn_ref, nwg_ref, nbg_ref, n_batch)
        out_ref[0] = x_new
    else:
        gf_ref, out_ref = refs[16:]
        out_ref[0] = x_new * lax.rsqrt(jnp.mean(x_new * x_new, axis=-1, keepdims=True) + EPS) * gf_ref[...]


def _merge(xall, haf, hab, hbf, hbb, sig, silu, mod_l, gha, ghb, wa, wb, wo, ctx_len, nxt=None, g_final=None):
    nb, tb, d = xall.shape
    tm = MERGE_ROWS
    assert (nxt is None) != (g_final is None) and ctx_len % tm == 0 and tb % tm == 0
    kern = functools.partial(_merge_kernel, ctx_len=ctx_len, tm=tm, n_batch=nb, fuse_next=nxt is not None)
    row = pl.BlockSpec((1, tm, d), lambda b, i: (b, i, 0))
    pcol = lambda col: pl.BlockSpec((1, tm, d), lambda b, i: (b, i, col))
    full = lambda shape: pl.BlockSpec(shape, lambda b, i: (0,) * len(shape))
    in_specs = [row, row, row, row, row,
                pcol(SIG_O), pcol(SILU_ZA), pcol(SILU_ZB), pcol(SIG_MA), pcol(SIG_MB),
                full((8, 3 * d)), full((1, d)), full((1, d)),
                full((d, d)), full((d, d)), full((d, d))]
    args = [xall, haf, hab, hbf, hbb, sig, silu, silu, sig, sig, mod_l, gha, ghb, wa, wb, wo]
    out_shape = jax.ShapeDtypeStruct(xall.shape, F32)
    out_specs = row
    if nxt is not None:
        in_specs += [full((8, 3 * d)), full((1, d)), full((2, d, LANES)), full((1, LANES))]
        args += list(nxt)
        out_shape = (out_shape, jax.ShapeDtypeStruct((nb, tb, d), BF16),
                     jax.ShapeDtypeStruct((nb, tb, LANES), F32))
        out_specs = (row, row, pl.BlockSpec((1, tm, LANES), lambda b, i: (b, i, 0)))
    else:
        in_specs.append(full((1, d)))
        args.append(g_final)
        ctx_tiles = ctx_len // tm
        out_shape = jax.ShapeDtypeStruct((nb, tb - ctx_len, d), F32)
        out_specs = pl.BlockSpec((1, tm, d), lambda b, i: (b, jnp.maximum(i - ctx_tiles, 0), 0))
    return pl.pallas_call(
        kern,
        out_shape=out_shape,
        grid=(nb, tb // tm),
        in_specs=in_specs,
        out_specs=out_specs,
        compiler_params=_cparams(("arbitrary", "arbitrary")),
        name="merge",
    )(*args)


def kernel(x, c, ctx, c_ctx, w_ada, b_ada, g_norm, w_in, b_in, w_conv, b_conv, lb_logits,
           g_head_a, g_head_b, w_a, w_b, w_out, g_final):
    nb, seq, d = x.shape
    ctx_len = ctx.shape[1]
    depth = w_ada.shape[0]
    d_a = g_head_a.shape[-1]
    d_b = g_head_b.shape[-1]
    assert d_a == d and d_b == d and nb + 1 <= 8
    dh_a = d_a // H_A

    g0 = 5 * d
    g1 = g0 + N_GATES
    grp = lambda n: slice(n * d, (n + 1) * d) if n < 5 else slice(g1 + (n - 5) * d, g1 + (n - 4) * d)
    QK0, QK1, V, O, ZA, QB, IB, FF, FB, ZB, MA, MB = range(12)
    order = (V, QB, IB, O, MA, MB, QK0, QK1, ZA, ZB, FF, FB)
    w16 = w_in.astype(BF16)
    w_main = jnp.concatenate([w16[:, :, grp(n)] for n in order], axis=-1)
    b_main = jnp.concatenate([b_in[:, grp(n)] for n in order], axis=-1).astype(F32)
    wg = jnp.pad(w_in[:, :, g0:g1].astype(F32), ((0, 0), (0, 0), (0, LANES - N_GATES)))
    wg_hi = wg.astype(BF16)
    wg_lo = (wg - wg_hi.astype(F32)).astype(BF16)
    wg2 = jnp.stack([wg_hi, wg_lo], axis=1)
    bg = jnp.pad(b_in[:, g0:g1].astype(F32), ((0, 0), (0, LANES - N_GATES)))
    qk_scale = jnp.concatenate([jnp.ones((1, d_a), F32), jnp.full((1, d_a), dh_a ** -0.5, F32)], axis=-1)
    w9 = w_conv.reshape(depth, 9, 2 * d_a).astype(F32)

    cc = jnp.concatenate([c.astype(F32), c_ctx.astype(F32)[None, :],
                          jnp.zeros((8 - nb - 1, d), F32)], axis=0)
    mod = _ada(cc, w_ada.astype(F32), b_ada.astype(F32))
    lbs = _lbs(lb_logits)
    c_sig, c_raw, c_silu, c_dec = (int(n) for n in np.cumsum((N_MIX, N_SIG, N_RAW, N_SILU)))

    xall = jnp.concatenate([ctx.astype(F32), x.astype(F32)], axis=1)
    norm_args = lambda l: (mod[l], g_norm[l][None, :].astype(F32), wg2[l], bg[l][None, :])
    h16, g = _norm_gates(xall, *norm_args(0), ctx_len)
    for l in range(depth):
        bm = b_main[l][None, :]
        raw = _proj(h16, w_main[l], bm, c_raw, N_RAW, "raw")
        mix = _proj(h16, w_main[l], bm, 0, N_MIX, "cast")
        sig = _proj(h16, w_main[l], bm, c_sig, N_SIG, "sigmoid")
        silu = _proj(h16, w_main[l], bm, c_silu, N_SILU, "silu")
        lf, kdec = _proj(h16, w_main[l], bm, c_dec, N_DECAY, "decay", lbs[l][None, :])
        qk = _conv_silu(raw, w9[l], b_conv[l][None, :].astype(F32), qk_scale, ctx_len)
        haf, hab, hbf, hbb = _scans(qk, mix, g, kdec, lf, ctx_len, d_a, d_b)
        last = l + 1 == depth
        res = _merge(xall, haf, hab, hbf, hbb, sig, silu, mod[l], g_head_a[l][None, :].astype(F32),
                     g_head_b[l][None, :].astype(F32), w_a[l].astype(BF16), w_b[l].astype(BF16),
                     w_out[l].astype(BF16), ctx_len, nxt=None if last else norm_args(l + 1),
                     g_final=g_final[None, :].astype(F32) if last else None)
        xall, h16, g = (res, None, None) if last else res
    return xall.astype(x.dtype)
```

```python
import functools
import math

import numpy as np
import jax
import jax.numpy as jnp
from jax import lax
from jax.experimental import pallas as pl
from jax.experimental.pallas import tpu as pltpu

F32 = jnp.float32
BF16 = jnp.bfloat16

H_A = 4
H_B = 8
GRID_W = 64
EPS = 1e-6
NEG = -1e30
LOG2E = math.log2(math.e)
N_GATES = 4 * H_A
LANES = 128
L_A = 256
L_B = 128
HGRN_COMPACT_MIN = 16
HGRN_LOCAL = 32
HGRN_LOCAL_MAX_LOG2 = 100.0
HGRN_STEP_CHUNKS = 2
CONV_ROWS = 256
CONV_HALO = 128
HGRN_CUMSUM_TERMS = 2
MLSTM_CUMSUM_TERMS = 2
CONV_COLS = 2048
PROJ_ROWS = 256
MERGE_ROWS = 256
V7X_VMEM_LIMIT = 56 * 1024 * 1024
N_MIX, N_SIG, N_RAW, N_SILU, N_DECAY = 3, 3, 2, 2, 2
MIX_V, MIX_QB, MIX_IB = 0, 1, 2
SIG_O, SIG_MA, SIG_MB = 0, 1, 2
SILU_ZA, SILU_ZB = 0, 1


def _cparams(semantics):
    return pltpu.CompilerParams(dimension_semantics=semantics, vmem_limit_bytes=V7X_VMEM_LIMIT)


def _sigmoid(x):
    return 1.0 / (1.0 + jnp.exp(-x))


def _sigmoid_tanh(x):
    return 0.5 * jnp.tanh(0.5 * x) + 0.5


def _split3(x):
    hi = x.astype(BF16)
    r1 = x - hi.astype(F32)
    mid = r1.astype(BF16)
    lo = (r1 - mid.astype(F32)).astype(BF16)
    return hi, mid, lo


def _cumsum_mm(tri, x, terms=3):
    out = None
    for part in _split3(x)[:terms]:
        term = jnp.dot(tri, part, preferred_element_type=F32)
        out = term if out is None else out + term
    return out


def _dot_nt(a, b):
    return lax.dot_general(a, b, (((1,), (1,)), ((), ())), preferred_element_type=F32)


def _dot_tn(a, b):
    return lax.dot_general(a, b, (((0,), (0,)), ((), ())), preferred_element_type=F32)


def _ada_kernel(cc_ref, w_ref, b_ref, out_ref):
    s = cc_ref[...]
    s = s * _sigmoid(s)
    w = w_ref[0]
    s_hi = s.astype(BF16)
    s_lo = (s - s_hi.astype(F32)).astype(BF16)
    w_hi = w.astype(BF16)
    w_lo = (w - w_hi.astype(F32)).astype(BF16)
    d = lambda a, b: jnp.dot(a, b, preferred_element_type=F32)
    out_ref[0] = d(s_hi, w_hi) + d(s_lo, w_hi) + d(s_hi, w_lo) + b_ref[0]


def _ada(cc, w_ada, b_ada):
    depth, d, n3 = w_ada.shape
    tn = 512
    return pl.pallas_call(
        _ada_kernel,
        out_shape=jax.ShapeDtypeStruct((depth, 8, n3), F32),
        grid=(depth, n3 // tn),
        in_specs=[pl.BlockSpec((8, d), lambda l, j: (0, 0)),
                  pl.BlockSpec((1, d, tn), lambda l, j: (l, 0, j)),
                  pl.BlockSpec((1, 1, tn), lambda l, j: (l, 0, j))],
        out_specs=pl.BlockSpec((1, 8, tn), lambda l, j: (l, 0, j)),
        compiler_params=_cparams(("arbitrary", "arbitrary")),
        name="ada",
    )(cc, w_ada, b_ada.reshape(depth, 1, n3))


def _lbs_kernel(lg_ref, out_ref):
    x = lg_ref[...]
    depth = x.shape[0]
    m = jnp.max(x, axis=0, keepdims=True)
    e = jnp.exp(x - m)
    p = e / jnp.sum(e, axis=0, keepdims=True)
    acc = jnp.zeros_like(p[0:1])
    for l in range(depth):
        acc = acc + p[l:l + 1]
        out_ref[l:l + 1, :] = acc - p[0:1]


def _lbs(lb_logits):
    return pl.pallas_call(
        _lbs_kernel, out_shape=jax.ShapeDtypeStruct(lb_logits.shape, F32), name="lbs",
    )(lb_logits.astype(F32))


def _norm_modulate(x, is_ctx, b, mod_ref, gn_ref, wg_ref, bg_ref, n_batch):
    d = x.shape[-1]
    ms = jnp.mean(x * x, axis=-1, keepdims=True)
    y = x * lax.rsqrt(ms + EPS) * gn_ref[...]
    m = mod_ref[pl.ds(jnp.where(is_ctx, n_batch, b), 1), :]
    shift, scale = m[:, 0:d], m[:, d:2 * d]
    h = y * (1.0 + scale) + shift
    hi = h.astype(BF16)
    lo = (h - hi.astype(F32)).astype(BF16)
    dd = lambda a, bb: jnp.dot(a, bb, preferred_element_type=F32)
    return hi, dd(hi, wg_ref[0]) + dd(lo, wg_ref[0]) + dd(hi, wg_ref[1]) + bg_ref[...]


def _norm_kernel(x_ref, mod_ref, gn_ref, wg_ref, bg_ref, h_ref, g_ref, *, ctx_len, tm, n_batch):
    b = pl.program_id(0)
    for r0 in range(0, tm, MERGE_ROWS):
        rows = slice(r0, r0 + MERGE_ROWS)
        is_ctx = pl.program_id(1) * tm + r0 < ctx_len
        h_ref[0, rows, :], g_ref[0, rows, :] = _norm_modulate(x_ref[0, rows, :], is_ctx, b, mod_ref, gn_ref,
                                                              wg_ref, bg_ref, n_batch)


def _pick_tile(n, candidates):
    for c in candidates:
        if n % c == 0:
            return c
    raise ValueError(f"no tile for {n}")


def _norm_gates(xall, mod_l, gn, wg, bg, ctx_len):
    nb, tb, d = xall.shape
    tm = _pick_tile(tb, (5 * MERGE_ROWS, MERGE_ROWS))
    assert ctx_len % MERGE_ROWS == 0
    kern = functools.partial(_norm_kernel, ctx_len=ctx_len, tm=tm, n_batch=nb)
    return pl.pallas_call(
        kern,
        out_shape=(jax.ShapeDtypeStruct((nb, tb, d), BF16),
                   jax.ShapeDtypeStruct((nb, tb, LANES), F32)),
        grid=(nb, tb // tm),
        in_specs=[pl.BlockSpec((1, tm, d), lambda b, i: (b, i, 0)),
                  pl.BlockSpec((8, 3 * d), lambda b, i: (0, 0)),
                  pl.BlockSpec((1, d), lambda b, i: (0, 0)),
                  pl.BlockSpec((2, d, LANES), lambda b, i: (0, 0, 0)),
                  pl.BlockSpec((1, LANES), lambda b, i: (0, 0))],
        out_specs=(pl.BlockSpec((1, tm, d), lambda b, i: (b, i, 0)),
                   pl.BlockSpec((1, tm, LANES), lambda b, i: (b, i, 0))),
        compiler_params=_cparams(("arbitrary", "arbitrary")),
        name="norm_gates",
    )(xall, mod_l, gn, wg, bg)


def _proj_kernel(*refs, kind, rs):
    if kind == "decay":
        h_ref, w_ref, b_ref, lb_ref, lf_ref, k_ref = refs
    else:
        h_ref, w_ref, b_ref, out_ref = refs
    tm = h_ref.shape[1]
    for r0 in range(0, tm, rs):
        rows = slice(r0, r0 + rs)
        acc = jnp.dot(h_ref[0, rows, :], w_ref[...], preferred_element_type=F32) + b_ref[...]
        if kind == "raw":
            out_ref[0, rows, :] = acc
        elif kind == "cast":
            out_ref[0, rows, :] = acc.astype(BF16)
        elif kind == "sigmoid":
            out_ref[0, rows, :] = _sigmoid_tanh(acc).astype(BF16)
        elif kind == "silu":
            out_ref[0, rows, :] = (acc * _sigmoid_tanh(acc)).astype(BF16)
        else:
            lbv = lb_ref[...]
            t = jnp.exp2(jnp.abs(acc) * (-LOG2E))
            pos = acc >= 0.0
            rc = 1.0 / (1.0 + t)
            f = jnp.where(pos, 1.0 + lbv * t, t + lbv) * rc
            lf_ref[0, rows, :] = jnp.log(f) * LOG2E
            k_ref[0, rows, :] = ((1.0 - lbv) * jnp.where(pos, t, 1.0) * rc).astype(BF16)


def _proj(h16, w_main, b_main, col0, n_groups, kind, lb=None):
    nb, tb, d = h16.shape
    tm = _pick_tile(tb, (1280, 640, 256))
    tn = n_groups * d
    off = col0 // n_groups
    assert col0 % n_groups == 0
    kern = functools.partial(_proj_kernel, kind=kind, rs=PROJ_ROWS)
    in_specs = [pl.BlockSpec((1, tm, d), lambda b, i: (b, i, 0)),
                pl.BlockSpec((d, tn), lambda b, i: (0, off)),
                pl.BlockSpec((1, tn), lambda b, i: (0, off))]
    args = [h16, w_main, b_main]
    oblk = pl.BlockSpec((1, tm, tn), lambda b, i: (b, i, 0))
    shp = lambda dt: jax.ShapeDtypeStruct((nb, tb, tn), dt)
    if kind == "decay":
        in_specs.append(pl.BlockSpec((1, tn), lambda b, i: (0, 0)))
        args.append(jnp.concatenate([lb] * n_groups, axis=-1))
        out_shape, out_specs = (shp(F32), shp(BF16)), (oblk, oblk)
    else:
        out_shape, out_specs = shp(F32 if kind == "raw" else BF16), oblk
    return pl.pallas_call(
        kern,
        out_shape=out_shape,
        grid=(nb, tb // tm),
        in_specs=in_specs,
        out_specs=out_specs,
        compiler_params=_cparams(("arbitrary", "arbitrary")),
        name="proj_" + kind,
    )(*args)


def _conv_kernel(prev_ref, cur_ref, next_ref, w_ref, bc_ref, sc_ref, out_ref, *, n_tiles, tr, halo):
    i = pl.program_id(1)
    is_ctx = i == 0
    prev_ok = i >= 2
    next_ok = jnp.logical_and(i >= 1, i < n_tiles - 1)
    margin = GRID_W + 8
    full = jnp.concatenate([jnp.where(prev_ok, prev_ref[0, halo - margin:halo, :], 0.0), cur_ref[0],
                            jnp.where(next_ok, next_ref[0, 0:margin, :], 0.0)], axis=0)
    w = w_ref[...]
    sums = []
    for dc in (-1, 0, 1):
        acc = None
        for dr in (-1, 0, 1):
            tap = (dr + 1) * 3 + (dc + 1)
            wt = w[tap:tap + 1, :]
            if dr != 0:
                wt = jnp.where(is_ctx, 0.0, wt)
            start = margin + GRID_W * dr - 8
            term = wt * full[start:start + tr + 16, :]
            acc = term if acc is None else acc + term
        sums.append(acc)
    col = lax.broadcasted_iota(jnp.int32, (tr, 1), 0) % GRID_W
    rows = tr + 16
    left = pltpu.roll(sums[0], 1, axis=0)[8:8 + tr, :]
    right = pltpu.roll(sums[2], rows - 1, axis=0)[8:8 + tr, :]
    y = (sums[1][8:8 + tr, :] + jnp.where(jnp.logical_or(is_ctx, col != 0), left, 0.0)
         + jnp.where(jnp.logical_or(is_ctx, col != GRID_W - 1), right, 0.0) + bc_ref[...])
    out_ref[0] = (y * _sigmoid(y) * sc_ref[...]).astype(out_ref.dtype)


def _conv_silu(raw, w9, bconv, scale, ctx_len):
    nb, tb, n_qk = raw.shape
    tr, halo, tc = CONV_ROWS, CONV_HALO, CONV_COLS
    assert ctx_len == tr and tb % tr == 0 and tr % GRID_W == 0 and halo % GRID_W == 0
    n_tiles = tb // tr
    per = tr // halo
    n_halo = tb // halo
    kern = functools.partial(_conv_kernel, n_tiles=n_tiles, tr=tr, halo=halo)
    return pl.pallas_call(
        kern,
        out_shape=jax.ShapeDtypeStruct((nb, tb, n_qk), BF16),
        grid=(nb, n_tiles, n_qk // tc),
        in_specs=[pl.BlockSpec((1, halo, tc), lambda b, i, j: (b, jnp.maximum(i * per - 1, 0), j)),
                  pl.BlockSpec((1, tr, tc), lambda b, i, j: (b, i, j)),
                  pl.BlockSpec((1, halo, tc), lambda b, i, j: (b, jnp.minimum((i + 1) * per, n_halo - 1), j)),
                  pl.BlockSpec((9, tc), lambda b, i, j: (0, j)),
                  pl.BlockSpec((1, tc), lambda b, i, j: (0, j)),
                  pl.BlockSpec((1, tc), lambda b, i, j: (0, j))],
        out_specs=pl.BlockSpec((1, tr, tc), lambda b, i, j: (b, i, j)),
        compiler_params=_cparams(("arbitrary", "arbitrary", "arbitrary")),
        name="conv_silu",
    )(raw, raw, raw, w9, bconv, scale)


def _mlstm_chunk(qkf_ref, vf_ref, gf_ref, qkb_ref, vb_ref, gb_ref, hf_ref, hb_ref,
                 c_scr, n_scr, m_scr, *, la, dh):
    r = lax.broadcasted_iota(jnp.int32, (la, la), 0)
    s = lax.broadcasted_iota(jnp.int32, (la, la), 1)
    ones_blk = jnp.ones((la, LANES), BF16)
    wide = lambda x, n: jnp.concatenate([x] * (n // LANES), axis=1)
    dirs = ((qkf_ref, vf_ref, gf_ref, hf_ref), (qkb_ref, vb_ref, gb_ref, hb_ref))
    for d, (qk_ref, v_ref, g_ref, h_ref) in enumerate(dirs):
        seen = (s <= r) if d == 0 else (s >= r)
        tri = jnp.where(seen, 1.0, 0.0).astype(BF16)
        g = g_ref[0]
        lsig = jnp.minimum(g, 0.0) - jnp.log(1.0 + jnp.exp(-jnp.abs(g)))
        bc = _cumsum_mm(tri, lsig, terms=MLSTM_CUMSUM_TERMS)
        b2 = pltpu.roll(bc, LANES - H_A, axis=1) * LOG2E
        r2 = g * LOG2E - b2
        r2_t = r2.T
        last = la - 1 if d == 0 else 0
        b2_l = b2[last:last + 1, :]
        m2 = m_scr[d:d + 1, :]
        ws2 = r2 + b2_l
        m2_new = jnp.maximum(b2_l + m2, jnp.max(ws2, axis=0, keepdims=True))
        decay = jnp.exp2(b2_l + m2 - m2_new)
        wse = jnp.exp2(ws2 - m2_new)
        m_scr[d:d + 1, :] = m2_new
        cols = [2 * d * H_A + h for h in range(H_A)]
        q = [qk_ref[0, :, h * dh:(h + 1) * dh] for h in range(H_A)]
        k = [qk_ref[0, :, (H_A + h) * dh:(H_A + h + 1) * dh] for h in range(H_A)]
        v = [v_ref[0, :, h * dh:(h + 1) * dh] for h in range(H_A)]
        col = lambda x, ci: jnp.broadcast_to(x[:, ci:ci + 1], (la, LANES))
        rm = [jnp.where(seen, r2_t[ci:ci + 1, :], NEG) for ci in cols]
        mx = [jnp.broadcast_to(jnp.maximum(m2[:, ci:ci + 1], jnp.max(rm_h, axis=-1, keepdims=True)),
                               (la, LANES)) for ci, rm_h in zip(cols, rm)]
        s_mat = [_dot_nt(q[h], k[h]) for h in range(H_A)]
        for h, ci in enumerate(cols):
            u = d * H_A + h
            w16 = (s_mat[h] * jnp.exp2(rm[h] - wide(mx[h], la))).astype(BF16)
            a_in = jnp.exp2(m2[:, ci:ci + 1] - mx[h])
            em = jnp.exp2(-(col(b2, ci) + mx[h]))
            c_st = c_scr[u]
            n_st = n_scr[u]
            qa = q[h] * wide(a_in.astype(BF16), dh)
            num = jnp.dot(jnp.concatenate([w16, qa], axis=1),
                          jnp.concatenate([v[h], c_st.astype(BF16)], axis=0), preferred_element_type=F32)
            qn = _dot_nt(q[h], jnp.broadcast_to(n_st, (LANES, dh)).astype(BF16))
            den = a_in * qn + jnp.dot(w16, ones_blk, preferred_element_type=F32)
            rcp = 1.0 / jnp.maximum(jnp.abs(den), em)
            h_ref[0, :, h * dh:(h + 1) * dh] = (num * wide(rcp, dh)).astype(h_ref.dtype)
            wse_u = col(wse, ci).astype(BF16)
            dec = decay[:, ci:ci + 1]
            c_scr[u] = dec * c_st + _dot_tn(k[h] * wide(wse_u, dh), v[h])
            n_scr[u] = dec * n_st + _dot_tn(wse_u, k[h])[0:1, :]


def _bwd_chunk(c, n_ctx_chunks, n_chunks):
    return jnp.where(c < n_ctx_chunks, n_ctx_chunks - 1 - c, n_chunks - 1 + n_ctx_chunks - c)


def _hgrn_levels(l):
    return [1 << j for j in range(int(np.log2(l)))]


def _hgrn_q_rows(l, h, d):
    off = h if d == 0 else 0
    return [(k * 2 * h + off, k * 2 * h + off + h) for k in range(l // (2 * h))]


def _hgrn_masks(l):
    r = np.arange(l)[:, None]
    s = np.arange(l)[None, :]
    full, half = [[], []], [[], []]
    for d in (0, 1):
        full[d].append(r == s)
        for h in _hgrn_levels(l):
            same = (r // (2 * h)) == (s // (2 * h))
            m = same & ((r % (2 * h)) >= h) & ((s % (2 * h)) < h)
            m = m if d == 0 else m.T
            if h >= HGRN_COMPACT_MIN:
                half[d].append(np.concatenate([m[a:b] for a, b in _hgrn_q_rows(l, h, d)]))
            else:
                full[d].append(m)
    local = [(r // HGRN_LOCAL == s // HGRN_LOCAL) & (s <= r), (r // HGRN_LOCAL == s // HGRN_LOCAL) & (s >= r)]
    return np.array(full, np.float32), np.array(half, np.float32), np.array(local, np.float32)


def _hgrn_tri(l):
    r = np.arange(l)[:, None]
    s = np.arange(l)[None, :]
    return np.stack([(s <= r), (s >= r)]).astype(np.float32)


def _hgrn_split_rows(g_scr, l, h, d):
    assert h in (2, 4)
    off = h - 1 if d == 0 else h
    n = g_scr.shape[-1]
    sub = lax.broadcasted_iota(jnp.int32, (8, n), 0)
    pieces = []
    for grp in range(l // 8):
        lo = jnp.broadcast_to(g_scr[grp * 8 + off:grp * 8 + off + 1, :], (8, n))
        if h == 4:
            pieces.append(lo)
        else:
            hi = jnp.broadcast_to(g_scr[grp * 8 + 4 + off:grp * 8 + 4 + off + 1, :], (8, n))
            pieces.append(jnp.where(sub < 4, lo, hi))
    return jnp.concatenate(pieces, axis=0)


def _hgrn_block_decay(g, l):
    spans = [jnp.abs(g[b0:b0 + 1, :] - g[b0 + HGRN_LOCAL - 1:b0 + HGRN_LOCAL, :]) for b0 in range(0, l, HGRN_LOCAL)]
    return jnp.max(jnp.concatenate(spans, axis=0))


def _hgrn_chunk(dirs, mfull_ref, mhalf_ref, mlocal_ref, st_scr, g_scr, *, l, dh, fast, rows):
    heads = [slice(h * dh, (h + 1) * dh) for h in range(H_B)]
    q = [dirs[d][0][0, rows[d], :] for d in (0, 1)]
    kk = [dirs[d][2][0, rows[d], :] for d in (0, 1)]

    def scores(qs, ks, mask):
        return [_dot_nt(qs[:, hs], ks[:, hs]).astype(BF16) * mask for hs in heads]

    if fast:
        acc = []
        for d in (0, 1):
            g = g_scr.at[d]
            parts = []
            for b0 in range(0, l, HGRN_LOCAL):
                base = b0 if d == 0 else b0 + HGRN_LOCAL - 1
                parts.append(g[b0:b0 + HGRN_LOCAL, :] - g[base:base + 1, :])
            x = jnp.concatenate(parts, axis=0)
            acc.append(scores(q[d] * jnp.exp2(x).astype(BF16), kk[d] * jnp.exp2(-x).astype(BF16), mlocal_ref[d]))
        levels = [hl for hl in _hgrn_levels(l) if hl >= HGRN_LOCAL]
    else:
        acc = [scores(q[d], kk[d], mfull_ref[d, 0]) for d in (0, 1)]
        levels = _hgrn_levels(l)
    all_levels = _hgrn_levels(l)
    for hl in levels:
        n_full = 1 + all_levels.index(hl)
        n_half = all_levels.index(hl) - all_levels.index(HGRN_COMPACT_MIN)
        for d in (0, 1):
            g = g_scr.at[d]
            m_off = hl - 1 if d == 0 else hl
            q_rows = _hgrn_q_rows(l, hl, d)
            k_rows = _hgrn_q_rows(l, hl, 1 - d)
            if hl < HGRN_COMPACT_MIN:
                if hl == 1:
                    qs, ks = q[d] * jnp.exp2(dirs[d][3][0, rows[d], :]).astype(BF16), kk[d]
                else:
                    if hl < 8:
                        x = -jnp.abs(g[...] - _hgrn_split_rows(g, l, hl, d))
                    else:
                        parts = {}
                        for (q0, q1), (k0, k1) in zip(q_rows, k_rows):
                            gm = g[min(q0, k0) + m_off:min(q0, k0) + m_off + 1, :]
                            parts[q0] = g[q0:q1, :] - gm
                            parts[k0] = gm - g[k0:k1, :]
                        x = jnp.concatenate([parts[r0] for r0 in sorted(parts)], axis=0)
                    e = jnp.exp2(x).astype(BF16)
                    qs, ks = q[d] * e, kk[d] * e
                acc[d] = [a + t for a, t in zip(acc[d], scores(qs, ks, mfull_ref[d, n_full]))]
            else:
                xq, xk = [], []
                for (q0, q1), (k0, k1) in zip(q_rows, k_rows):
                    gm = g[min(q0, k0) + m_off:min(q0, k0) + m_off + 1, :]
                    xq.append(g[q0:q1, :] - gm)
                    xk.append(gm - g[k0:k1, :])
                eq = jnp.exp2(jnp.concatenate(xq, axis=0)).astype(BF16)
                ek = jnp.exp2(jnp.concatenate(xk, axis=0)).astype(BF16)
                qc = jnp.concatenate([q[d][q0:q1] for q0, q1 in q_rows], axis=0) * eq
                kparts = {q0: kk[d][q0:q1] for q0, q1 in q_rows}
                for n, (k0, k1) in enumerate(k_rows):
                    kparts[k0] = kk[d][k0:k1] * ek[n * hl:(n + 1) * hl]
                kt = jnp.concatenate([kparts[r0] for r0 in sorted(kparts)], axis=0)
                terms = scores(qc, kt, mhalf_ref[d, n_half])
                for i in range(H_B):
                    pieces = {k0: acc[d][i][k0:k1] for k0, k1 in k_rows}
                    for n, (q0, q1) in enumerate(q_rows):
                        pieces[q0] = acc[d][i][q0:q1] + terms[i][n * hl:(n + 1) * hl]
                    acc[d][i] = jnp.concatenate([pieces[r0] for r0 in sorted(pieces)], axis=0)
    for d in (0, 1):
        o_ref = dirs[d][4]
        v = dirs[d][1][0, rows[d], :]
        last = l - 1 if d == 0 else 0
        g_all = g_scr[d]
        g_l = g_scr[d, last:last + 1, :]
        qi = q[d] * jnp.exp2(g_all).astype(BF16)
        ks = kk[d] * jnp.exp2(g_l - g_all).astype(BF16)
        dec = jnp.exp2(g_l)
        for i, hs in enumerate(heads):
            u = d * H_B + i
            st = st_scr[u]
            o = _dot_nt(qi[:, hs], st.astype(BF16)) + jnp.dot(acc[d][i], v[:, hs], preferred_element_type=F32)
            o_ref[0, rows[d], hs] = o.astype(o_ref.dtype)
            st_scr[u] = st * dec[:, hs] + _dot_tn(v[:, hs], ks[:, hs])


def _hgrn_kernel(qf_ref, vf_ref, kf_ref, lff_ref, qb_ref, vb_ref, kb_ref, lfb_ref,
                 tri_ref, mfull_ref, mhalf_ref, mlocal_ref, of_ref, ob_ref, st_scr, g_scr, *, l, dh):
    @pl.when(pl.program_id(1) == 0)
    def _():
        st_scr[...] = jnp.zeros_like(st_scr)

    dirs = ((qf_ref, vf_ref, kf_ref, lff_ref, of_ref), (qb_ref, vb_ref, kb_ref, lfb_ref, ob_ref))
    n_sub = qf_ref.shape[1] // l
    rows = [(slice(j * l, (j + 1) * l), slice((n_sub - 1 - j) * l, (n_sub - j) * l)) for j in range(n_sub)]
    worst = None
    for j in range(n_sub):
        for d in (0, 1):
            g_scr[j, d] = _cumsum_mm(tri_ref[d], dirs[d][3][0, rows[j][d], :], terms=HGRN_CUMSUM_TERMS)
            span = _hgrn_block_decay(g_scr.at[j, d], l)
            worst = span if worst is None else jnp.maximum(worst, span)
    in_range = worst <= HGRN_LOCAL_MAX_LOG2

    def run(fast):
        for j in range(n_sub):
            _hgrn_chunk(dirs, mfull_ref, mhalf_ref, mlocal_ref, st_scr, g_scr.at[j], l=l, dh=dh, fast=fast,
                        rows=rows[j])

    pl.when(in_range)(functools.partial(run, True))
    pl.when(jnp.logical_not(in_range))(functools.partial(run, False))


def _scan_kernel(*refs, la, dh_a, lb, dh_b):
    m_in, h_in = refs[0:6], refs[6:18]
    ha_f, ha_b, hb_f, hb_b = refs[18:22]
    c_scr, n_scr, m_scr, st_scr, g_scr = refs[22:]

    @pl.when(pl.program_id(1) == 0)
    def _():
        for scr in (c_scr, n_scr, m_scr):
            scr[...] = jnp.zeros_like(scr)

    _mlstm_chunk(*m_in, ha_f, ha_b, c_scr, n_scr, m_scr, la=la, dh=dh_a)
    _hgrn_kernel(*h_in, hb_f, hb_b, st_scr, g_scr, l=lb, dh=dh_b)


def _scans(qk, mix, g, kdec, lf, ctx_len, d_a, d_b):
    nb, tb, _ = qk.shape
    la, lb = L_A, L_B
    dh_a, dh_b = d_a // H_A, d_b // H_B
    assert dh_b == LANES and la == HGRN_STEP_CHUNKS * lb and ctx_len % la == 0 and tb % la == 0
    nc, ncc = tb // la, ctx_len // la
    bw = functools.partial(_bwd_chunk, n_ctx_chunks=ncc, n_chunks=nc)
    kern = functools.partial(_scan_kernel, la=la, dh_a=dh_a, lb=lb, dh_b=dh_b)
    tri = jnp.asarray(_hgrn_tri(lb), BF16)
    mfull, mhalf, mlocal = (jnp.asarray(m, BF16) for m in _hgrn_masks(lb))
    ident = lambda c: c
    blk = lambda w, col, chunk: pl.BlockSpec((1, la, w), lambda b, c: (b, chunk(c), col))
    const = lambda a: pl.BlockSpec(a.shape, lambda b, c: (0,) * a.ndim)
    hshape = jax.ShapeDtypeStruct((nb, tb, d_a), BF16)
    oshape = jax.ShapeDtypeStruct((nb, tb, d_b), BF16)
    return pl.pallas_call(
        kern,
        out_shape=(hshape, hshape, oshape, oshape),
        grid=(nb, nc),
        in_specs=[blk(2 * d_a, 0, ident), blk(d_a, MIX_V, ident), blk(LANES, 0, ident),
                  blk(2 * d_a, 0, bw), blk(d_a, MIX_V, bw), blk(LANES, 0, bw),
                  blk(d_b, MIX_QB, ident), blk(d_b, MIX_IB, ident), blk(d_b, 0, ident), blk(d_b, 0, ident),
                  blk(d_b, MIX_QB, bw), blk(d_b, MIX_IB, bw), blk(d_b, 1, bw), blk(d_b, 1, bw),
                  const(tri), const(mfull), const(mhalf), const(mlocal)],
        out_specs=(blk(d_a, 0, ident), blk(d_a, 0, bw), blk(d_b, 0, ident), blk(d_b, 0, bw)),
        scratch_shapes=[pltpu.VMEM((2 * H_A, dh_a, dh_a), F32),
                        pltpu.VMEM((2 * H_A, 1, dh_a), F32),
                        pltpu.VMEM((8, LANES), F32),
                        pltpu.VMEM((2 * H_B, dh_b, dh_b), F32),
                        pltpu.VMEM((HGRN_STEP_CHUNKS, 2, lb, d_b), F32)],
        compiler_params=_cparams(("arbitrary", "arbitrary")),
        name="scans",
    )(qk, mix, g, qk, mix, g, mix, mix, kdec, lf, mix, mix, kdec, lf, tri, mfull, mhalf, mlocal)


def _head_rms(x, n_heads):
    dh = x.shape[-1] // n_heads
    outs = []
    for h in range(n_heads):
        xh = x[:, h * dh:(h + 1) * dh]
        outs.append(xh * lax.rsqrt(jnp.mean(xh * xh, axis=-1, keepdims=True) + EPS))
    return jnp.concatenate(outs, axis=-1)


def _merge_kernel(*refs, ctx_len, tm, n_batch, fuse_next):
    (x_ref, haf_ref, hab_ref, hbf_ref, hbb_ref, o_ref, za_ref, zb_ref, ma_ref, mb_ref,
     mod_ref, gha_ref, ghb_ref, wa_ref, wb_ref, wo_ref) = refs[:16]
    b = pl.program_id(0)
    d = x_ref.shape[-1]
    f32 = lambda ref: ref[0].astype(F32)
    h_a = (f32(haf_ref) + f32(hab_ref)) * f32(o_ref)
    y_a = _head_rms(h_a, H_A) * gha_ref[...] * f32(za_ref)
    y_b = _head_rms(f32(hbf_ref) + f32(hbb_ref), H_B) * ghb_ref[...] * f32(zb_ref)
    pa = jnp.dot(y_a.astype(BF16), wa_ref[...], preferred_element_type=F32)
    pb = jnp.dot(y_b.astype(BF16), wb_ref[...], preferred_element_type=F32)
    y = f32(ma_ref) * pa + f32(mb_ref) * pb
    br = jnp.dot(y.astype(BF16), wo_ref[...], preferred_element_type=F32)
    is_ctx = pl.program_id(1) * tm < ctx_len
    gate = mod_ref[pl.ds(jnp.where(is_ctx, n_batch, b), 1), 2 * d:3 * d]
    x_new = x_ref[0] + gate * br
    if fuse_next:
        nmod_ref, ngn_ref, nwg_ref, nbg_ref, out_ref, h_ref, g_ref = refs[16:]
        h_ref[0], g_ref[0] = _norm_modulate(x_new, is_ctx, b, nmod_ref, ngn_ref, nwg_ref, nbg_ref, n_batch)
        out_ref[0] = x_new
    else:
        gf_ref, out_ref = refs[16:]
        out_ref[0] = x_new * lax.rsqrt(jnp.mean(x_new * x_new, axis=-1, keepdims=True) + EPS) * gf_ref[...]


def _merge(xall, haf, hab, hbf, hbb, sig, silu, mod_l, gha, ghb, wa, wb, wo, ctx_len, nxt=None, g_final=None):
    nb, tb, d = xall.shape
    tm = MERGE_ROWS
    assert (nxt is None) != (g_final is None) and ctx_len % tm == 0 and tb % tm == 0
    kern = functools.partial(_merge_kernel, ctx_len=ctx_len, tm=tm, n_batch=nb, fuse_next=nxt is not None)
    row = pl.BlockSpec((1, tm, d), lambda b, i: (b, i, 0))
    pcol = lambda col: pl.BlockSpec((1, tm, d), lambda b, i: (b, i, col))
    full = lambda shape: pl.BlockSpec(shape, lambda b, i: (0,) * len(shape))
    in_specs = [row, row, row, row, row,
                pcol(SIG_O), pcol(SILU_ZA), pcol(SILU_ZB), pcol(SIG_MA), pcol(SIG_MB),
                full((8, 3 * d)), full((1, d)), full((1, d)),
                full((d, d)), full((d, d)), full((d, d))]
    args = [xall, haf, hab, hbf, hbb, sig, silu, silu, sig, sig, mod_l, gha, ghb, wa, wb, wo]
    out_shape = jax.ShapeDtypeStruct(xall.shape, F32)
    out_specs = row
    if nxt is not None:
        in_specs += [full((8, 3 * d)), full((1, d)), full((2, d, LANES)), full((1, LANES))]
        args += list(nxt)
        out_shape = (out_shape, jax.ShapeDtypeStruct((nb, tb, d), BF16),
                     jax.ShapeDtypeStruct((nb, tb, LANES), F32))
        out_specs = (row, row, pl.BlockSpec((1, tm, LANES), lambda b, i: (b, i, 0)))
    else:
        in_specs.append(full((1, d)))
        args.append(g_final)
        ctx_tiles = ctx_len // tm
        out_shape = jax.ShapeDtypeStruct((nb, tb - ctx_len, d), F32)
        out_specs = pl.BlockSpec((1, tm, d), lambda b, i: (b, jnp.maximum(i - ctx_tiles, 0), 0))
    return pl.pallas_call(
        kern,
        out_shape=out_shape,
        grid=(nb, tb // tm),
        in_specs=in_specs,
        out_specs=out_specs,
        compiler_params=_cparams(("arbitrary", "arbitrary")),
        name="merge",
    )(*args)


def kernel(x, c, ctx, c_ctx, w_ada, b_ada, g_norm, w_in, b_in, w_conv, b_conv, lb_logits,
           g_head_a, g_head_b, w_a, w_b, w_out, g_final):
    nb, seq, d = x.shape
    ctx_len = ctx.shape[1]
    depth = w_ada.shape[0]
    d_a = g_head_a.shape[-1]
    d_b = g_head_b.shape[-1]
    assert d_a == d and d_b == d and nb + 1 <= 8
    dh_a = d_a // H_A

    g0 = 5 * d
    g1 = g0 + N_GATES
    grp = lambda n: slice(n * d, (n + 1) * d) if n < 5 else slice(g1 + (n - 5) * d, g1 + (n - 4) * d)
    QK0, QK1, V, O, ZA, QB, IB, FF, FB, ZB, MA, MB = range(12)
    order = (V, QB, IB, O, MA, MB, QK0, QK1, ZA, ZB, FF, FB)
    w16 = w_in.astype(BF16)
    w_main = jnp.concatenate([w16[:, :, grp(n)] for n in order], axis=-1)
    b_main = jnp.concatenate([b_in[:, grp(n)] for n in order], axis=-1).astype(F32)
    wg = jnp.pad(w_in[:, :, g0:g1].astype(F32), ((0, 0), (0, 0), (0, LANES - N_GATES)))
    wg_hi = wg.astype(BF16)
    wg_lo = (wg - wg_hi.astype(F32)).astype(BF16)
    wg2 = jnp.stack([wg_hi, wg_lo], axis=1)
    bg = jnp.pad(b_in[:, g0:g1].astype(F32), ((0, 0), (0, LANES - N_GATES)))
    qk_scale = jnp.concatenate([jnp.ones((1, d_a), F32), jnp.full((1, d_a), dh_a ** -0.5, F32)], axis=-1)
    w9 = w_conv.reshape(depth, 9, 2 * d_a).astype(F32)

    cc = jnp.concatenate([c.astype(F32), c_ctx.astype(F32)[None, :],
                          jnp.zeros((8 - nb - 1, d), F32)], axis=0)
    mod = _ada(cc, w_ada.astype(F32), b_ada.astype(F32))
    lbs = _lbs(lb_logits)
    c_sig, c_raw, c_silu, c_dec = (int(n) for n in np.cumsum((N_MIX, N_SIG, N_RAW, N_SILU)))

    xall = jnp.concatenate([ctx.astype(F32), x.astype(F32)], axis=1)
    norm_args = lambda l: (mod[l], g_norm[l][None, :].astype(F32), wg2[l], bg[l][None, :])
    h16, g = _norm_gates(xall, *norm_args(0), ctx_len)
    for l in range(depth):
        bm = b_main[l][None, :]
        raw = _proj(h16, w_main[l], bm, c_raw, N_RAW, "raw")
        mix = _proj(h16, w_main[l], bm, 0, N_MIX, "cast")
        sig = _proj(h16, w_main[l], bm, c_sig, N_SIG, "sigmoid")
        silu = _proj(h16, w_main[l], bm, c_silu, N_SILU, "silu")
        lf, kdec = _proj(h16, w_main[l], bm, c_dec, N_DECAY, "decay", lbs[l][None, :])
        qk = _conv_silu(raw, w9[l], b_conv[l][None, :].astype(F32), qk_scale, ctx_len)
        haf, hab, hbf, hbb = _scans(qk, mix, g, kdec, lf, ctx_len, d_a, d_b)
        last = l + 1 == depth
        res = _merge(xall, haf, hab, hbf, hbb, sig, silu, mod[l], g_head_a[l][None, :].astype(F32),
                     g_head_b[l][None, :].astype(F32), w_a[l].astype(BF16), w_b[l].astype(BF16),
                     w_out[l].astype(BF16), ctx_len, nxt=None if last else norm_args(l + 1),
                     g_final=g_final[None, :].astype(F32) if last else None)
        xall, h16, g = (res, None, None) if last else res
    return xall.astype(x.dtype)
```

```python
import functools
import math

import numpy as np
import jax
import jax.numpy as jnp
from jax import lax
from jax.experimental import pallas as pl
from jax.experimental.pallas import tpu as pltpu

F32 = jnp.float32
BF16 = jnp.bfloat16

H_A = 4
H_B = 8
GRID_W = 64
EPS = 1e-6
NEG = -1e30
LOG2E = math.log2(math.e)
N_GATES = 4 * H_A
LANES = 128
L_A = 256
L_B = 128
HGRN_COMPACT_MIN = 16
HGRN_LOCAL = 32
HGRN_LOCAL_MAX_LOG2 = 100.0
HGRN_STEP_CHUNKS = 2
CONV_ROWS = 256
CONV_HALO = 128
HGRN_CUMSUM_TERMS = 2
MLSTM_CUMSUM_TERMS = 2
CONV_COLS = 2048
PROJ_ROWS = 256
MERGE_ROWS = 256
V7X_VMEM_LIMIT = 56 * 1024 * 1024
N_MIX, N_SIG, N_RAW, N_SILU, N_DECAY = 3, 3, 2, 2, 2
MIX_V, MIX_QB, MIX_IB = 0, 1, 2
SIG_O, SIG_MA, SIG_MB = 0, 1, 2
SILU_ZA, SILU_ZB = 0, 1


def _cparams(semantics):
    return pltpu.CompilerParams(dimension_semantics=semantics, vmem_limit_bytes=V7X_VMEM_LIMIT)


def _sigmoid(x):
    return 1.0 / (1.0 + jnp.exp(-x))


def _sigmoid_tanh(x):
    return 0.5 * jnp.tanh(0.5 * x) + 0.5


def _split3(x):
    hi = x.astype(BF16)
    r1 = x - hi.astype(F32)
    mid = r1.astype(BF16)
    lo = (r1 - mid.astype(F32)).astype(BF16)
    return hi, mid, lo


def _cumsum_mm(tri, x, terms=3):
    out = None
    for part in _split3(x)[:terms]:
        term = jnp.dot(tri, part, preferred_element_type=F32)
        out = term if out is None else out + term
    return out


def _dot_nt(a, b):
    return lax.dot_general(a, b, (((1,), (1,)), ((), ())), preferred_element_type=F32)


def _dot_tn(a, b):
    return lax.dot_general(a, b, (((0,), (0,)), ((), ())), preferred_element_type=F32)


def _ada_kernel(cc_ref, w_ref, b_ref, out_ref):
    s = cc_ref[...]
    s = s * _sigmoid(s)
    w = w_ref[0]
    s_hi = s.astype(BF16)
    s_lo = (s - s_hi.astype(F32)).astype(BF16)
    w_hi = w.astype(BF16)
    w_lo = (w - w_hi.astype(F32)).astype(BF16)
    d = lambda a, b: jnp.dot(a, b, preferred_element_type=F32)
    out_ref[0] = d(s_hi, w_hi) + d(s_lo, w_hi) + d(s_hi, w_lo) + b_ref[0]


def _ada(cc, w_ada, b_ada):
    depth, d, n3 = w_ada.shape
    tn = 512
    return pl.pallas_call(
        _ada_kernel,
        out_shape=jax.ShapeDtypeStruct((depth, 8, n3), F32),
        grid=(depth, n3 // tn),
        in_specs=[pl.BlockSpec((8, d), lambda l, j: (0, 0)),
                  pl.BlockSpec((1, d, tn), lambda l, j: (l, 0, j)),
                  pl.BlockSpec((1, 1, tn), lambda l, j: (l, 0, j))],
        out_specs=pl.BlockSpec((1, 8, tn), lambda l, j: (l, 0, j)),
        compiler_params=_cparams(("arbitrary", "arbitrary")),
        name="ada",
    )(cc, w_ada, b_ada.reshape(depth, 1, n3))


def _lbs_kernel(lg_ref, out_ref):
    x = lg_ref[...]
    depth = x.shape[0]
    m = jnp.max(x, axis=0, keepdims=True)
    e = jnp.exp(x - m)
    p = e / jnp.sum(e, axis=0, keepdims=True)
    acc = jnp.zeros_like(p[0:1])
    for l in range(depth):
        acc = acc + p[l:l + 1]
        out_ref[l:l + 1, :] = acc - p[0:1]


def _lbs(lb_logits):
    return pl.pallas_call(
        _lbs_kernel, out_shape=jax.ShapeDtypeStruct(lb_logits.shape, F32), name="lbs",
    )(lb_logits.astype(F32))


def _norm_modulate(x, is_ctx, b, mod_ref, gn_ref, wg_ref, bg_ref, n_batch):
    d = x.shape[-1]
    ms = jnp.mean(x * x, axis=-1, keepdims=True)
    y = x * lax.rsqrt(ms + EPS) * gn_ref[...]
    mb = mod_ref[pl.ds(b, 1), :]
    mc = mod_ref[pl.ds(n_batch, 1), :]
    shift = jnp.where(is_ctx, mc[:, 0:d], mb[:, 0:d])
    scale = jnp.where(is_ctx, mc[:, d:2 * d], mb[:, d:2 * d])
    h = y * (1.0 + scale) + shift
    hi = h.astype(BF16)
    lo = (h - hi.astype(F32)).astype(BF16)
    dd = lambda a, bb: jnp.dot(a, bb, preferred_element_type=F32)
    return hi, dd(hi, wg_ref[0]) + dd(lo, wg_ref[0]) + dd(hi, wg_ref[1]) + bg_ref[...]


def _norm0_kernel(ctx_ref, x_ref, mod_ref, gn_ref, wg_ref, bg_ref, xall_ref, h_ref, g_ref, *, n_batch):
    b = pl.program_id(0)
    is_ctx = pl.program_id(1) == 0
    x = jnp.where(is_ctx, ctx_ref[0], x_ref[0])
    xall_ref[0] = x
    h_ref[0], g_ref[0] = _norm_modulate(x, is_ctx, b, mod_ref, gn_ref, wg_ref, bg_ref, n_batch)


def _assemble_norm_gates(ctx, x, mod_l, gn, wg, bg):
    nb, ctx_len, d = ctx.shape
    seq = x.shape[1]
    tm = ctx_len
    assert seq % tm == 0
    tb = ctx_len + seq
    row = pl.BlockSpec((1, tm, d), lambda b, i: (b, i, 0))
    return pl.pallas_call(
        functools.partial(_norm0_kernel, n_batch=nb),
        out_shape=(jax.ShapeDtypeStruct((nb, tb, d), F32), jax.ShapeDtypeStruct((nb, tb, d), BF16),
                   jax.ShapeDtypeStruct((nb, tb, LANES), F32)),
        grid=(nb, tb // tm),
        in_specs=[pl.BlockSpec((1, tm, d), lambda b, i: (b, 0, 0)),
                  pl.BlockSpec((1, tm, d), lambda b, i: (b, jnp.maximum(i - 1, 0), 0)),
                  pl.BlockSpec((8, 3 * d), lambda b, i: (0, 0)),
                  pl.BlockSpec((1, d), lambda b, i: (0, 0)),
                  pl.BlockSpec((2, d, LANES), lambda b, i: (0, 0, 0)),
                  pl.BlockSpec((1, LANES), lambda b, i: (0, 0))],
        out_specs=(row, row, pl.BlockSpec((1, tm, LANES), lambda b, i: (b, i, 0))),
        compiler_params=_cparams(("arbitrary", "arbitrary")),
        name="assemble_norm_gates",
    )(ctx, x, mod_l, gn, wg, bg)


def _pick_tile(n, candidates):
    for c in candidates:
        if n % c == 0:
            return c
    raise ValueError(f"no tile for {n}")


def _proj_kernel(*refs, kind, rs):
    if kind == "decay":
        h_ref, w_ref, b_ref, lb_ref, lf_ref, k_ref = refs
    else:
        h_ref, w_ref, b_ref, out_ref = refs
    tm = h_ref.shape[1]
    for r0 in range(0, tm, rs):
        rows = slice(r0, r0 + rs)
        acc = jnp.dot(h_ref[0, rows, :], w_ref[...], preferred_element_type=F32) + b_ref[...]
        if kind == "raw":
            out_ref[0, rows, :] = acc
        elif kind == "cast":
            out_ref[0, rows, :] = acc.astype(BF16)
        elif kind == "sigmoid":
            out_ref[0, rows, :] = _sigmoid_tanh(acc).astype(BF16)
        elif kind == "silu":
            out_ref[0, rows, :] = (acc * _sigmoid_tanh(acc)).astype(BF16)
        else:
            lbv = lb_ref[...]
            t = jnp.exp2(jnp.abs(acc) * (-LOG2E))
            pos = acc >= 0.0
            rc = 1.0 / (1.0 + t)
            f = jnp.where(pos, 1.0 + lbv * t, t + lbv) * rc
            lf_ref[0, rows, :] = jnp.log(f) * LOG2E
            k_ref[0, rows, :] = ((1.0 - lbv) * jnp.where(pos, t, 1.0) * rc).astype(BF16)


def _proj(h16, w_main, b_main, col0, n_groups, kind, lb=None):
    nb, tb, d = h16.shape
    tm = _pick_tile(tb, (1280, 640, 256))
    tn = n_groups * d
    off = col0 // n_groups
    assert col0 % n_groups == 0
    kern = functools.partial(_proj_kernel, kind=kind, rs=PROJ_ROWS)
    in_specs = [pl.BlockSpec((1, tm, d), lambda b, i: (b, i, 0)),
                pl.BlockSpec((d, tn), lambda b, i: (0, off)),
                pl.BlockSpec((1, tn), lambda b, i: (0, off))]
    args = [h16, w_main, b_main]
    oblk = pl.BlockSpec((1, tm, tn), lambda b, i: (b, i, 0))
    shp = lambda dt: jax.ShapeDtypeStruct((nb, tb, tn), dt)
    if kind == "decay":
        in_specs.append(pl.BlockSpec((1, tn), lambda b, i: (0, 0)))
        args.append(jnp.concatenate([lb] * n_groups, axis=-1))
        out_shape, out_specs = (shp(F32), shp(BF16)), (oblk, oblk)
    else:
        out_shape, out_specs = shp(F32 if kind == "raw" else BF16), oblk
    return pl.pallas_call(
        kern,
        out_shape=out_shape,
        grid=(nb, tb // tm),
        in_specs=in_specs,
        out_specs=out_specs,
        compiler_params=_cparams(("arbitrary", "arbitrary")),
        name="proj_" + kind,
    )(*args)


def _conv_kernel(prev_ref, cur_ref, next_ref, w_ref, bc_ref, sc_ref, out_ref, *, n_tiles, tr, halo):
    i = pl.program_id(1)
    is_ctx = i == 0
    prev_ok = i >= 2
    next_ok = jnp.logical_and(i >= 1, i < n_tiles - 1)
    margin = GRID_W + 8
    full = jnp.concatenate([jnp.where(prev_ok, prev_ref[0, halo - margin:halo, :], 0.0), cur_ref[0],
                            jnp.where(next_ok, next_ref[0, 0:margin, :], 0.0)], axis=0)
    w = w_ref[...]
    sums = []
    for dc in (-1, 0, 1):
        acc = None
        for dr in (-1, 0, 1):
            tap = (dr + 1) * 3 + (dc + 1)
            wt = w[tap:tap + 1, :]
            if dr != 0:
                wt = jnp.where(is_ctx, 0.0, wt)
            start = margin + GRID_W * dr - 8
            term = wt * full[start:start + tr + 16, :]
            acc = term if acc is None else acc + term
        sums.append(acc)
    col = lax.broadcasted_iota(jnp.int32, (tr, 1), 0) % GRID_W
    rows = tr + 16
    left = pltpu.roll(sums[0], 1, axis=0)[8:8 + tr, :]
    right = pltpu.roll(sums[2], rows - 1, axis=0)[8:8 + tr, :]
    y = (sums[1][8:8 + tr, :] + jnp.where(jnp.logical_or(is_ctx, col != 0), left, 0.0)
         + jnp.where(jnp.logical_or(is_ctx, col != GRID_W - 1), right, 0.0) + bc_ref[...])
    out_ref[0] = (y * _sigmoid(y) * sc_ref[...]).astype(out_ref.dtype)


def _conv_silu(raw, w9, bconv, scale, ctx_len):
    nb, tb, n_qk = raw.shape
    tr, halo, tc = CONV_ROWS, CONV_HALO, CONV_COLS
    assert ctx_len == tr and tb % tr == 0 and tr % GRID_W == 0 and halo % GRID_W == 0
    n_tiles = tb // tr
    per = tr // halo
    n_halo = tb // halo
    kern = functools.partial(_conv_kernel, n_tiles=n_tiles, tr=tr, halo=halo)
    return pl.pallas_call(
        kern,
        out_shape=jax.ShapeDtypeStruct((nb, tb, n_qk), BF16),
        grid=(nb, n_tiles, n_qk // tc),
        in_specs=[pl.BlockSpec((1, halo, tc), lambda b, i, j: (b, jnp.maximum(i * per - 1, 0), j)),
                  pl.BlockSpec((1, tr, tc), lambda b, i, j: (b, i, j)),
                  pl.BlockSpec((1, halo, tc), lambda b, i, j: (b, jnp.minimum((i + 1) * per, n_halo - 1), j)),
                  pl.BlockSpec((9, tc), lambda b, i, j: (0, j)),
                  pl.BlockSpec((1, tc), lambda b, i, j: (0, j)),
                  pl.BlockSpec((1, tc), lambda b, i, j: (0, j))],
        out_specs=pl.BlockSpec((1, tr, tc), lambda b, i, j: (b, i, j)),
        compiler_params=_cparams(("arbitrary", "arbitrary", "arbitrary")),
        name="conv_silu",
    )(raw, raw, raw, w9, bconv, scale)


def _mlstm_chunk(qkf_ref, vf_ref, gf_ref, qkb_ref, vb_ref, gb_ref, hf_ref, hb_ref,
                 c_scr, n_scr, m_scr, *, la, dh):
    r = lax.broadcasted_iota(jnp.int32, (la, la), 0)
    s = lax.broadcasted_iota(jnp.int32, (la, la), 1)
    ones_blk = jnp.ones((la, LANES), BF16)
    wide = lambda x, n: jnp.concatenate([x] * (n // LANES), axis=1)
    dirs = ((qkf_ref, vf_ref, gf_ref, hf_ref), (qkb_ref, vb_ref, gb_ref, hb_ref))
    for d, (qk_ref, v_ref, g_ref, h_ref) in enumerate(dirs):
        seen = (s <= r) if d == 0 else (s >= r)
        tri = jnp.where(seen, 1.0, 0.0).astype(BF16)
        g = g_ref[0]
        lsig = jnp.minimum(g, 0.0) - jnp.log(1.0 + jnp.exp(-jnp.abs(g)))
        bc = _cumsum_mm(tri, lsig, terms=MLSTM_CUMSUM_TERMS)
        b2 = pltpu.roll(bc, LANES - H_A, axis=1) * LOG2E
        r2 = g * LOG2E - b2
        r2_t = r2.T
        last = la - 1 if d == 0 else 0
        b2_l = b2[last:last + 1, :]
        m2 = m_scr[d:d + 1, :]
        ws2 = r2 + b2_l
        m2_new = jnp.maximum(b2_l + m2, jnp.max(ws2, axis=0, keepdims=True))
        decay = jnp.exp2(b2_l + m2 - m2_new)
        wse = jnp.exp2(ws2 - m2_new)
        m_scr[d:d + 1, :] = m2_new
        cols = [2 * d * H_A + h for h in range(H_A)]
        q = [qk_ref[0, :, h * dh:(h + 1) * dh] for h in range(H_A)]
        k = [qk_ref[0, :, (H_A + h) * dh:(H_A + h + 1) * dh] for h in range(H_A)]
        v = [v_ref[0, :, h * dh:(h + 1) * dh] for h in range(H_A)]
        col = lambda x, ci: jnp.broadcast_to(x[:, ci:ci + 1], (la, LANES))
        rm = [jnp.where(seen, r2_t[ci:ci + 1, :], NEG) for ci in cols]
        mx = [jnp.broadcast_to(jnp.maximum(m2[:, ci:ci + 1], jnp.max(rm_h, axis=-1, keepdims=True)),
                               (la, LANES)) for ci, rm_h in zip(cols, rm)]
        s_mat = [_dot_nt(q[h], k[h]) for h in range(H_A)]
        for h, ci in enumerate(cols):
            u = d * H_A + h
            w16 = (s_mat[h] * jnp.exp2(rm[h] - wide(mx[h], la))).astype(BF16)
            a_in = jnp.exp2(m2[:, ci:ci + 1] - mx[h])
            em = jnp.exp2(-(col(b2, ci) + mx[h]))
            c_st = c_scr[u]
            n_st = n_scr[u]
            qa = q[h] * wide(a_in.astype(BF16), dh)
            num = jnp.dot(jnp.concatenate([w16, qa], axis=1),
                          jnp.concatenate([v[h], c_st.astype(BF16)], axis=0), preferred_element_type=F32)
            qn = _dot_nt(q[h], jnp.broadcast_to(n_st, (LANES, dh)).astype(BF16))
            den = a_in * qn + jnp.dot(w16, ones_blk, preferred_element_type=F32)
            rcp = 1.0 / jnp.maximum(jnp.abs(den), em)
            h_ref[0, :, h * dh:(h + 1) * dh] = (num * wide(rcp, dh)).astype(h_ref.dtype)
            wse_u = col(wse, ci).astype(BF16)
            dec = decay[:, ci:ci + 1]
            c_scr[u] = dec * c_st + _dot_tn(k[h] * wide(wse_u, dh), v[h])
            n_scr[u] = dec * n_st + _dot_tn(wse_u, k[h])[0:1, :]


def _bwd_chunk(c, n_ctx_chunks, n_chunks):
    return jnp.where(c < n_ctx_chunks, n_ctx_chunks - 1 - c, n_chunks - 1 + n_ctx_chunks - c)


def _hgrn_levels(l):
    return [1 << j for j in range(int(np.log2(l)))]


def _hgrn_q_rows(l, h, d):
    off = h if d == 0 else 0
    return [(k * 2 * h + off, k * 2 * h + off + h) for k in range(l // (2 * h))]


def _hgrn_masks(l):
    r = np.arange(l)[:, None]
    s = np.arange(l)[None, :]
    full, half = [[], []], [[], []]
    for d in (0, 1):
        full[d].append(r == s)
        for h in _hgrn_levels(l):
            same = (r // (2 * h)) == (s // (2 * h))
            m = same & ((r % (2 * h)) >= h) & ((s % (2 * h)) < h)
            m = m if d == 0 else m.T
            if h >= HGRN_COMPACT_MIN:
                half[d].append(np.concatenate([m[a:b] for a, b in _hgrn_q_rows(l, h, d)]))
            else:
                full[d].append(m)
    local = [(r // HGRN_LOCAL == s // HGRN_LOCAL) & (s <= r), (r // HGRN_LOCAL == s // HGRN_LOCAL) & (s >= r)]
    return np.array(full, np.float32), np.array(half, np.float32), np.array(local, np.float32)


def _hgrn_tri(l):
    r = np.arange(l)[:, None]
    s = np.arange(l)[None, :]
    return np.stack([(s <= r), (s >= r)]).astype(np.float32)


def _hgrn_split_rows(g_scr, l, h, d):
    assert h in (2, 4)
    off = h - 1 if d == 0 else h
    n = g_scr.shape[-1]
    sub = lax.broadcasted_iota(jnp.int32, (8, n), 0)
    pieces = []
    for grp in range(l // 8):
        lo = jnp.broadcast_to(g_scr[grp * 8 + off:grp * 8 + off + 1, :], (8, n))
        if h == 4:
            pieces.append(lo)
        else:
            hi = jnp.broadcast_to(g_scr[grp * 8 + 4 + off:grp * 8 + 4 + off + 1, :], (8, n))
            pieces.append(jnp.where(sub < 4, lo, hi))
    return jnp.concatenate(pieces, axis=0)


def _hgrn_block_decay(g, l):
    spans = [jnp.abs(g[b0:b0 + 1, :] - g[b0 + HGRN_LOCAL - 1:b0 + HGRN_LOCAL, :]) for b0 in range(0, l, HGRN_LOCAL)]
    return jnp.max(jnp.concatenate(spans, axis=0))


def _hgrn_chunk(dirs, mfull_ref, mhalf_ref, mlocal_ref, st_scr, g_scr, *, l, dh, fast, rows):
    heads = [slice(h * dh, (h + 1) * dh) for h in range(H_B)]
    q = [dirs[d][0][0, rows[d], :] for d in (0, 1)]
    kk = [dirs[d][2][0, rows[d], :] for d in (0, 1)]

    def scores(qs, ks, mask):
        return [_dot_nt(qs[:, hs], ks[:, hs]).astype(BF16) * mask for hs in heads]

    if fast:
        acc = []
        for d in (0, 1):
            g = g_scr.at[d]
            parts = []
            for b0 in range(0, l, HGRN_LOCAL):
                base = b0 if d == 0 else b0 + HGRN_LOCAL - 1
                parts.append(g[b0:b0 + HGRN_LOCAL, :] - g[base:base + 1, :])
            x = jnp.concatenate(parts, axis=0)
            acc.append(scores(q[d] * jnp.exp2(x).astype(BF16), kk[d] * jnp.exp2(-x).astype(BF16), mlocal_ref[d]))
        levels = [hl for hl in _hgrn_levels(l) if hl >= HGRN_LOCAL]
    else:
        acc = [scores(q[d], kk[d], mfull_ref[d, 0]) for d in (0, 1)]
        levels = _hgrn_levels(l)
    all_levels = _hgrn_levels(l)
    for hl in levels:
        n_full = 1 + all_levels.index(hl)
        n_half = all_levels.index(hl) - all_levels.index(HGRN_COMPACT_MIN)
        for d in (0, 1):
            g = g_scr.at[d]
            m_off = hl - 1 if d == 0 else hl
            q_rows = _hgrn_q_rows(l, hl, d)
            k_rows = _hgrn_q_rows(l, hl, 1 - d)
            if hl < HGRN_COMPACT_MIN:
                if hl == 1:
                    qs, ks = q[d] * jnp.exp2(dirs[d][3][0, rows[d], :]).astype(BF16), kk[d]
                else:
                    if hl < 8:
                        x = -jnp.abs(g[...] - _hgrn_split_rows(g, l, hl, d))
                    else:
                        parts = {}
                        for (q0, q1), (k0, k1) in zip(q_rows, k_rows):
                            gm = g[min(q0, k0) + m_off:min(q0, k0) + m_off + 1, :]
                            parts[q0] = g[q0:q1, :] - gm
                            parts[k0] = gm - g[k0:k1, :]
                        x = jnp.concatenate([parts[r0] for r0 in sorted(parts)], axis=0)
                    e = jnp.exp2(x).astype(BF16)
                    qs, ks = q[d] * e, kk[d] * e
                acc[d] = [a + t for a, t in zip(acc[d], scores(qs, ks, mfull_ref[d, n_full]))]
            else:
                xq, xk = [], []
                for (q0, q1), (k0, k1) in zip(q_rows, k_rows):
                    gm = g[min(q0, k0) + m_off:min(q0, k0) + m_off + 1, :]
                    xq.append(g[q0:q1, :] - gm)
                    xk.append(gm - g[k0:k1, :])
                eq = jnp.exp2(jnp.concatenate(xq, axis=0)).astype(BF16)
                ek = jnp.exp2(jnp.concatenate(xk, axis=0)).astype(BF16)
                qc = jnp.concatenate([q[d][q0:q1] for q0, q1 in q_rows], axis=0) * eq
                kparts = {q0: kk[d][q0:q1] for q0, q1 in q_rows}
                for n, (k0, k1) in enumerate(k_rows):
                    kparts[k0] = kk[d][k0:k1] * ek[n * hl:(n + 1) * hl]
                kt = jnp.concatenate([kparts[r0] for r0 in sorted(kparts)], axis=0)
                terms = scores(qc, kt, mhalf_ref[d, n_half])
                for i in range(H_B):
                    pieces = {k0: acc[d][i][k0:k1] for k0, k1 in k_rows}
                    for n, (q0, q1) in enumerate(q_rows):
                        pieces[q0] = acc[d][i][q0:q1] + terms[i][n * hl:(n + 1) * hl]
                    acc[d][i] = jnp.concatenate([pieces[r0] for r0 in sorted(pieces)], axis=0)
    for d in (0, 1):
        o_ref = dirs[d][4]
        v = dirs[d][1][0, rows[d], :]
        last = l - 1 if d == 0 else 0
        g_all = g_scr[d]
        g_l = g_scr[d, last:last + 1, :]
        qi = q[d] * jnp.exp2(g_all).astype(BF16)
        ks = kk[d] * jnp.exp2(g_l - g_all).astype(BF16)
        dec = jnp.exp2(g_l)
        for i, hs in enumerate(heads):
            u = d * H_B + i
            st = st_scr[u]
            o = _dot_nt(qi[:, hs], st.astype(BF16)) + jnp.dot(acc[d][i], v[:, hs], preferred_element_type=F32)
            o_ref[0, rows[d], hs] = o.astype(o_ref.dtype)
            st_scr[u] = st * dec[:, hs] + _dot_tn(v[:, hs], ks[:, hs])


def _hgrn_kernel(qf_ref, vf_ref, kf_ref, lff_ref, qb_ref, vb_ref, kb_ref, lfb_ref,
                 tri_ref, mfull_ref, mhalf_ref, mlocal_ref, of_ref, ob_ref, st_scr, g_scr, *, l, dh):
    @pl.when(pl.program_id(1) == 0)
    def _():
        st_scr[...] = jnp.zeros_like(st_scr)

    dirs = ((qf_ref, vf_ref, kf_ref, lff_ref, of_ref), (qb_ref, vb_ref, kb_ref, lfb_ref, ob_ref))
    n_sub = qf_ref.shape[1] // l
    rows = [(slice(j * l, (j + 1) * l), slice((n_sub - 1 - j) * l, (n_sub - j) * l)) for j in range(n_sub)]
    worst = None
    for j in range(n_sub):
        for d in (0, 1):
            g_scr[j, d] = _cumsum_mm(tri_ref[d], dirs[d][3][0, rows[j][d], :], terms=HGRN_CUMSUM_TERMS)
            span = _hgrn_block_decay(g_scr.at[j, d], l)
            worst = span if worst is None else jnp.maximum(worst, span)
    in_range = worst <= HGRN_LOCAL_MAX_LOG2

    def run(fast):
        for j in range(n_sub):
            _hgrn_chunk(dirs, mfull_ref, mhalf_ref, mlocal_ref, st_scr, g_scr.at[j], l=l, dh=dh, fast=fast,
                        rows=rows[j])

    pl.when(in_range)(functools.partial(run, True))
    pl.when(jnp.logical_not(in_range))(functools.partial(run, False))


def _scan_kernel(*refs, la, dh_a, lb, dh_b):
    m_in, h_in = refs[0:6], refs[6:18]
    ha_f, ha_b, hb_f, hb_b = refs[18:22]
    c_scr, n_scr, m_scr, st_scr, g_scr = refs[22:]

    @pl.when(pl.program_id(1) == 0)
    def _():
        for scr in (c_scr, n_scr, m_scr):
            scr[...] = jnp.zeros_like(scr)

    _mlstm_chunk(*m_in, ha_f, ha_b, c_scr, n_scr, m_scr, la=la, dh=dh_a)
    _hgrn_kernel(*h_in, hb_f, hb_b, st_scr, g_scr, l=lb, dh=dh_b)


def _scans(qk, mix, g, kdec, lf, ctx_len, d_a, d_b):
    nb, tb, _ = qk.shape
    la, lb = L_A, L_B
    dh_a, dh_b = d_a // H_A, d_b // H_B
    assert dh_b == LANES and la == HGRN_STEP_CHUNKS * lb and ctx_len % la == 0 and tb % la == 0
    nc, ncc = tb // la, ctx_len // la
    bw = functools.partial(_bwd_chunk, n_ctx_chunks=ncc, n_chunks=nc)
    kern = functools.partial(_scan_kernel, la=la, dh_a=dh_a, lb=lb, dh_b=dh_b)
    tri = jnp.asarray(_hgrn_tri(lb), BF16)
    mfull, mhalf, mlocal = (jnp.asarray(m, BF16) for m in _hgrn_masks(lb))
    ident = lambda c: c
    blk = lambda w, col, chunk: pl.BlockSpec((1, la, w), lambda b, c: (b, chunk(c), col))
    const = lambda a: pl.BlockSpec(a.shape, lambda b, c: (0,) * a.ndim)
    hshape = jax.ShapeDtypeStruct((nb, tb, d_a), BF16)
    oshape = jax.ShapeDtypeStruct((nb, tb, d_b), BF16)
    return pl.pallas_call(
        kern,
        out_shape=(hshape, hshape, oshape, oshape),
        grid=(nb, nc),
        in_specs=[blk(2 * d_a, 0, ident), blk(d_a, MIX_V, ident), blk(LANES, 0, ident),
                  blk(2 * d_a, 0, bw), blk(d_a, MIX_V, bw), blk(LANES, 0, bw),
                  blk(d_b, MIX_QB, ident), blk(d_b, MIX_IB, ident), blk(d_b, 0, ident), blk(d_b, 0, ident),
                  blk(d_b, MIX_QB, bw), blk(d_b, MIX_IB, bw), blk(d_b, 1, bw), blk(d_b, 1, bw),
                  const(tri), const(mfull), const(mhalf), const(mlocal)],
        out_specs=(blk(d_a, 0, ident), blk(d_a, 0, bw), blk(d_b, 0, ident), blk(d_b, 0, bw)),
        scratch_shapes=[pltpu.VMEM((2 * H_A, dh_a, dh_a), F32),
                        pltpu.VMEM((2 * H_A, 1, dh_a), F32),
                        pltpu.VMEM((8, LANES), F32),
                        pltpu.VMEM((2 * H_B, dh_b, dh_b), F32),
                        pltpu.VMEM((HGRN_STEP_CHUNKS, 2, lb, d_b), F32)],
        compiler_params=_cparams(("arbitrary", "arbitrary")),
        name="scans",
    )(qk, mix, g, qk, mix, g, mix, mix, kdec, lf, mix, mix, kdec, lf, tri, mfull, mhalf, mlocal)


def _head_rms(x, n_heads):
    dh = x.shape[-1] // n_heads
    outs = []
    for h in range(n_heads):
        xh = x[:, h * dh:(h + 1) * dh]
        outs.append(xh * lax.rsqrt(jnp.mean(xh * xh, axis=-1, keepdims=True) + EPS))
    return jnp.concatenate(outs, axis=-1)


def _merge_kernel(*refs, ctx_len, tm, n_batch, fuse_next):
    (x_ref, haf_ref, hab_ref, hbf_ref, hbb_ref, o_ref, za_ref, zb_ref, ma_ref, mb_ref,
     mod_ref, gha_ref, ghb_ref, wa_ref, wb_ref, wo_ref) = refs[:16]
    b = pl.program_id(0)
    d = x_ref.shape[-1]
    f32 = lambda ref: ref[0].astype(F32)
    h_a = (f32(haf_ref) + f32(hab_ref)) * f32(o_ref)
    y_a = _head_rms(h_a, H_A) * gha_ref[...] * f32(za_ref)
    y_b = _head_rms(f32(hbf_ref) + f32(hbb_ref), H_B) * ghb_ref[...] * f32(zb_ref)
    pa = jnp.dot(y_a.astype(BF16), wa_ref[...], preferred_element_type=F32)
    pb = jnp.dot(y_b.astype(BF16), wb_ref[...], preferred_element_type=F32)
    y = f32(ma_ref) * pa + f32(mb_ref) * pb
    br = jnp.dot(y.astype(BF16), wo_ref[...], preferred_element_type=F32)
    is_ctx = pl.program_id(1) * tm + lax.broadcasted_iota(jnp.int32, (tm, 1), 0) < ctx_len
    gate = jnp.where(is_ctx, mod_ref[pl.ds(n_batch, 1), 2 * d:3 * d], mod_ref[pl.ds(b, 1), 2 * d:3 * d])
    x_new = x_ref[0] + gate * br
    if fuse_next:
        nmod_ref, ngn_ref, nwg_ref, nbg_ref, out_ref, h_ref, g_ref = refs[16:]
        h_ref[0], g_ref[0] = _norm_modulate(x_new, is_ctx, b, nmod_ref, ngn_ref, nwg_ref, nbg_ref, n_batch)
        out_ref[0] = x_new
    else:
        gf_ref, out_ref = refs[16:]
        out_ref[0] = x_new * lax.rsqrt(jnp.mean(x_new * x_new, axis=-1, keepdims=True) + EPS) * gf_ref[...]


def _merge(xall, haf, hab, hbf, hbb, sig, silu, mod_l, gha, ghb, wa, wb, wo, ctx_len, nxt=None, g_final=None):
    nb, tb, d = xall.shape
    tm = MERGE_ROWS
    assert (nxt is None) != (g_final is None) and ctx_len % tm == 0 and tb % tm == 0
    kern = functools.partial(_merge_kernel, ctx_len=ctx_len, tm=tm, n_batch=nb, fuse_next=nxt is not None)
    row = pl.BlockSpec((1, tm, d), lambda b, i: (b, i, 0))
    pcol = lambda col: pl.BlockSpec((1, tm, d), lambda b, i: (b, i, col))
    full = lambda shape: pl.BlockSpec(shape, lambda b, i: (0,) * len(shape))
    in_specs = [row, row, row, row, row,
                pcol(SIG_O), pcol(SILU_ZA), pcol(SILU_ZB), pcol(SIG_MA), pcol(SIG_MB),
                full((8, 3 * d)), full((1, d)), full((1, d)),
                full((d, d)), full((d, d)), full((d, d))]
    args = [xall, haf, hab, hbf, hbb, sig, silu, silu, sig, sig, mod_l, gha, ghb, wa, wb, wo]
    out_shape = jax.ShapeDtypeStruct(xall.shape, F32)
    out_specs = row
    if nxt is not None:
        in_specs += [full((8, 3 * d)), full((1, d)), full((2, d, LANES)), full((1, LANES))]
        args += list(nxt)
        out_shape = (out_shape, jax.ShapeDtypeStruct((nb, tb, d), BF16),
                     jax.ShapeDtypeStruct((nb, tb, LANES), F32))
        out_specs = (row, row, pl.BlockSpec((1, tm, LANES), lambda b, i: (b, i, 0)))
    else:
        in_specs.append(full((1, d)))
        args.append(g_final)
        ctx_tiles = ctx_len // tm
        out_shape = jax.ShapeDtypeStruct((nb, tb - ctx_len, d), F32)
        out_specs = pl.BlockSpec((1, tm, d), lambda b, i: (b, jnp.maximum(i - ctx_tiles, 0), 0))
    return pl.pallas_call(
        kern,
        out_shape=out_shape,
        grid=(nb, tb // tm),
        in_specs=in_specs,
        out_specs=out_specs,
        compiler_params=_cparams(("arbitrary", "arbitrary")),
        name="merge",
    )(*args)


def kernel(x, c, ctx, c_ctx, w_ada, b_ada, g_norm, w_in, b_in, w_conv, b_conv, lb_logits,
           g_head_a, g_head_b, w_a, w_b, w_out, g_final):
    nb, seq, d = x.shape
    ctx_len = ctx.shape[1]
    depth = w_ada.shape[0]
    d_a = g_head_a.shape[-1]
    d_b = g_head_b.shape[-1]
    assert d_a == d and d_b == d and nb + 1 <= 8
    dh_a = d_a // H_A

    g0 = 5 * d
    g1 = g0 + N_GATES
    grp = lambda n: slice(n * d, (n + 1) * d) if n < 5 else slice(g1 + (n - 5) * d, g1 + (n - 4) * d)
    QK0, QK1, V, O, ZA, QB, IB, FF, FB, ZB, MA, MB = range(12)
    order = (V, QB, IB, O, MA, MB, QK0, QK1, ZA, ZB, FF, FB)
    w16 = w_in.astype(BF16)
    w_main = jnp.concatenate([w16[:, :, grp(n)] for n in order], axis=-1)
    b_main = jnp.concatenate([b_in[:, grp(n)] for n in order], axis=-1).astype(F32)
    wg = jnp.pad(w_in[:, :, g0:g1].astype(F32), ((0, 0), (0, 0), (0, LANES - N_GATES)))
    wg_hi = wg.astype(BF16)
    wg_lo = (wg - wg_hi.astype(F32)).astype(BF16)
    wg2 = jnp.stack([wg_hi, wg_lo], axis=1)
    bg = jnp.pad(b_in[:, g0:g1].astype(F32), ((0, 0), (0, LANES - N_GATES)))
    qk_scale = jnp.concatenate([jnp.ones((1, d_a), F32), jnp.full((1, d_a), dh_a ** -0.5, F32)], axis=-1)
    w9 = w_conv.reshape(depth, 9, 2 * d_a).astype(F32)

    cc = jnp.concatenate([c.astype(F32), c_ctx.astype(F32)[None, :],
                          jnp.zeros((8 - nb - 1, d), F32)], axis=0)
    mod = _ada(cc, w_ada.astype(F32), b_ada.astype(F32))
    lbs = _lbs(lb_logits)
    c_sig, c_raw, c_silu, c_dec = (int(n) for n in np.cumsum((N_MIX, N_SIG, N_RAW, N_SILU)))

    norm_args = lambda l: (mod[l], g_norm[l][None, :].astype(F32), wg2[l], bg[l][None, :])
    xall, h16, g = _assemble_norm_gates(ctx.astype(F32), x.astype(F32), *norm_args(0))
    for l in range(depth):
        bm = b_main[l][None, :]
        raw = _proj(h16, w_main[l], bm, c_raw, N_RAW, "raw")
        mix = _proj(h16, w_main[l], bm, 0, N_MIX, "cast")
        sig = _proj(h16, w_main[l], bm, c_sig, N_SIG, "sigmoid")
        silu = _proj(h16, w_main[l], bm, c_silu, N_SILU, "silu")
        lf, kdec = _proj(h16, w_main[l], bm, c_dec, N_DECAY, "decay", lbs[l][None, :])
        qk = _conv_silu(raw, w9[l], b_conv[l][None, :].astype(F32), qk_scale, ctx_len)
        haf, hab, hbf, hbb = _scans(qk, mix, g, kdec, lf, ctx_len, d_a, d_b)
        last = l + 1 == depth
        res = _merge(xall, haf, hab, hbf, hbb, sig, silu, mod[l], g_head_a[l][None, :].astype(F32),
                     g_head_b[l][None, :].astype(F32), w_a[l].astype(BF16), w_b[l].astype(BF16),
                     w_out[l].astype(BF16), ctx_len, nxt=None if last else norm_args(l + 1),
                     g_final=g_final[None, :].astype(F32) if last else None)
        xall, h16, g = (res, None, None) if last else res
    return xall.astype(x.dtype)
```
